```python
import functools
import jax
import jax.numpy as jnp
from jax import lax
import numpy as np

D_MODEL = 2048
BATCH = 2
SEQ = 8192
DEPTH = 1
DEC_BATCH = 32
DEC_SEQ = 1
PAST_LEN = 16384
PAGE_SIZE = 128

A_GROUPS = 8
A_GROUP_DIM = 128
A_WIDTH = A_GROUPS * A_GROUP_DIM
CHUNK = 128
HEAD_DIM = 128
DIL_GROUPS = ((128, 1), (512, 4), (2048, 16))
HEADS_PER_GROUP = 4
N_HEADS = HEADS_PER_GROUP * len(DIL_GROUPS)
B_WIDTH = N_HEADS * HEAD_DIM
B_OUT = HEADS_PER_GROUP * HEAD_DIM
Q_BLOCK = 128
ALIBI_SLOPES = tuple(2.0 ** (-8.0 * (h + 1) / N_HEADS) for h in range(N_HEADS))
SPLITS = (A_WIDTH, 2 * A_WIDTH, 2 * A_WIDTH + B_WIDTH, 2 * A_WIDTH + 2 * B_WIDTH,
          2 * A_WIDTH + 3 * B_WIDTH, 2 * A_WIDTH + 3 * B_WIDTH + D_MODEL)
IN_COLS = 2 * A_WIDTH + 3 * B_WIDTH + 2 * D_MODEL
N_EXPERTS = 32
TOP_K = 4
D_FF = D_MODEL
SWIGLU_LIMIT = 7.0
SWIGLU_ALPHA = 1.702
MOE_BLOCK = 128
N_ADA = 6
EPS = 1e-6

kernel_name = 'hybrid_gmlp_dilated_swa_moe_step'


def rms_norm(x, g):
    xf = x.astype(jnp.float32)
    y = xf * lax.rsqrt(jnp.mean(xf * xf, axis=-1, keepdims=True) + EPS)
    return y.astype(x.dtype) * g


def layer_norm(x, g, b):
    xf = x.astype(jnp.float32)
    mu = jnp.mean(xf, axis=-1, keepdims=True)
    var = jnp.mean(jnp.square(xf - mu), axis=-1, keepdims=True)
    return ((xf - mu) * lax.rsqrt(var + EPS)).astype(x.dtype) * g + b


def ada_params(c, w_ada, b_ada):
    m = jax.nn.silu(c) @ w_ada + b_ada
    return jnp.split(m[:, None, :], N_ADA, axis=-1)


def chunk_spatial_gate(u, v, w_s, b_s):
    bsz, n = v.shape[:2]
    pad = (-n) % CHUNK
    vc = jnp.pad(v, ((0, 0), (0, pad), (0, 0), (0, 0))).reshape(bsz, -1, CHUNK, A_GROUPS, A_GROUP_DIM)
    w = w_s * jnp.tril(jnp.ones((CHUNK, CHUNK), w_s.dtype))
    mixed = jnp.einsum('gts,bnsgc->bntgc', w, vc) + b_s.T[None, None, :, :, None]
    return u * mixed.reshape(bsz, -1, A_GROUPS, A_GROUP_DIM)[:, :n]


def _dilated_block(q, k, v, q_idx, dil, n_keys, slopes):
    dist = jnp.arange(n_keys, dtype=jnp.int32) * dil
    idx = q_idx[:, None] - dist[None, :]
    valid = idx >= 0
    idx = jnp.maximum(idx, 0)
    kg = k[:, idx]
    vg = v[:, idx]
    s = jnp.einsum('bqhd,bqkhd->bhqk', q, kg, preferred_element_type=jnp.float32) * (HEAD_DIM ** -0.5)
    s = s - slopes[None, :, None, None] * dist.astype(jnp.float32)[None, None, None, :]
    s = jnp.where(valid[None, None], s, -jnp.inf)
    m = jnp.max(s, axis=-1, keepdims=True)
    p = jnp.exp(s - m)
    l = jnp.sum(p, axis=-1, keepdims=True)
    o = jnp.einsum('bhqk,bqkhd->bqhd', (p / l).astype(v.dtype), vg)
    lse = (m + jnp.log(l))[..., 0]
    return o, jnp.swapaxes(lse, 1, 2)


def dilated_attention(q, k, v, q_idx, dil, n_keys, slopes):
    nq = q.shape[1]
    if nq <= Q_BLOCK:
        return _dilated_block(q, k, v, q_idx, dil, n_keys, slopes)
    bsz = q.shape[0]
    pad = (-nq) % Q_BLOCK
    qp = jnp.pad(q, ((0, 0), (0, pad), (0, 0), (0, 0)))
    ip = jnp.pad(q_idx, (0, pad), mode='edge')
    nb = (nq + pad) // Q_BLOCK
    qb = jnp.moveaxis(qp.reshape(bsz, nb, Q_BLOCK, q.shape[2], q.shape[3]), 1, 0)
    ib = ip.reshape(nb, Q_BLOCK)
    o, lse = lax.map(lambda a: _dilated_block(a[0], k, v, a[1], dil, n_keys, slopes), (qb, ib))
    o = jnp.moveaxis(o, 0, 1).reshape(bsz, nb * Q_BLOCK, q.shape[2], q.shape[3])[:, :nq]
    lse = jnp.moveaxis(lse, 0, 1).reshape(bsz, nb * Q_BLOCK, q.shape[2])[:, :nq]
    return o, lse


def combine_groups(outs, lses):
    o = jnp.stack(outs, axis=2)
    a = jax.nn.softmax(jnp.stack(lses, axis=2), axis=2)
    return jnp.sum(o * a[..., None].astype(o.dtype), axis=2)


def prompt_attention(q, k, v):
    bsz, s = q.shape[:2]
    slopes = jnp.asarray(ALIBI_SLOPES, jnp.float32)
    pos = jnp.arange(s, dtype=jnp.int32)
    outs, lses, kv_rows = [], [], []
    for g, (win, dil) in enumerate(DIL_GROUPS):
        hs = slice(g * HEADS_PER_GROUP, (g + 1) * HEADS_PER_GROUP)
        o, lse = dilated_attention(q[:, :, hs], k[:, :, hs], v[:, :, hs], pos, dil, win // dil + 1, slopes[hs])
        outs.append(o)
        lses.append(lse)
        keep = min(win, s)
        kv_rows.append(jnp.stack([k[:, s - keep:, hs], v[:, s - keep:, hs]], axis=2))
    return combine_groups(outs, lses).reshape(bsz, s, B_OUT), kv_rows


def sample_attention(q, k, v, bufs):
    bsz, n = q.shape[:2]
    slopes = jnp.asarray(ALIBI_SLOPES, jnp.float32)
    outs, lses, kv_rows = [], [], []
    for g, (win, dil) in enumerate(DIL_GROUPS):
        hs = slice(g * HEADS_PER_GROUP, (g + 1) * HEADS_PER_GROUP)
        buf = bufs[g]
        L = buf.shape[1]
        kk = jnp.concatenate([buf[:, :, 0], k[:, :, hs]], axis=1)
        vv = jnp.concatenate([buf[:, :, 1], v[:, :, hs]], axis=1)
        q_idx = L + jnp.arange(n, dtype=jnp.int32)
        o, lse = dilated_attention(q[:, :, hs], kk, vv, q_idx, dil, win // dil + 1, slopes[hs])
        outs.append(o)
        lses.append(lse)
        kv_rows.append(jnp.stack([k[:, :, hs], v[:, :, hs]], axis=2))
    return combine_groups(outs, lses).reshape(bsz, n, B_OUT), kv_rows


def moe_ffn(h, w_router, b_router, w_gate, b_gate, w_up, b_up, w_down, b_down):
    hf = h.reshape(-1, D_MODEL)
    T = hf.shape[0]
    logits = jnp.matmul(hf, w_router, preferred_element_type=jnp.float32) + b_router.astype(jnp.float32)
    top_logit, top_e = lax.top_k(logits, TOP_K)
    top_w = jax.nn.softmax(top_logit, axis=-1).astype(h.dtype)
    n_pairs = T * TOP_K
    blk = min(MOE_BLOCK, max(8, n_pairs // N_EXPERTS))
    n_rows = -(-(n_pairs + N_EXPERTS * (blk - 1)) // blk) * blk
    n_blocks = n_rows // blk
    flat_e = top_e.reshape(-1)
    flat_tok = jnp.arange(n_pairs, dtype=jnp.int32) // TOP_K
    order = jnp.argsort(flat_e)
    sorted_e = flat_e[order]
    counts = jnp.bincount(flat_e, length=N_EXPERTS)
    padded = (counts + blk - 1) // blk * blk
    pad_end = jnp.cumsum(padded)
    pad_start = pad_end - padded
    start = jnp.cumsum(counts) - counts
    dest = pad_start[sorted_e] + jnp.arange(n_pairs, dtype=jnp.int32) - start[sorted_e]
    row_tok = jnp.full((n_rows,), T, jnp.int32).at[dest].set(flat_tok[order])
    row_w = jnp.zeros((n_rows,), h.dtype).at[dest].set(top_w.reshape(-1)[order])
    block_e = jnp.minimum(jnp.searchsorted(pad_end, jnp.arange(n_blocks, dtype=jnp.int32) * blk, side='right'), N_EXPERTS - 1)
    x_rows = jnp.concatenate([hf, jnp.zeros((1, D_MODEL), hf.dtype)], axis=0)[row_tok].reshape(n_blocks, blk, D_MODEL)

    def expert_block(args):
        xb, e = args
        gt = xb @ w_gate[e] + b_gate[e]
        up = xb @ w_up[e] + b_up[e]
        gt = jnp.minimum(gt, SWIGLU_LIMIT)
        up = jnp.clip(up, -SWIGLU_LIMIT, SWIGLU_LIMIT)
        return ((up + 1) * gt * jax.nn.sigmoid(SWIGLU_ALPHA * gt)) @ w_down[e] + b_down[e]

    y_rows = lax.map(expert_block, (x_rows, block_e)).reshape(n_rows, D_MODEL) * row_w[:, None]
    y = jax.ops.segment_sum(y_rows, row_tok, num_segments=T + 1)[:T]
    return y.reshape(h.shape)


def decoder_layer(x, c, attn_fn, w_ada, b_ada, g_mix, w_in, ln_g, ln_b, w_s, b_s, p_a, p_b, w_o,
                  g_ffn, w_router, b_router, w_gate, b_gate, w_up, b_up, w_down, b_down):
    bsz, n = x.shape[:2]
    sh1, sc1, gt1, sh2, sc2, gt2 = ada_params(c, w_ada, b_ada)
    h = rms_norm(x, g_mix) * (1 + sc1) + sh1
    z = h @ w_in
    u, va, q, k, v, ga, gb = jnp.split(z, SPLITS, axis=-1)
    u = jax.nn.gelu(u).reshape(bsz, n, A_GROUPS, A_GROUP_DIM)
    va = layer_norm(jax.nn.gelu(va), ln_g, ln_b)
    a_out = chunk_spatial_gate(u, va.reshape(bsz, n, A_GROUPS, A_GROUP_DIM), w_s, b_s).reshape(bsz, n, A_WIDTH)
    q = q.reshape(bsz, n, N_HEADS, HEAD_DIM)
    k = k.reshape(bsz, n, N_HEADS, HEAD_DIM)
    v = v.reshape(bsz, n, N_HEADS, HEAD_DIM)
    b_out, kv_rows = attn_fn(q, k, v)
    merged = jax.nn.sigmoid(ga) * (a_out @ p_a) + jax.nn.sigmoid(gb) * (b_out @ p_b)
    x = x + gt1 * (merged @ w_o)
    h2 = rms_norm(x, g_ffn) * (1 + sc2) + sh2
    x = x + gt2 * moe_ffn(h2, w_router, b_router, w_gate, b_gate, w_up, b_up, w_down, b_down)
    return x, kv_rows, va


def setup_inputs(seed: int = 0) -> dict:
    key = jax.random.key(seed)
    ks = iter(jax.random.split(key, 40))

    def nrm(shape, scale):
        return jax.random.normal(next(ks), shape, jnp.float32) * scale

    lens = [min(w, PAST_LEN) for (w, _) in DIL_GROUPS]
    kvs = (HEADS_PER_GROUP, HEAD_DIM)
    return {
        'x_prompt': nrm((BATCH, SEQ, D_MODEL), 1.0),
        'x_sample': nrm((DEC_BATCH, DEC_SEQ, D_MODEL), 1.0),
        'cache_kv_g0': nrm((DEPTH, DEC_BATCH, lens[0], 2) + kvs, 1.0),
        'cache_kv_g1': nrm((DEPTH, DEC_BATCH, lens[1], 2) + kvs, 1.0),
        'cache_kv_g2': nrm((DEPTH, DEC_BATCH, lens[2], 2) + kvs, 1.0),
        'c_prompt': nrm((BATCH, D_MODEL), 1.0),
        'c_sample': nrm((DEC_BATCH, D_MODEL), 1.0),
        'w_ada': nrm((DEPTH, D_MODEL, N_ADA * D_MODEL), 0.5 * D_MODEL ** -0.5),
        'b_ada': nrm((DEPTH, N_ADA * D_MODEL), 0.02),
        'g_mix': 1.0 + nrm((DEPTH, D_MODEL), 0.02),
        'w_in': nrm((DEPTH, D_MODEL, IN_COLS), D_MODEL ** -0.5),
        'ln_g': 1.0 + nrm((DEPTH, A_WIDTH), 0.02),
        'ln_b': nrm((DEPTH, A_WIDTH), 0.02),
        'w_s': nrm((DEPTH, A_GROUPS, CHUNK, CHUNK), 0.5 * CHUNK ** -0.5),
        'b_s': 1.0 + nrm((DEPTH, A_GROUPS, CHUNK), 0.1),
        'p_a': nrm((DEPTH, A_WIDTH, D_MODEL), A_WIDTH ** -0.5),
        'p_b': nrm((DEPTH, B_OUT, D_MODEL), B_OUT ** -0.5),
        'w_o': nrm((DEPTH, D_MODEL, D_MODEL), D_MODEL ** -0.5),
        'g_ffn': 1.0 + nrm((DEPTH, D_MODEL), 0.02),
        'w_router': nrm((DEPTH, D_MODEL, N_EXPERTS), D_MODEL ** -0.5),
        'b_router': nrm((DEPTH, N_EXPERTS), 0.01),
        'w_gate': nrm((DEPTH, N_EXPERTS, D_MODEL, D_FF), D_MODEL ** -0.5),
        'b_gate': nrm((DEPTH, N_EXPERTS, D_FF), 0.01),
        'w_up': nrm((DEPTH, N_EXPERTS, D_MODEL, D_FF), D_MODEL ** -0.5),
        'b_up': nrm((DEPTH, N_EXPERTS, D_FF), 0.01),
        'w_down': nrm((DEPTH, N_EXPERTS, D_FF, D_MODEL), D_FF ** -0.5),
        'b_down': nrm((DEPTH, N_EXPERTS, D_MODEL), 0.01),
        'g_final': 1.0 + nrm((D_MODEL,), 0.02),
    }


def reference(x_prompt, x_sample, cache_kv_g0, cache_kv_g1, cache_kv_g2, c_prompt, c_sample,
              w_ada, b_ada, g_mix, w_in, ln_g, ln_b, w_s, b_s, p_a, p_b, w_o, g_ffn,
              w_router, b_router, w_gate, b_gate, w_up, b_up, w_down, b_down, g_final):
    yp, ys = x_prompt, x_sample
    kvp_layers, kvs_layers, vs_layers = [], [], []
    for l in range(DEPTH):
        lw = (w_ada[l], b_ada[l], g_mix[l], w_in[l], ln_g[l], ln_b[l], w_s[l], b_s[l], p_a[l], p_b[l], w_o[l],
              g_ffn[l], w_router[l], b_router[l], w_gate[l], b_gate[l], w_up[l], b_up[l], w_down[l], b_down[l])
        yp, kvp, _ = decoder_layer(yp, c_prompt, prompt_attention, *lw)
        s_attn = functools.partial(sample_attention, bufs=(cache_kv_g0[l], cache_kv_g1[l], cache_kv_g2[l]))
        ys, kvs, vs = decoder_layer(ys, c_sample, s_attn, *lw)
        kvp_layers.append(kvp)
        kvs_layers.append(kvs)
        vs_layers.append(vs)
    y_prompt = rms_norm(yp, g_final)
    y_sample = rms_norm(ys, g_final)
    kv_g0_prompt = jnp.stack([r[0] for r in kvp_layers], axis=0)
    kv_g1_prompt = jnp.stack([r[1] for r in kvp_layers], axis=0)
    kv_g2_prompt = jnp.stack([r[2] for r in kvp_layers], axis=0)
    kv_g0_sample = jnp.stack([r[0] for r in kvs_layers], axis=0)
    kv_g1_sample = jnp.stack([r[1] for r in kvs_layers], axis=0)
    kv_g2_sample = jnp.stack([r[2] for r in kvs_layers], axis=0)
    chunk_v_sample = jnp.stack(vs_layers, axis=0)
    return (y_prompt, y_sample, kv_g0_prompt, kv_g1_prompt, kv_g2_prompt, kv_g0_sample, kv_g1_sample, kv_g2_sample, chunk_v_sample)
```

```python
import functools

import jax
import jax.numpy as jnp
from jax import lax
from jax.experimental import pallas as pl
from jax.experimental.pallas import tpu as pltpu

F32 = jnp.float32
BF16 = jnp.bfloat16
HIGHEST = lax.Precision.HIGHEST

EPS = 1e-6
A_GROUPS = 8
CHUNK = 128
HEAD_DIM = 128
HEADS_PER_GROUP = 4
DIL_GROUPS = ((128, 1), (512, 4), (2048, 16))
N_HEADS = HEADS_PER_GROUP * len(DIL_GROUPS)
ALIBI_SLOPES = tuple(2.0 ** (-8.0 * (h + 1) / N_HEADS) for h in range(N_HEADS))
GROUP_COLS = HEADS_PER_GROUP * HEAD_DIM
N_EXPERTS = 32
TOP_K = 4
SWIGLU_LIMIT = 7.0
SWIGLU_ALPHA = 1.702
N_ADA = 6
NEG = -1e30

COL_TILE = 512
MOE_TILE = 512
GATHER_ROWS = 512
VMEM_LIMIT = 56 * 1024 * 1024


def _params(sem, vmem=VMEM_LIMIT):
    return pltpu.CompilerParams(dimension_semantics=sem, vmem_limit_bytes=vmem)


def _dot(a, b, hi):
    if hi:
        return jnp.dot(a, b, precision=HIGHEST, preferred_element_type=F32)
    return jnp.dot(a.astype(BF16), b.astype(BF16), preferred_element_type=F32)


def _dot_t(a, b, hi):
    dn = (((1,), (1,)), ((), ()))
    if hi:
        return lax.dot_general(a, b, dn, precision=HIGHEST, preferred_element_type=F32)
    return lax.dot_general(a.astype(BF16), b.astype(BF16), dn, preferred_element_type=F32)


def _mod_rms(x, g, sc, sh):
    y = x * lax.rsqrt(jnp.mean(x * x, axis=-1, keepdims=True) + EPS)
    return y * g * (1.0 + sc) + sh


def _ada_kernel(c_ref, w_ref, b_ref, o_ref):
    c = c_ref[...]
    o_ref[...] = _dot(c * jax.nn.sigmoid(c), w_ref[...], True) + b_ref[...]


def _ada(c_all, w_ada, b_ada):
    n, d = c_all.shape
    cols = w_ada.shape[1]
    tn = 1024
    return pl.pallas_call(
        _ada_kernel,
        grid=(cols // tn,),
        in_specs=[pl.BlockSpec((n, d), lambda j: (0, 0)),
                  pl.BlockSpec((d, tn), lambda j: (0, j)),
                  pl.BlockSpec((1, tn), lambda j: (0, j))],
        out_specs=pl.BlockSpec((n, tn), lambda j: (0, j)),
        out_shape=jax.ShapeDtypeStruct((n, cols), F32),
        compiler_params=_params(("arbitrary",)),
        name="ada",
    )(c_all, w_ada, b_ada.reshape(1, cols))


def _in_proj_kernel(x_ref, sc_ref, sh_ref, g_ref, w_ref, o_ref, h_ref, *, hi, n_gelu, n_raw):
    j = pl.program_id(1)

    @pl.when(j == 0)
    def _():
        h_ref[...] = _mod_rms(x_ref[...], g_ref[...], sc_ref[0], sh_ref[0]).astype(h_ref.dtype)

    z = _dot(h_ref[...], w_ref[...], hi)

    @pl.when(j < n_gelu)
    def _():
        o_ref[...] = jax.nn.gelu(z)

    @pl.when((j >= n_gelu) & (j < n_gelu + n_raw))
    def _():
        o_ref[...] = z

    @pl.when(j >= n_gelu + n_raw)
    def _():
        o_ref[...] = jax.nn.sigmoid(z)


def _in_proj(x, sc, sh, g, w_in, *, rows_per_mod, tm, hi):
    t, d = x.shape
    cols = w_in.shape[1]
    a_width = A_GROUPS * CHUNK
    n_gelu = 2 * a_width // COL_TILE
    n_raw = 3 * N_HEADS * HEAD_DIM // COL_TILE
    mod_rows = sc.shape[1]
    kern = functools.partial(_in_proj_kernel, hi=hi, n_gelu=n_gelu, n_raw=n_raw)
    return pl.pallas_call(
        kern,
        grid=(t // tm, cols // COL_TILE),
        in_specs=[pl.BlockSpec((tm, d), lambda i, j: (i, 0)),
                  pl.BlockSpec((1, mod_rows, d), lambda i, j: (i * tm // rows_per_mod, 0, 0)),
                  pl.BlockSpec((1, mod_rows, d), lambda i, j: (i * tm // rows_per_mod, 0, 0)),
                  pl.BlockSpec((1, d), lambda i, j: (0, 0)),
                  pl.BlockSpec((d, COL_TILE), lambda i, j: (0, j))],
        out_specs=pl.BlockSpec((tm, COL_TILE), lambda i, j: (i, j)),
        out_shape=jax.ShapeDtypeStruct((t, cols), F32),
        scratch_shapes=[pltpu.VMEM((tm, d), F32 if hi else BF16)],
        compiler_params=_params(("arbitrary", "arbitrary")),
        name="in_proj",
    )(x, sc, sh, g.reshape(1, d), w_in)


def _dil_attn_kernel(q_ref, kp_ref, kc_ref, vp_ref, vc_ref, o_ref, l_ref, *, dil, slopes):
    i = pl.program_id(2)
    row = lax.broadcasted_iota(jnp.int32, (CHUNK, CHUNK), 0)
    col = lax.broadcasted_iota(jnp.int32, (CHUNK, CHUNK), 1)
    dist_c = ((row - col) * dil).astype(F32)
    dist_p = ((row + CHUNK - col) * dil).astype(F32)
    valid_c = col <= row
    valid_p = col >= row
    no_prev = jnp.where(i > 0, 0.0, NEG)
    scale = HEAD_DIM ** -0.5
    for h in range(HEADS_PER_GROUP):
        hs = slice(h * HEAD_DIM, (h + 1) * HEAD_DIM)
        q = q_ref[0, :, hs]
        s_c = _dot_t(q, kc_ref[0, :, hs], False) * scale - slopes[h] * dist_c
        s_p = _dot_t(q, kp_ref[0, :, hs], False) * scale - slopes[h] * dist_p + no_prev
        s_c = jnp.where(valid_c, s_c, NEG)
        s_p = jnp.where(valid_p, s_p, NEG)
        m = jnp.maximum(jnp.max(s_c, axis=1, keepdims=True), jnp.max(s_p, axis=1, keepdims=True))
        p_c = jnp.exp(s_c - m)
        p_p = jnp.exp(s_p - m)
        l = jnp.sum(p_c, axis=1, keepdims=True) + jnp.sum(p_p, axis=1, keepdims=True)
        o = _dot(p_c / l, vc_ref[0, :, hs], False) + _dot(p_p / l, vp_ref[0, :, hs], False)
        o_ref[0, :, hs] = o
        l_ref[0, :, hs] = jnp.broadcast_to(m + jnp.log(l), (CHUNK, HEAD_DIM))


def _dil_attn(z3, g, bsz, seq, cols):
    win, dil = DIL_GROUPS[g]
    n_t = seq // dil
    blocks_per_pos = cols // GROUP_COLS
    a_blocks = 2 * A_GROUPS * CHUNK // GROUP_COLS
    n_grp = len(DIL_GROUPS)
    qb, kb, vb = a_blocks + g, a_blocks + n_grp + g, a_blocks + 2 * n_grp + g
    blk = (1, CHUNK, GROUP_COLS)

    def spec(cb, prev):
        if prev:
            return pl.BlockSpec(blk, lambda b, r, i: (b, jnp.maximum(i - 1, 0), r * blocks_per_pos + cb))
        return pl.BlockSpec(blk, lambda b, r, i: (b, i, r * blocks_per_pos + cb))

    kern = functools.partial(_dil_attn_kernel, dil=dil,
                             slopes=ALIBI_SLOPES[g * HEADS_PER_GROUP:(g + 1) * HEADS_PER_GROUP])
    out_spec = pl.BlockSpec(blk, lambda b, r, i: (b, i, r))
    out_sds = jax.ShapeDtypeStruct((bsz, n_t, dil * GROUP_COLS), F32)
    return pl.pallas_call(
        kern,
        grid=(bsz, dil, n_t // CHUNK),
        in_specs=[spec(qb, False), spec(kb, True), spec(kb, False), spec(vb, True), spec(vb, False)],
        out_specs=[out_spec, out_spec],
        out_shape=[out_sds, out_sds],
        compiler_params=_params(("arbitrary", "arbitrary", "arbitrary")),
        name=f"dil_attn_g{g}",
    )(z3, z3, z3, z3, z3)


def _sample_attn_kernel(q_ref, k_ref, v_ref, c0_ref, c1_ref, c2_ref, o_ref):
    caches = (c0_ref, c1_ref, c2_ref)
    scale = HEAD_DIM ** -0.5
    lane = lax.broadcasted_iota(jnp.int32, (1, CHUNK), 1)
    outs, lses = [], []
    for g, (win, dil) in enumerate(DIL_GROUPS):
        n_buf = win // dil
        dist = ((n_buf - lane) * dil).astype(F32)
        go, gl = [], []
        for h in range(HEADS_PER_GROUP):
            hs = slice(g * GROUP_COLS + h * HEAD_DIM, g * GROUP_COLS + (h + 1) * HEAD_DIM)
            ks = slice(h * HEAD_DIM, (h + 1) * HEAD_DIM)
            vs = slice(GROUP_COLS + h * HEAD_DIM, GROUP_COLS + (h + 1) * HEAD_DIM)
            slope = ALIBI_SLOPES[g * HEADS_PER_GROUP + h]
            q = q_ref[0, :, hs]
            k_new = k_ref[0, :, hs]
            v_new = v_ref[0, :, hs]
            q8 = jnp.broadcast_to(q, (8, HEAD_DIM))
            s_buf = _dot_t(q8, caches[g][0, :, ks], True)[0:1] * scale - slope * dist
            s_new = jnp.sum(q * k_new, axis=1, keepdims=True) * scale
            m = jnp.maximum(jnp.max(s_buf, axis=1, keepdims=True), s_new)
            p_buf = jnp.exp(s_buf - m)
            p_new = jnp.exp(s_new - m)
            l = jnp.sum(p_buf, axis=1, keepdims=True) + p_new
            pb8 = jnp.broadcast_to(p_buf / l, (8, CHUNK))
            o = _dot(pb8, caches[g][0, :, vs], True)[0:1] + (p_new / l) * v_new
            go.append(o)
            gl.append(m + jnp.log(l))
        outs.append(go)
        lses.append(gl)
    for h in range(HEADS_PER_GROUP):
        m = jnp.maximum(jnp.maximum(lses[0][h], lses[1][h]), lses[2][h])
        e = [jnp.exp(lses[g][h] - m) for g in range(3)]
        den = e[0] + e[1] + e[2]
        o = outs[0][h] * (e[0] / den) + outs[1][h] * (e[1] / den) + outs[2][h] * (e[2] / den)
        o_ref[0, :, h * HEAD_DIM:(h + 1) * HEAD_DIM] = o


def _sample_attn(zs, caches):
    n, cols = zs.shape
    z3 = zs.reshape(n, 1, cols)
    a_blocks = 2 * A_GROUPS * CHUNK // GROUP_COLS
    n_grp = len(DIL_GROUPS)
    qkv_cols = n_grp * GROUP_COLS
    c_views = []
    c_specs = []
    for g, (win, dil) in enumerate(DIL_GROUPS):
        c = caches[g]
        assert c.shape[1] == win, "cache must hold exactly one window"
        row = 2 * GROUP_COLS
        c_views.append(c.reshape(n, win // dil, dil * row))
        c_specs.append(pl.BlockSpec((1, win // dil, row), lambda b: (b, 0, 0)))

    q0 = a_blocks * GROUP_COLS
    row_spec = pl.BlockSpec((1, 1, qkv_cols), lambda b: (b, 0, 0))
    out = pl.pallas_call(
        _sample_attn_kernel,
        grid=(n,),
        in_specs=[row_spec, row_spec, row_spec] + c_specs,
        out_specs=pl.BlockSpec((1, 1, GROUP_COLS), lambda b: (b, 0, 0)),
        out_shape=jax.ShapeDtypeStruct((n, 1, GROUP_COLS), F32),
        compiler_params=_params(("arbitrary",)),
        name="sample_attn",
    )(z3[:, :, q0:q0 + qkv_cols], z3[:, :, q0 + qkv_cols:q0 + 2 * qkv_cols],
      z3[:, :, q0 + 2 * qkv_cols:q0 + 3 * qkv_cols], *c_views)
    return out.reshape(n, GROUP_COLS)


def _layer_norm(v, g, b):
    mu = jnp.mean(v, axis=-1, keepdims=True)
    var = jnp.mean(jnp.square(v - mu), axis=-1, keepdims=True)
    return (v - mu) * lax.rsqrt(var + EPS) * g + b


def _chunk_gate_kernel(u_ref, v_ref, lg_ref, lb_ref, ws_ref, bs_ref, o_ref, *, n_chunks):
    va = _layer_norm(v_ref[...], lg_ref[...], lb_ref[...]).astype(BF16)
    row = lax.broadcasted_iota(jnp.int32, (CHUNK, CHUNK), 0)
    col = lax.broadcasted_iota(jnp.int32, (CHUNK, CHUNK), 1)
    for g in range(A_GROUPS):
        w = jnp.where(col <= row, ws_ref[g], 0.0).astype(BF16)
        gs = slice(g * CHUNK, (g + 1) * CHUNK)
        for c in range(n_chunks):
            rs = slice(c * CHUNK, (c + 1) * CHUNK)
            mixed = jnp.dot(w, va[rs, gs], preferred_element_type=F32) + bs_ref[g]
            o_ref[rs, gs] = (u_ref[rs, gs] * mixed).astype(o_ref.dtype)


def _chunk_gate(z, ln_g, ln_b, w_s, b_s, *, tm):
    t = z.shape[0]
    aw = A_GROUPS * CHUNK
    bs_full = jnp.broadcast_to(b_s[:, :, None], (A_GROUPS, CHUNK, CHUNK))
    kern = functools.partial(_chunk_gate_kernel, n_chunks=tm // CHUNK)
    return pl.pallas_call(
        kern,
        grid=(t // tm,),
        in_specs=[pl.BlockSpec((tm, aw), lambda i: (i, 0)),
                  pl.BlockSpec((tm, aw), lambda i: (i, 1)),
                  pl.BlockSpec((1, aw), lambda i: (0, 0)),
                  pl.BlockSpec((1, aw), lambda i: (0, 0)),
                  pl.BlockSpec((A_GROUPS, CHUNK, CHUNK), lambda i: (0, 0, 0)),
                  pl.BlockSpec((A_GROUPS, CHUNK, CHUNK), lambda i: (0, 0, 0))],
        out_specs=pl.BlockSpec((tm, aw), lambda i: (i, 0)),
        out_shape=jax.ShapeDtypeStruct((t, aw), BF16),
        compiler_params=_params(("arbitrary",)),
        name="chunk_gate",
    )(z, z, ln_g.reshape(1, aw), ln_b.reshape(1, aw), w_s, bs_full)


def _sample_gate_kernel(u_ref, v_ref, lg_ref, lb_ref, w0_ref, b0_ref, a_ref, vn_ref):
    va = _layer_norm(v_ref[...], lg_ref[...], lb_ref[...])
    vn_ref[...] = va
    a_ref[...] = u_ref[...] * (w0_ref[...] * va + b0_ref[...])


def _sample_gate(zs, ln_g, ln_b, w_s, b_s):
    n = zs.shape[0]
    aw = A_GROUPS * CHUNK
    w0 = jnp.repeat(w_s[:, 0, 0], CHUNK).reshape(1, aw)
    b0 = jnp.repeat(b_s[:, 0], CHUNK).reshape(1, aw)
    vec = pl.BlockSpec((1, aw), lambda i: (0, 0))
    return pl.pallas_call(
        _sample_gate_kernel,
        grid=(1,),
        in_specs=[pl.BlockSpec((n, aw), lambda i: (0, 0)), pl.BlockSpec((n, aw), lambda i: (0, 1)),
                  vec, vec, vec, vec],
        out_specs=[pl.BlockSpec((n, aw), lambda i: (0, 0)), pl.BlockSpec((n, aw), lambda i: (0, 0))],
        out_shape=[jax.ShapeDtypeStruct((n, aw), F32), jax.ShapeDtypeStruct((n, aw), F32)],
        compiler_params=_params(("arbitrary",)),
        name="sample_gate",
    )(zs, zs, ln_g.reshape(1, aw), ln_b.reshape(1, aw), w0, b0)


def _combine_groups(os_, ls_):
    m = jnp.maximum(jnp.maximum(ls_[0], ls_[1]), ls_[2])
    e = [jnp.exp(l - m) for l in ls_]
    den = e[0] + e[1] + e[2]
    return os_[0] * (e[0] / den) + os_[1] * (e[1] / den) + os_[2] * (e[2] / den)


def _merge_kernel(*refs, hi, combine):
    if combine:
        a_ref, o0, o1, o2, l0, l1, l2, ga_ref, gb_ref, pa_ref, pb_ref, out_ref, b_ref = refs
    else:
        a_ref, bo_ref, ga_ref, gb_ref, pa_ref, pb_ref, out_ref, b_ref = refs
    j = pl.program_id(1)

    @pl.when(j == 0)
    def _():
        if combine:
            b = _combine_groups((o0[...], o1[...], o2[...]), (l0[...], l1[...], l2[...]))
        else:
            b = bo_ref[...]
        b_ref[...] = b.astype(b_ref.dtype)

    ya = _dot(a_ref[...], pa_ref[...], hi)
    yb = _dot(b_ref[...], pb_ref[...], hi)
    out_ref[...] = (ga_ref[...] * ya + gb_ref[...] * yb).astype(out_ref.dtype)


def _merge(a_out, b_parts, z, p_a, p_b, *, tm, hi):
    t, aw = a_out.shape
    d = p_a.shape[1]
    combine = len(b_parts) > 1
    tn = COL_TILE
    cols = z.shape[1]
    ga_blk0 = (cols - 2 * d) // tn
    gb_blk0 = (cols - d) // tn
    row_spec = lambda w: pl.BlockSpec((tm, w), lambda i, j: (i, 0))
    kern = functools.partial(_merge_kernel, hi=hi, combine=combine)
    return pl.pallas_call(
        kern,
        grid=(t // tm, d // tn),
        in_specs=[row_spec(aw)] + [row_spec(GROUP_COLS)] * len(b_parts) + [
            pl.BlockSpec((tm, tn), lambda i, j: (i, ga_blk0 + j)),
            pl.BlockSpec((tm, tn), lambda i, j: (i, gb_blk0 + j)),
            pl.BlockSpec((aw, tn), lambda i, j: (0, j)),
            pl.BlockSpec((GROUP_COLS, tn), lambda i, j: (0, j))],
        out_specs=pl.BlockSpec((tm, tn), lambda i, j: (i, j)),
        out_shape=jax.ShapeDtypeStruct((t, d), F32 if hi else BF16),
        scratch_shapes=[pltpu.VMEM((tm, GROUP_COLS), F32 if hi else BF16)],
        compiler_params=_params(("arbitrary", "arbitrary")),
        name="merge",
    )(a_out, *b_parts, z, z, p_a, p_b)


def _out_proj_kernel(m_ref, w_ref, x_ref, gt_ref, o_ref, *, hi):
    o_ref[...] = x_ref[...] + gt_ref[0] * _dot(m_ref[...], w_ref[...], hi)


def _out_proj(merged, w_o, x, gt, *, rows_per_mod, tm, hi):
    t, d = x.shape
    tn = COL_TILE
    mod_rows = gt.shape[1]
    return pl.pallas_call(
        functools.partial(_out_proj_kernel, hi=hi),
        grid=(t // tm, d // tn),
        in_specs=[pl.BlockSpec((tm, d), lambda i, j: (i, 0)),
                  pl.BlockSpec((d, tn), lambda i, j: (0, j)),
                  pl.BlockSpec((tm, tn), lambda i, j: (i, j)),
                  pl.BlockSpec((1, mod_rows, tn), lambda i, j: (i * tm // rows_per_mod, 0, j))],
        out_specs=pl.BlockSpec((tm, tn), lambda i, j: (i, j)),
        out_shape=jax.ShapeDtypeStruct((t, d), F32),
        compiler_params=_params(("arbitrary", "arbitrary")),
        name="out_proj",
    )(merged, w_o, x, gt)


def _ffn_norm_kernel(x_ref, sc_ref, sh_ref, g_ref, wr_ref, br_ref, h_ref, e_ref, w_ref):
    h = _mod_rms(x_ref[...], g_ref[...], sc_ref[0], sh_ref[0])
    h_ref[...] = h
    logits = _dot(h, wr_ref[...], True) + br_ref[...]
    lane = lax.broadcasted_iota(jnp.int32, logits.shape, 1)
    lane_f = lane.astype(F32)
    vals, idxs = [], []
    for _ in range(TOP_K):
        m = jnp.max(logits, axis=1, keepdims=True)
        idx = jnp.min(jnp.where(logits == m, lane_f, float(logits.shape[1])), axis=1, keepdims=True)
        vals.append(m)
        idxs.append(idx)
        logits = jnp.where(lane_f == idx, 2.0 * NEG, logits)
    es = [jnp.exp(v - vals[0]) for v in vals]
    den = es[0] + es[1] + es[2] + es[3]
    e_out = jnp.zeros(lane.shape, F32)
    w_out = jnp.zeros(lane.shape, F32)
    for k in range(TOP_K):
        e_out = jnp.where(lane == k, idxs[k], e_out)
        w_out = jnp.where(lane == k, es[k] / den, w_out)
    e_ref[...] = e_out.astype(jnp.int32)
    w_ref[...] = w_out


def _ffn_norm(x1, sc, sh, g, w_router, b_router, *, rows_per_mod, tm):
    t, d = x1.shape
    ne = w_router.shape[1]
    lanes = 128
    wr = jnp.zeros((d, lanes), F32).at[:, :ne].set(w_router)
    br = jnp.full((1, lanes), NEG, F32).at[0, :ne].set(b_router)
    mod_rows = sc.shape[1]
    mod_spec = pl.BlockSpec((1, mod_rows, d), lambda i: (i * tm // rows_per_mod, 0, 0))
    return pl.pallas_call(
        _ffn_norm_kernel,
        grid=(t // tm,),
        in_specs=[pl.BlockSpec((tm, d), lambda i: (i, 0)), mod_spec, mod_spec,
                  pl.BlockSpec((1, d), lambda i: (0, 0)),
                  pl.BlockSpec((d, lanes), lambda i: (0, 0)),
                  pl.BlockSpec((1, lanes), lambda i: (0, 0))],
        out_specs=[pl.BlockSpec((tm, d), lambda i: (i, 0)),
                   pl.BlockSpec((tm, lanes), lambda i: (i, 0)),
                   pl.BlockSpec((tm, lanes), lambda i: (i, 0))],
        out_shape=[jax.ShapeDtypeStruct((t, d), F32),
                   jax.ShapeDtypeStruct((t, lanes), jnp.int32),
                   jax.ShapeDtypeStruct((t, lanes), F32)],
        compiler_params=_params(("arbitrary",)),
        name="ffn_norm",
    )(x1, sc, sh, g.reshape(1, d), wr, br)


def _row_copy(src_hbm, buf, sem, slot, src_row, dst_row):
    return pltpu.make_async_copy(src_hbm.at[pl.ds(src_row, 1)], buf.at[slot, pl.ds(dst_row, 1)], sem.at[slot])


def _issue_rows(idx_ref, src_hbm, buf, sem, slot, n_rows):
    def body(r, carry):
        _row_copy(src_hbm, buf, sem, slot, idx_ref[0, 0, r], r).start()
        return carry
    lax.fori_loop(0, n_rows, body, 0)


def _wait_rows(src_hbm, buf, sem, slot, n_rows):
    pltpu.make_async_copy(src_hbm.at[pl.ds(0, n_rows)], buf.at[slot], sem.at[slot]).wait()


def _gather_step(idx_ref, nxt_ref, src_hbm, buf, sem, n_rows):
    i = pl.program_id(0)
    n = pl.num_programs(0)
    slot = lax.rem(i, 2)

    @pl.when(i == 0)
    def _():
        _issue_rows(idx_ref, src_hbm, buf, sem, 0, n_rows)

    @pl.when(i + 1 < n)
    def _():
        _issue_rows(nxt_ref, src_hbm, buf, sem, 1 - slot, n_rows)

    _wait_rows(src_hbm, buf, sem, slot, n_rows)
    return slot


def _idx_specs(n_rows, steps):
    blk = (1, 1, n_rows)
    return [pl.BlockSpec(blk, lambda i: (i, 0, 0), memory_space=pltpu.SMEM),
            pl.BlockSpec(blk, lambda i: (jnp.minimum(i + 1, steps - 1), 0, 0), memory_space=pltpu.SMEM)]


def _row_gather_kernel(idx_ref, nxt_ref, src_hbm, o_ref, buf, sem, *, n_rows):
    slot = _gather_step(idx_ref, nxt_ref, src_hbm, buf, sem, n_rows)
    o_ref[...] = buf[slot].astype(o_ref.dtype)


def _row_gather(src, idx, out_dtype):
    n = idx.shape[0]
    d = src.shape[1]
    steps = n // GATHER_ROWS
    idx3 = idx.reshape(steps, 1, GATHER_ROWS)
    return pl.pallas_call(
        functools.partial(_row_gather_kernel, n_rows=GATHER_ROWS),
        grid=(steps,),
        in_specs=_idx_specs(GATHER_ROWS, steps) +[pl.BlockSpec(memory_space=pl.ANY)],
        out_specs=pl.BlockSpec((GATHER_ROWS, d), lambda i: (i, 0)),
        out_shape=jax.ShapeDtypeStruct((n, d), out_dtype),
        scratch_shapes=[pltpu.VMEM((2, GATHER_ROWS, d), src.dtype), pltpu.SemaphoreType.DMA((2,))],
        compiler_params=_params(("arbitrary",)),
        name="row_gather",
    )(idx3, idx3, src)


def _moe_up_kernel(te_ref, nu_ref, x_ref, wg_ref, wu_ref, bg_ref, bu_ref, h_ref, wg_s, wu_s):
    t = pl.program_id(1)
    e = te_ref[t]
    prev = te_ref[jnp.maximum(t - 1, 0)]

    @pl.when((t == 0) | (e != prev))
    def _():
        wg_s[...] = wg_ref[0].astype(BF16)
        wu_s[...] = wu_ref[0].astype(BF16)

    @pl.when(t < nu_ref[0])
    def _():
        x = x_ref[...]
        gt = jnp.dot(x, wg_s[...], preferred_element_type=F32) + bg_ref[0]
        up = jnp.dot(x, wu_s[...], preferred_element_type=F32) + bu_ref[0]
        gt = jnp.minimum(gt, SWIGLU_LIMIT)
        up = jnp.clip(up, -SWIGLU_LIMIT, SWIGLU_LIMIT)
        h_ref[...] = ((up + 1.0) * gt * jax.nn.sigmoid(SWIGLU_ALPHA * gt)).astype(h_ref.dtype)

    @pl.when(t >= nu_ref[0])
    def _():
        h_ref[...] = jnp.zeros(h_ref.shape, h_ref.dtype)


def _moe_down_kernel(te_ref, nu_ref, h_ref, wd_ref, bd_ref, y_ref, wd_s):
    t = pl.program_id(1)
    e = te_ref[t]
    prev = te_ref[jnp.maximum(t - 1, 0)]

    @pl.when((t == 0) | (e != prev))
    def _():
        wd_s[...] = wd_ref[0].astype(BF16)

    @pl.when(t < nu_ref[0])
    def _():
        y_ref[...] = jnp.dot(h_ref[...], wd_s[...], preferred_element_type=F32) + bd_ref[0]

    @pl.when(t >= nu_ref[0])
    def _():
        y_ref[...] = jnp.zeros(y_ref.shape, y_ref.dtype)


def _moe_experts(x_sorted, tile_expert, n_used, w_gate, b_gate, w_up, b_up, w_down, b_down):
    rows, d = x_sorted.shape
    ne, _, f = w_gate.shape
    nt = rows // MOE_TILE
    tf = 512
    tn = 1024

    def row_map(j, t, te, nu):
        return (jnp.minimum(t, nu[0] - 1), 0)

    def out_map(j, t, te, nu):
        return (t, j)

    def w_map(j, t, te, nu):
        return (te[t], 0, j)

    h = pl.pallas_call(
        _moe_up_kernel,
        grid_spec=pltpu.PrefetchScalarGridSpec(
            num_scalar_prefetch=2,
            grid=(f // tf, nt),
            in_specs=[pl.BlockSpec((MOE_TILE, d), row_map),
                      pl.BlockSpec((1, d, tf), w_map),
                      pl.BlockSpec((1, d, tf), w_map),
                      pl.BlockSpec((1, 1, tf), w_map),
                      pl.BlockSpec((1, 1, tf), w_map)],
            out_specs=pl.BlockSpec((MOE_TILE, tf), out_map),
            scratch_shapes=[pltpu.VMEM((d, tf), BF16), pltpu.VMEM((d, tf), BF16)]),
        out_shape=jax.ShapeDtypeStruct((rows, f), BF16),
        compiler_params=_params(("arbitrary", "arbitrary")),
        name="moe_up",
    )(tile_expert, n_used, x_sorted, w_gate, w_up, b_gate.reshape(ne, 1, f), b_up.reshape(ne, 1, f))

    return pl.pallas_call(
        _moe_down_kernel,
        grid_spec=pltpu.PrefetchScalarGridSpec(
            num_scalar_prefetch=2,
            grid=(d // tn, nt),
            in_specs=[pl.BlockSpec((MOE_TILE, f), row_map),
                      pl.BlockSpec((1, f, tn), w_map),
                      pl.BlockSpec((1, 1, tn), w_map)],
            out_specs=pl.BlockSpec((MOE_TILE, tn), out_map),
            scratch_shapes=[pltpu.VMEM((f, tn), BF16)]),
        out_shape=jax.ShapeDtypeStruct((rows, d), F32),
        compiler_params=_params(("arbitrary", "arbitrary")),
        name="moe_down",
    )(tile_expert, n_used, h, w_down, b_down.reshape(ne, 1, d))


def _moe_combine_kernel(idx_ref, nxt_ref, y_hbm, w_ref, x_ref, gt_ref, g_ref, o_ref, buf, sem, *, tok):
    slot = _gather_step(idx_ref, nxt_ref, y_hbm, buf, sem, TOP_K * tok)
    w = w_ref[...]
    acc = jnp.zeros(x_ref.shape, F32)
    for k in range(TOP_K):
        acc = acc + w[:, k:k + 1] * buf[slot, k * tok:(k + 1) * tok, :]
    x = x_ref[...] + gt_ref[0] * acc
    o_ref[...] = x * lax.rsqrt(jnp.mean(x * x, axis=-1, keepdims=True) + EPS) * g_ref[...]


def _moe_combine(y_sorted, pos, top_w, x1, gt, g_final, *, rows_per_mod, tok):
    t, d = x1.shape
    steps = t // tok
    idx3 = pos[:, :TOP_K].reshape(steps, tok, TOP_K).transpose(0, 2, 1).reshape(steps, 1, TOP_K * tok)
    lanes = top_w.shape[1]
    mod_rows = gt.shape[1]
    return pl.pallas_call(
        functools.partial(_moe_combine_kernel, tok=tok),
        grid=(steps,),
        in_specs=_idx_specs(TOP_K * tok, steps) + [
            pl.BlockSpec(memory_space=pl.ANY),
            pl.BlockSpec((tok, lanes), lambda i: (i, 0)),
            pl.BlockSpec((tok, d), lambda i: (i, 0)),
            pl.BlockSpec((1, mod_rows, d), lambda i: (i * tok // rows_per_mod, 0, 0)),
            pl.BlockSpec((1, d), lambda i: (0, 0))],
        out_specs=pl.BlockSpec((tok, d), lambda i: (i, 0)),
        out_shape=jax.ShapeDtypeStruct((t, d), F32),
        scratch_shapes=[pltpu.VMEM((2, TOP_K * tok, d), F32), pltpu.SemaphoreType.DMA((2,))],
        compiler_params=_params(("arbitrary",)),
        name="moe_combine",
    )(idx3, idx3, y_sorted, top_w, x1, gt, g_final.reshape(1, d))


def _route(top_e, n_tiles):
    flat_e = top_e.reshape(-1)
    n_pairs = flat_e.shape[0]
    onehot = flat_e[:, None] == jnp.arange(N_EXPERTS, dtype=jnp.int32)[None, :]
    csum = jnp.cumsum(onehot.astype(jnp.int32), axis=0)
    rank = jnp.sum(jnp.where(onehot, csum, 0), axis=1) - 1
    counts = csum[-1]
    tiles_e = (counts + MOE_TILE - 1) // MOE_TILE
    tile_end = jnp.cumsum(tiles_e)
    tile_start = tile_end - tiles_e
    pos = tile_start[flat_e] * MOE_TILE + rank
    n_used = tile_end[-1]
    tile_ids = jnp.minimum(jnp.arange(n_tiles, dtype=jnp.int32), n_used - 1)
    tile_expert = jnp.minimum(jnp.searchsorted(tile_end, tile_ids, side='right'), N_EXPERTS - 1).astype(jnp.int32)
    src_tok = jnp.zeros((n_tiles * MOE_TILE,), jnp.int32).at[pos].set(
        jnp.arange(n_pairs, dtype=jnp.int32) // TOP_K)
    return pos.astype(jnp.int32), src_tok, tile_expert, n_used.astype(jnp.int32).reshape(1)


def _row_tile(t, want):
    tm = min(t, want)
    assert t % tm == 0
    return tm


def kernel(x_prompt, x_sample, cache_kv_g0, cache_kv_g1, cache_kv_g2, c_prompt, c_sample, w_ada, b_ada, g_mix, w_in, ln_g, ln_b, w_s, b_s, p_a, p_b, w_o, g_ffn, w_router, b_router, w_gate, b_gate, w_up, b_up, w_down, b_down, g_final):
    depth = w_ada.shape[0]
    assert depth == 1, "single-layer trunk"
    bsz, seq, d = x_prompt.shape
    n_s, dec_seq, _ = x_sample.shape
    assert dec_seq == 1, "one new position per sample"
    caches = (cache_kv_g0, cache_kv_g1, cache_kv_g2)
    l = 0
    cols = w_in.shape[2]
    t_p = bsz * seq
    aw = A_GROUPS * CHUNK

    n_c = bsz + n_s
    n_c_pad = -(-n_c // 8) * 8
    c_all = jnp.concatenate([c_prompt, c_sample, jnp.zeros((n_c_pad - n_c, d), F32)], axis=0)
    mod = _ada(c_all, w_ada[l], b_ada[l])
    mod_p = mod[:bsz].reshape(bsz, 1, N_ADA, d)
    mod_s = mod[bsz:n_c].reshape(1, n_s, N_ADA, d)
    sh1_p, sc1_p, gt1_p, sh2_p, sc2_p, gt2_p = (mod_p[:, :, k] for k in range(N_ADA))
    sh1_s, sc1_s, gt1_s, sh2_s, sc2_s, gt2_s = (mod_s[:, :, k] for k in range(N_ADA))

    xp = x_prompt.reshape(t_p, d)
    xs = x_sample.reshape(n_s, d)

    tm_big = _row_tile(seq, 1024)
    z_p = _in_proj(xp, sc1_p, sh1_p, g_mix[l], w_in[l], rows_per_mod=seq, tm=tm_big, hi=False)
    parts_o, parts_l = [], []
    for g, (win, dil) in enumerate(DIL_GROUPS):
        z3 = z_p.reshape(bsz, seq // dil, dil * cols)
        o, lse = _dil_attn(z3, g, bsz, seq, cols)
        parts_o.append(o.reshape(t_p, GROUP_COLS))
        parts_l.append(lse.reshape(t_p, GROUP_COLS))
    a_p = _chunk_gate(z_p, ln_g[l], ln_b[l], w_s[l], b_s[l], tm=_row_tile(seq, 512))
    merged_p = _merge(a_p, tuple(parts_o + parts_l), z_p, p_a[l], p_b[l], tm=_row_tile(seq, 512), hi=False)
    x1_p = _out_proj(merged_p, w_o[l], xp, gt1_p, rows_per_mod=seq, tm=tm_big, hi=False)
    h2_p, e_p, w_p = _ffn_norm(x1_p, sc2_p, sh2_p, g_ffn[l], w_router[l], b_router[l],
                               rows_per_mod=seq, tm=_row_tile(seq, 512))

    z_s = _in_proj(xs, sc1_s, sh1_s, g_mix[l], w_in[l], rows_per_mod=n_s, tm=n_s, hi=True)
    b_s_out = _sample_attn(z_s, tuple(c[l] for c in caches))
    a_s, vn_s = _sample_gate(z_s, ln_g[l], ln_b[l], w_s[l], b_s[l])
    merged_s = _merge(a_s, (b_s_out,), z_s, p_a[l], p_b[l], tm=n_s, hi=True)
    x1_s = _out_proj(merged_s, w_o[l], xs, gt1_s, rows_per_mod=n_s, tm=n_s, hi=True)
    h2_s, e_s, w_s_top = _ffn_norm(x1_s, sc2_s, sh2_s, g_ffn[l], w_router[l], b_router[l],
                                   rows_per_mod=n_s, tm=n_s)

    t_all = t_p + n_s
    h2_all = jnp.concatenate([h2_p, h2_s], axis=0)
    top_e = jnp.concatenate([e_p[:, :TOP_K], e_s[:, :TOP_K]], axis=0)
    n_tiles = (t_all * TOP_K + N_EXPERTS * (MOE_TILE - 1)) // MOE_TILE
    pos, src_tok, tile_expert, n_used = _route(top_e, n_tiles)
    x_sorted = _row_gather(h2_all, src_tok, BF16)
    y_sorted = _moe_experts(x_sorted, tile_expert, n_used, w_gate[l], b_gate[l], w_up[l], b_up[l],
                            w_down[l], b_down[l])
    pos2 = pos.reshape(t_all, TOP_K)
    y_p = _moe_combine(y_sorted, pos2[:t_p], w_p, x1_p, gt2_p, g_final, rows_per_mod=seq,
                       tok=_row_tile(seq, 128))
    y_s = _moe_combine(y_sorted, pos2[t_p:], w_s_top, x1_s, gt2_s, g_final, rows_per_mod=n_s, tok=n_s)

    n_grp = len(DIL_GROUPS)
    k0 = 2 * aw + n_grp * GROUP_COLS
    v0 = k0 + n_grp * GROUP_COLS
    z_p3 = z_p.reshape(bsz, seq, cols)
    kv_prompt, kv_sample = [], []
    for g, (win, dil) in enumerate(DIL_GROUPS):
        keep = min(win, seq)
        kc = slice(k0 + g * GROUP_COLS, k0 + (g + 1) * GROUP_COLS)
        vc = slice(v0 + g * GROUP_COLS, v0 + (g + 1) * GROUP_COLS)
        kv = jnp.stack([z_p3[:, seq - keep:, kc], z_p3[:, seq - keep:, vc]], axis=2)
        kv_prompt.append(kv.reshape(1, bsz, keep, 2, HEADS_PER_GROUP, HEAD_DIM))
        kvs = jnp.stack([z_s[:, kc], z_s[:, vc]], axis=1)
        kv_sample.append(kvs.reshape(1, n_s, 1, 2, HEADS_PER_GROUP, HEAD_DIM))
    return (y_p.reshape(bsz, seq, d), y_s.reshape(n_s, 1, d),
            kv_prompt[0], kv_prompt[1], kv_prompt[2],
            kv_sample[0], kv_sample[1], kv_sample[2],
            vn_s.reshape(1, n_s, 1, aw))
```

```python
import functools

import jax
import jax.numpy as jnp
from jax import lax
from jax.experimental import pallas as pl
from jax.experimental.pallas import tpu as pltpu

F32 = jnp.float32
BF16 = jnp.bfloat16
U32 = jnp.uint32
I32 = jnp.int32
HIGHEST = lax.Precision.HIGHEST

EPS = 1e-6
A_GROUPS = 8
CHUNK = 128
HEAD_DIM = 128
HEADS_PER_GROUP = 4
DIL_GROUPS = ((128, 1), (512, 4), (2048, 16))
N_GROUPS = len(DIL_GROUPS)
N_HEADS = HEADS_PER_GROUP * N_GROUPS
ALIBI_SLOPES = tuple(2.0 ** (-8.0 * (h + 1) / N_HEADS) for h in range(N_HEADS))
GROUP_COLS = HEADS_PER_GROUP * HEAD_DIM
N_EXPERTS = 32
TOP_K = 4
SWIGLU_LIMIT = 7.0
SWIGLU_ALPHA = 1.702
N_ADA = 6
NEG = -1e30

LANES = 128
SUBLANES = 8
COL_TILE = 512
ATTN_BLOCK = 2048
MOE_TILE = 512
MOE_HALF = MOE_TILE // 2
ROUTE_TILE = 512
VMEM_LIMIT = 56 * 1024 * 1024


def _params(sem, vmem=VMEM_LIMIT):
    return pltpu.CompilerParams(dimension_semantics=sem, vmem_limit_bytes=vmem)


def _dot(a, b, hi):
    if hi:
        return jnp.dot(a, b, precision=HIGHEST, preferred_element_type=F32)
    return jnp.dot(a.astype(BF16), b.astype(BF16), preferred_element_type=F32)


def _dot_t(a, b, hi):
    dn = (((1,), (1,)), ((), ()))
    if hi:
        return lax.dot_general(a, b, dn, precision=HIGHEST, preferred_element_type=F32)
    return lax.dot_general(a.astype(BF16), b.astype(BF16), dn, preferred_element_type=F32)


def _mod_rms(x, g, sc, sh):
    y = x * lax.rsqrt(jnp.mean(x * x, axis=-1, keepdims=True) + EPS)
    return y * g * (1.0 + sc) + sh


def _ada_kernel(c_ref, w_ref, b_ref, o_ref):
    c = c_ref[...]
    o_ref[...] = _dot(c * jax.nn.sigmoid(c), w_ref[...], True) + b_ref[...]


def _ada(c_all, w_ada, b_ada):
    n, d = c_all.shape
    cols = w_ada.shape[1]
    tn = 1024
    return pl.pallas_call(
        _ada_kernel,
        grid=(cols // tn,),
        in_specs=[pl.BlockSpec((n, d), lambda j: (0, 0)),
                  pl.BlockSpec((d, tn), lambda j: (0, j)),
                  pl.BlockSpec((1, tn), lambda j: (0, j))],
        out_specs=pl.BlockSpec((n, tn), lambda j: (0, j)),
        out_shape=jax.ShapeDtypeStruct((n, cols), F32),
        compiler_params=_params(("arbitrary",)),
        name="ada",
    )(c_all, w_ada, b_ada.reshape(1, cols))


def _in_proj_kernel(x_ref, sc_ref, sh_ref, g_ref, w_ref, o_ref, h_ref, *, hi, n_gelu, n_raw):
    j = pl.program_id(1)

    @pl.when(j == 0)
    def _():
        h_ref[...] = _mod_rms(x_ref[...], g_ref[...], sc_ref[0], sh_ref[0]).astype(h_ref.dtype)

    z = _dot(h_ref[...], w_ref[...], hi)

    @pl.when(j < n_gelu)
    def _():
        o_ref[...] = jax.nn.gelu(z)

    @pl.when((j >= n_gelu) & (j < n_gelu + n_raw))
    def _():
        o_ref[...] = z

    @pl.when(j >= n_gelu + n_raw)
    def _():
        o_ref[...] = jax.nn.sigmoid(z)


def _in_proj(x, sc, sh, g, w_in, *, rows_per_mod, tm, hi):
    t, d = x.shape
    cols = w_in.shape[1]
    a_width = A_GROUPS * CHUNK
    n_gelu = 2 * a_width // COL_TILE
    n_raw = 3 * N_HEADS * HEAD_DIM // COL_TILE
    mod_rows = sc.shape[1]
    kern = functools.partial(_in_proj_kernel, hi=hi, n_gelu=n_gelu, n_raw=n_raw)
    return pl.pallas_call(
        kern,
        grid=(t // tm, cols // COL_TILE),
        in_specs=[pl.BlockSpec((tm, d), lambda i, j: (i, 0)),
                  pl.BlockSpec((1, mod_rows, d), lambda i, j: (i * tm // rows_per_mod, 0, 0)),
                  pl.BlockSpec((1, mod_rows, d), lambda i, j: (i * tm // rows_per_mod, 0, 0)),
                  pl.BlockSpec((1, d), lambda i, j: (0, 0)),
                  pl.BlockSpec((d, COL_TILE), lambda i, j: (0, j))],
        out_specs=pl.BlockSpec((tm, COL_TILE), lambda i, j: (i, j)),
        out_shape=jax.ShapeDtypeStruct((t, cols), F32),
        scratch_shapes=[pltpu.VMEM((tm, d), F32 if hi else BF16)],
        compiler_params=_params(("arbitrary", "arbitrary")),
        name="in_proj",
    )(x, sc, sh, g.reshape(1, d), w_in)


def _attend(q, kc, kp, vc, vp, slope, dil, prev_bias):
    n = q.shape[0]
    row = lax.broadcasted_iota(I32, (n, CHUNK, CHUNK), 1)
    col = lax.broadcasted_iota(I32, (n, CHUNK, CHUNK), 2)
    dist_c = ((row - col) * dil).astype(F32)
    dist_p = ((row + CHUNK - col) * dil).astype(F32)
    scale = HEAD_DIM ** -0.5
    qk = (((2,), (2,)), ((0,), (0,)))
    pv = (((2,), (1,)), ((0,), (0,)))
    s_c = lax.dot_general(q, kc, qk, preferred_element_type=F32) * scale - slope * dist_c
    s_p = lax.dot_general(q, kp, qk, preferred_element_type=F32) * scale - slope * dist_p + prev_bias
    s_c = jnp.where(col <= row, s_c, NEG)
    s_p = jnp.where(col >= row, s_p, NEG)
    m = jnp.maximum(jnp.max(s_c, axis=2, keepdims=True), jnp.max(s_p, axis=2, keepdims=True))
    p_c = jnp.exp(s_c - m)
    p_p = jnp.exp(s_p - m)
    l = jnp.sum(p_c, axis=2, keepdims=True) + jnp.sum(p_p, axis=2, keepdims=True)
    o = (lax.dot_general((p_c / l).astype(BF16), vc, pv, preferred_element_type=F32)
         + lax.dot_general((p_p / l).astype(BF16), vp, pv, preferred_element_type=F32))
    return o, m + jnp.log(l)


def _dil_attn_kernel(slope_ref, *refs):
    ins = refs[:5 * N_GROUPS]
    o_ref = refs[5 * N_GROUPS]
    acc_refs = refs[5 * N_GROUPS + 1:5 * N_GROUPS + 1 + N_GROUPS]
    lse_refs = refs[5 * N_GROUPS + 1 + N_GROUPS:]
    head = pl.program_id(1)
    i = pl.program_id(2)
    no_prev = jnp.where(i > 0, 0.0, NEG)

    for g, (win, dil) in enumerate(DIL_GROUPS):
        q_ref, k_ref, v_ref, kp_ref, vp_ref = ins[5 * g:5 * g + 5]
        acc_ref, lse_ref = acc_refs[g], lse_refs[g]
        slope = slope_ref[g * HEADS_PER_GROUP + head]
        span = CHUNK * dil
        n_blocks = ATTN_BLOCK // CHUNK

        def rows(start, dil=dil):
            return pl.ds(start, CHUNK) if dil == 1 else pl.ds(start, CHUNK, stride=dil)

        cur = [rows((b // dil) * span + b % dil) for b in range(n_blocks)]
        prev = [rows((b // dil - 1) * span + b % dil) for b in range(n_blocks)]

        def gather(ref, first_ref=None):
            tiles = []
            for b in range(n_blocks):
                if first_ref is None:
                    tiles.append(ref[cur[b], :])
                elif b < dil:
                    tiles.append(first_ref[rows(b), :])
                else:
                    tiles.append(ref[prev[b], :])
            return jnp.stack(tiles).astype(BF16)

        first_blocks = lax.broadcasted_iota(I32, (n_blocks, 1, 1), 0) < dil
        o, lse = _attend(gather(q_ref), gather(k_ref), gather(k_ref, kp_ref), gather(v_ref), gather(v_ref, vp_ref),
                         slope, dil, jnp.where(first_blocks, no_prev, 0.0))
        for b in range(n_blocks):
            acc_ref[cur[b], :] = o[b]
            lse_ref[cur[b], :] = jnp.broadcast_to(lse[b], (CHUNK, HEAD_DIM))

    ls = [r[...] for r in lse_refs]
    m = jnp.maximum(jnp.maximum(ls[0], ls[1]), ls[2])
    e = [jnp.exp(l - m) for l in ls]
    den = e[0] + e[1] + e[2]
    o_ref[...] = (acc_refs[0][...] * (e[0] / den) + acc_refs[1][...] * (e[1] / den)
                  + acc_refs[2][...] * (e[2] / den))


def _dil_attn(z, bsz, seq):
    assert seq % ATTN_BLOCK == 0
    nblk = seq // ATTN_BLOCK
    a_blocks = 2 * A_GROUPS * CHUNK // HEAD_DIM
    in_specs = [pl.BlockSpec(memory_space=pltpu.SMEM)]
    for g, (win, dil) in enumerate(DIL_GROUPS):
        span = CHUNK * dil
        per_blk = ATTN_BLOCK // span
        for which in range(3):
            cb = a_blocks + (which * N_GROUPS + g) * HEADS_PER_GROUP
            in_specs.append(pl.BlockSpec((ATTN_BLOCK, HEAD_DIM), lambda b, h, i, cb=cb: (b * nblk + i, cb + h)))
        for which in (1, 2):
            cb = a_blocks + (which * N_GROUPS + g) * HEADS_PER_GROUP
            in_specs.append(pl.BlockSpec(
                (span, HEAD_DIM),
                lambda b, h, i, cb=cb, per_blk=per_blk: (jnp.maximum((b * nblk + i) * per_blk - 1, 0), cb + h)))
    slopes = jnp.asarray(ALIBI_SLOPES, F32)
    blk = pltpu.VMEM((ATTN_BLOCK, HEAD_DIM), F32)
    return pl.pallas_call(
        _dil_attn_kernel,
        grid=(bsz, HEADS_PER_GROUP, nblk),
        in_specs=in_specs,
        out_specs=pl.BlockSpec((ATTN_BLOCK, HEAD_DIM), lambda b, h, i: (b * nblk + i, h)),
        out_shape=jax.ShapeDtypeStruct((bsz * seq, GROUP_COLS), F32),
        scratch_shapes=[blk] * (2 * N_GROUPS),
        compiler_params=_params(("arbitrary", "arbitrary", "arbitrary")),
        name="dil_attn",
    )(slopes, *([z] * (5 * N_GROUPS)))


def _sample_attn_kernel(q_ref, k_ref, v_ref, c0_ref, c1_ref, c2_ref, o_ref):
    caches = (c0_ref, c1_ref, c2_ref)
    scale = HEAD_DIM ** -0.5
    lane = lax.broadcasted_iota(I32, (1, CHUNK), 1)
    outs, lses = [], []
    for g, (win, dil) in enumerate(DIL_GROUPS):
        n_buf = win // dil
        dist = ((n_buf - lane) * dil).astype(F32)
        go, gl = [], []
        for h in range(HEADS_PER_GROUP):
            hs = slice(g * GROUP_COLS + h * HEAD_DIM, g * GROUP_COLS + (h + 1) * HEAD_DIM)
            slope = ALIBI_SLOPES[g * HEADS_PER_GROUP + h]
            q = q_ref[0, :, hs]
            k_new = k_ref[0, :, hs]
            v_new = v_ref[0, :, hs]
            q8 = jnp.broadcast_to(q, (SUBLANES, HEAD_DIM))
            s_buf = _dot_t(q8, caches[g][:, 0, h, :], True)[0:1] * scale - slope * dist
            s_new = jnp.sum(q * k_new, axis=1, keepdims=True) * scale
            m = jnp.maximum(jnp.max(s_buf, axis=1, keepdims=True), s_new)
            p_buf = jnp.exp(s_buf - m)
            p_new = jnp.exp(s_new - m)
            l = jnp.sum(p_buf, axis=1, keepdims=True) + p_new
            pb8 = jnp.broadcast_to(p_buf / l, (SUBLANES, CHUNK))
            o = _dot(pb8, caches[g][:, 1, h, :], True)[0:1] + (p_new / l) * v_new
            go.append(o)
            gl.append(m + jnp.log(l))
        outs.append(go)
        lses.append(gl)
    for h in range(HEADS_PER_GROUP):
        m = jnp.maximum(jnp.maximum(lses[0][h], lses[1][h]), lses[2][h])
        e = [jnp.exp(lses[g][h] - m) for g in range(N_GROUPS)]
        den = e[0] + e[1] + e[2]
        o = outs[0][h] * (e[0] / den) + outs[1][h] * (e[1] / den) + outs[2][h] * (e[2] / den)
        o_ref[0, :, h * HEAD_DIM:(h + 1) * HEAD_DIM] = o


def _sample_attn(zs, caches):
    n, cols = zs.shape
    z3 = zs.reshape(n, 1, cols)
    a_blocks = 2 * A_GROUPS * CHUNK // GROUP_COLS
    qkv_cols = N_GROUPS * GROUP_COLS
    c_views = []
    c_specs = []
    for g, (win, dil) in enumerate(DIL_GROUPS):
        c = caches[g]
        assert c.shape[1] == win, "cache must hold exactly one window"
        c_views.append(c.reshape(n, win // dil, dil, 2, HEADS_PER_GROUP, HEAD_DIM))
        c_specs.append(pl.BlockSpec((None, win // dil, None, 2, HEADS_PER_GROUP, HEAD_DIM),
                                    lambda b: (b, 0, 0, 0, 0, 0)))
    q0 = a_blocks * GROUP_COLS
    row_spec = pl.BlockSpec((1, 1, qkv_cols), lambda b: (b, 0, 0))
    out = pl.pallas_call(
        _sample_attn_kernel,
        grid=(n,),
        in_specs=[row_spec, row_spec, row_spec] + c_specs,
        out_specs=pl.BlockSpec((1, 1, GROUP_COLS), lambda b: (b, 0, 0)),
        out_shape=jax.ShapeDtypeStruct((n, 1, GROUP_COLS), F32),
        compiler_params=_params(("arbitrary",)),
        name="sample_attn",
    )(z3[:, :, q0:q0 + qkv_cols], z3[:, :, q0 + qkv_cols:q0 + 2 * qkv_cols],
      z3[:, :, q0 + 2 * qkv_cols:q0 + 3 * qkv_cols], *c_views)
    return out.reshape(n, GROUP_COLS)


def _layer_norm(v, g, b):
    mu = jnp.mean(v, axis=-1, keepdims=True)
    var = jnp.mean(jnp.square(v - mu), axis=-1, keepdims=True)
    return (v - mu) * lax.rsqrt(var + EPS) * g + b


def _chunk_gate_kernel(u_ref, v_ref, lg_ref, lb_ref, ws_ref, bs_ref, o_ref, *, n_chunks):
    va = _layer_norm(v_ref[...], lg_ref[...], lb_ref[...]).astype(BF16)
    row = lax.broadcasted_iota(I32, (CHUNK, CHUNK), 0)
    col = lax.broadcasted_iota(I32, (CHUNK, CHUNK), 1)
    for g in range(A_GROUPS):
        w = jnp.where(col <= row, ws_ref[g], 0.0).astype(BF16)
        gs = slice(g * CHUNK, (g + 1) * CHUNK)
        for c in range(n_chunks):
            rs = slice(c * CHUNK, (c + 1) * CHUNK)
            mixed = jnp.dot(w, va[rs, gs], preferred_element_type=F32) + bs_ref[g]
            o_ref[rs, gs] = (u_ref[rs, gs] * mixed).astype(o_ref.dtype)


def _chunk_gate(z, ln_g, ln_b, w_s, b_s, *, tm):
    t = z.shape[0]
    aw = A_GROUPS * CHUNK
    bs_full = jnp.broadcast_to(b_s[:, :, None], (A_GROUPS, CHUNK, CHUNK))
    kern = functools.partial(_chunk_gate_kernel, n_chunks=tm // CHUNK)
    return pl.pallas_call(
        kern,
        grid=(t // tm,),
        in_specs=[pl.BlockSpec((tm, aw), lambda i: (i, 0)),
                  pl.BlockSpec((tm, aw), lambda i: (i, 1)),
                  pl.BlockSpec((1, aw), lambda i: (0, 0)),
                  pl.BlockSpec((1, aw), lambda i: (0, 0)),
                  pl.BlockSpec((A_GROUPS, CHUNK, CHUNK), lambda i: (0, 0, 0)),
                  pl.BlockSpec((A_GROUPS, CHUNK, CHUNK), lambda i: (0, 0, 0))],
        out_specs=pl.BlockSpec((tm, aw), lambda i: (i, 0)),
        out_shape=jax.ShapeDtypeStruct((t, aw), BF16),
        compiler_params=_params(("arbitrary",)),
        name="chunk_gate",
    )(z, z, ln_g.reshape(1, aw), ln_b.reshape(1, aw), w_s, bs_full)


def _sample_gate_kernel(u_ref, v_ref, lg_ref, lb_ref, w0_ref, b0_ref, a_ref, vn_ref):
    va = _layer_norm(v_ref[...], lg_ref[...], lb_ref[...])
    vn_ref[...] = va
    a_ref[...] = u_ref[...] * (w0_ref[...] * va + b0_ref[...])


def _sample_gate(zs, ln_g, ln_b, w_s, b_s):
    n = zs.shape[0]
    aw = A_GROUPS * CHUNK
    w0 = jnp.repeat(w_s[:, 0, 0], CHUNK).reshape(1, aw)
    b0 = jnp.repeat(b_s[:, 0], CHUNK).reshape(1, aw)
    vec = pl.BlockSpec((1, aw), lambda i: (0, 0))
    return pl.pallas_call(
        _sample_gate_kernel,
        grid=(1,),
        in_specs=[pl.BlockSpec((n, aw), lambda i: (0, 0)), pl.BlockSpec((n, aw), lambda i: (0, 1)),
                  vec, vec, vec, vec],
        out_specs=[pl.BlockSpec((n, aw), lambda i: (0, 0)), pl.BlockSpec((n, aw), lambda i: (0, 0))],
        out_shape=[jax.ShapeDtypeStruct((n, aw), F32), jax.ShapeDtypeStruct((n, aw), F32)],
        compiler_params=_params(("arbitrary",)),
        name="sample_gate",
    )(zs, zs, ln_g.reshape(1, aw), ln_b.reshape(1, aw), w0, b0)


def _merge_kernel(a_ref, b_ref, ga_ref, gb_ref, pa_ref, pb_ref, out_ref, *, hi):
    ya = _dot(a_ref[...], pa_ref[...], hi)
    yb = _dot(b_ref[...], pb_ref[...], hi)
    out_ref[...] = (ga_ref[...] * ya + gb_ref[...] * yb).astype(out_ref.dtype)


def _merge(a_out, b_out, z, p_a, p_b, *, tm, hi):
    t, aw = a_out.shape
    d = p_a.shape[1]
    tn = COL_TILE
    cols = z.shape[1]
    ga_blk0 = (cols - 2 * d) // tn
    gb_blk0 = (cols - d) // tn
    return pl.pallas_call(
        functools.partial(_merge_kernel, hi=hi),
        grid=(t // tm, d // tn),
        in_specs=[pl.BlockSpec((tm, aw), lambda i, j: (i, 0)),
                  pl.BlockSpec((tm, GROUP_COLS), lambda i, j: (i, 0)),
                  pl.BlockSpec((tm, tn), lambda i, j: (i, ga_blk0 + j)),
                  pl.BlockSpec((tm, tn), lambda i, j: (i, gb_blk0 + j)),
                  pl.BlockSpec((aw, tn), lambda i, j: (0, j)),
                  pl.BlockSpec((GROUP_COLS, tn), lambda i, j: (0, j))],
        out_specs=pl.BlockSpec((tm, tn), lambda i, j: (i, j)),
        out_shape=jax.ShapeDtypeStruct((t, d), F32 if hi else BF16),
        compiler_params=_params(("arbitrary", "arbitrary")),
        name="merge",
    )(a_out, b_out, z, z, p_a, p_b)


def _out_proj_kernel(m_ref, w_ref, x_ref, gt_ref, o_ref, *, hi):
    o_ref[...] = x_ref[...] + gt_ref[0] * _dot(m_ref[...], w_ref[...], hi)


def _out_proj(merged, w_o, x, gt, *, rows_per_mod, tm, hi):
    t, d = x.shape
    tn = COL_TILE
    mod_rows = gt.shape[1]
    return pl.pallas_call(
        functools.partial(_out_proj_kernel, hi=hi),
        grid=(t // tm, d // tn),
        in_specs=[pl.BlockSpec((tm, d), lambda i, j: (i, 0)),
                  pl.BlockSpec((d, tn), lambda i, j: (0, j)),
                  pl.BlockSpec((tm, tn), lambda i, j: (i, j)),
                  pl.BlockSpec((1, mod_rows, tn), lambda i, j: (i * tm // rows_per_mod, 0, j))],
        out_specs=pl.BlockSpec((tm, tn), lambda i, j: (i, j)),
        out_shape=jax.ShapeDtypeStruct((t, d), F32),
        compiler_params=_params(("arbitrary", "arbitrary")),
        name="out_proj",
    )(merged, w_o, x, gt)


def _pack_pair(lo, hi, pair_ref):
    n = lo.shape[0]
    pair_ref[pl.ds(0, n, stride=2), :] = lo
    pair_ref[pl.ds(1, n, stride=2), :] = hi
    return pltpu.bitcast(pair_ref[0:2 * n, :].astype(BF16), U32)


def _unpack_pair(w, pair_ref):
    n = w.shape[0]
    pair_ref[0:2 * n, :] = pltpu.bitcast(w, BF16).astype(F32)
    return pair_ref[pl.ds(0, n, stride=2), :].astype(BF16), pair_ref[pl.ds(1, n, stride=2), :].astype(BF16)


def _ffn_norm_kernel(x_ref, sc_ref, sh_ref, g_ref, wr_ref, br_ref, hp_ref, e_ref, w_ref, pair_ref):
    h = _mod_rms(x_ref[...], g_ref[...], sc_ref[0], sh_ref[0])
    tm, d = h.shape
    for s in range(d // (2 * LANES)):
        lo = h[:, (2 * s) * LANES:(2 * s + 1) * LANES]
        hi = h[:, (2 * s + 1) * LANES:(2 * s + 2) * LANES]
        hp_ref[pl.ds(s, tm, stride=SUBLANES), :] = _pack_pair(lo, hi, pair_ref)
    logits = _dot(h, wr_ref[...], True) + br_ref[...]
    lane = lax.broadcasted_iota(I32, logits.shape, 1)
    lane_f = lane.astype(F32)
    vals, idxs = [], []
    for _ in range(TOP_K):
        m = jnp.max(logits, axis=1, keepdims=True)
        idx = jnp.min(jnp.where(logits == m, lane_f, float(logits.shape[1])), axis=1, keepdims=True)
        vals.append(m)
        idxs.append(idx)
        logits = jnp.where(lane_f == idx, 2.0 * NEG, logits)
    es = [jnp.exp(v - vals[0]) for v in vals]
    den = es[0] + es[1] + es[2] + es[3]
    e_out = jnp.full(lane.shape, -1.0, F32)
    w_out = jnp.zeros(lane.shape, F32)
    for k in range(TOP_K):
        e_out = jnp.where(lane == k, idxs[k], e_out)
        w_out = jnp.where(lane == k, es[k] / den, w_out)
    e_ref[...] = e_out.astype(I32)
    w_ref[...] = w_out


def _ffn_norm(x1, sc, sh, g, w_router, b_router, *, rows_per_mod, tm):
    t, d = x1.shape
    assert d == 2 * LANES * SUBLANES, "one packed row must be exactly one (8, 128) tile"
    ne = w_router.shape[1]
    wr = jnp.zeros((d, LANES), F32).at[:, :ne].set(w_router)
    br = jnp.full((1, LANES), NEG, F32).at[0, :ne].set(b_router)
    mod_rows = sc.shape[1]
    mod_spec = pl.BlockSpec((1, mod_rows, d), lambda i: (i * tm // rows_per_mod, 0, 0))
    return pl.pallas_call(
        _ffn_norm_kernel,
        grid=(t // tm,),
        in_specs=[pl.BlockSpec((tm, d), lambda i: (i, 0)), mod_spec, mod_spec,
                  pl.BlockSpec((1, d), lambda i: (0, 0)),
                  pl.BlockSpec((d, LANES), lambda i: (0, 0)),
                  pl.BlockSpec((1, LANES), lambda i: (0, 0))],
        out_specs=[pl.BlockSpec((tm * SUBLANES, LANES), lambda i: (i, 0)),
                   pl.BlockSpec((tm, LANES), lambda i: (i, 0)),
                   pl.BlockSpec((tm, LANES), lambda i: (i, 0))],
        out_shape=[jax.ShapeDtypeStruct((t * SUBLANES, LANES), U32),
                   jax.ShapeDtypeStruct((t, LANES), I32),
                   jax.ShapeDtypeStruct((t, LANES), F32)],
        scratch_shapes=[pltpu.VMEM((2 * tm, LANES), F32)],
        compiler_params=_params(("arbitrary",)),
        name="ffn_norm",
    )(x1, sc, sh, g.reshape(1, d), wr, br)


def _route_kernel(e_ref, pos_ref, cnt_ref, tri_ref, run_ref):
    phase = pl.program_id(0)
    i = pl.program_id(1)
    e = e_ref[...]
    tm = e.shape[0]
    lane = lax.broadcasted_iota(I32, (tm, LANES), 1)
    hits = [lane == e[:, k:k + 1] for k in range(TOP_K)]
    chosen = jnp.zeros((tm, LANES), F32)
    for k in range(TOP_K):
        chosen = jnp.where(hits[k], 1.0, chosen)
    col_count = jnp.sum(chosen, axis=0, keepdims=True)

    @pl.when((phase == 0) & (i == 0))
    def _():
        cnt_ref[...] = jnp.zeros(cnt_ref.shape, F32)
        r = lax.broadcasted_iota(I32, (tm, tm), 0)
        c = lax.broadcasted_iota(I32, (tm, tm), 1)
        tri_ref[...] = jnp.where(c < r, 1.0, 0.0).astype(BF16)

    @pl.when(phase == 0)
    def _():
        cnt_ref[...] = cnt_ref[...] + col_count

    @pl.when((phase == 1) & (i == 0))
    def _():
        tiles = jnp.floor((cnt_ref[...] + (MOE_TILE - 1)) * (1.0 / MOE_TILE))
        r = lax.broadcasted_iota(I32, (LANES, LANES), 0)
        c = lax.broadcasted_iota(I32, (LANES, LANES), 1)
        below = jnp.where(r < c, 1.0, 0.0)
        tiles8 = jnp.broadcast_to(tiles, (SUBLANES, LANES))
        run_ref[...] = _dot(tiles8, below, True)[0:1] * float(MOE_TILE)

    @pl.when(phase == 1)
    def _():
        before = jnp.dot(tri_ref[...], chosen.astype(BF16), preferred_element_type=F32)
        dest = run_ref[...] + before
        out = jnp.zeros((tm, LANES), F32)
        for k in range(TOP_K):
            p = jnp.sum(jnp.where(hits[k], dest, 0.0), axis=1, keepdims=True)
            out = jnp.where(lane == k, p, out)
        pos_ref[...] = out.astype(I32)
        run_ref[...] = run_ref[...] + col_count


def _route(e_all):
    t = e_all.shape[0]
    steps = t // ROUTE_TILE
    return pl.pallas_call(
        _route_kernel,
        grid=(2, steps),
        in_specs=[pl.BlockSpec((ROUTE_TILE, LANES), lambda p, i: (i, 0))],
        out_specs=[pl.BlockSpec((ROUTE_TILE, LANES), lambda p, i: (i * p, 0)),
                   pl.BlockSpec((1, LANES), lambda p, i: (0, 0))],
        out_shape=[jax.ShapeDtypeStruct((t, LANES), I32), jax.ShapeDtypeStruct((1, LANES), F32)],
        scratch_shapes=[pltpu.VMEM((ROUTE_TILE, ROUTE_TILE), BF16), pltpu.VMEM((1, LANES), F32)],
        compiler_params=_params(("arbitrary", "arbitrary")),
        name="route",
    )(e_all)


def _row_scatter_kernel(pos_ref, pos_s_ref, pad_base_ref, pad_cnt_ref, hp_hbm, hs_hbm, xs_hbm, sem, *,
                        n_prompt_steps, n_sample):
    i = pl.program_id(0)
    tile_rows = ROUTE_TILE * TOP_K

    def copy(src_hbm, src_row, dst_row):
        return pltpu.make_async_copy(src_hbm.at[src_row], xs_hbm.at[dst_row], sem)

    @pl.when(i < n_prompt_steps)
    def _():
        def body(tt, carry):
            for k in range(TOP_K):
                copy(hp_hbm, i * ROUTE_TILE + tt, pos_ref[0, 0, tt * TOP_K + k]).start()
            return carry
        lax.fori_loop(0, ROUTE_TILE, body, 0)
        pltpu.make_async_copy(xs_hbm.at[pl.ds(0, tile_rows)], xs_hbm.at[pl.ds(0, tile_rows)], sem).wait()

    @pl.when(i == n_prompt_steps)
    def _():
        def body(tt, carry):
            for k in range(TOP_K):
                copy(hs_hbm, tt, pos_s_ref[0, 0, tt * TOP_K + k]).start()
            return carry
        lax.fori_loop(0, n_sample, body, 0)
        n = n_sample * TOP_K
        pltpu.make_async_copy(xs_hbm.at[pl.ds(0, n)], xs_hbm.at[pl.ds(0, n)], sem).wait()

        def fill(dst_row, size):
            return pltpu.make_async_copy(hp_hbm.at[pl.ds(0, size)], xs_hbm.at[pl.ds(dst_row, size)], sem)

        def per_segment(wait):
            def body(e, carry):
                cnt = pad_cnt_ref[e]
                base = pad_base_ref[e]

                def whole(r, c):
                    cp = fill(base + (cnt % MOE_TILE) + r * MOE_TILE, MOE_TILE)
                    cp.wait() if wait else cp.start()
                    return c
                lax.fori_loop(0, cnt // MOE_TILE, whole, 0)
                size = MOE_TILE // 2
                while size >= 1:
                    @pl.when((cnt & size) != 0)
                    def _(size=size):
                        cp = fill(base + (cnt & (size - 1)), size)
                        cp.wait() if wait else cp.start()
                    size //= 2
                return carry
            return body
        lax.fori_loop(0, pad_cnt_ref.shape[0], per_segment(False), 0)
        lax.fori_loop(0, pad_cnt_ref.shape[0], per_segment(True), 0)


def _row_scatter(hp_p, hp_s, pos_p, pos_s, pad_base, pad_cnt, n_rows):
    t_p = hp_p.shape[0]
    n_s = hp_s.shape[0]
    steps = t_p // ROUTE_TILE
    tile_rows = ROUTE_TILE * TOP_K
    return pl.pallas_call(
        functools.partial(_row_scatter_kernel, n_prompt_steps=steps, n_sample=n_s),
        grid=(steps + 1,),
        in_specs=[pl.BlockSpec((1, 1, tile_rows), lambda i: (jnp.minimum(i, steps - 1), 0, 0),
                               memory_space=pltpu.SMEM),
                  pl.BlockSpec((1, 1, n_s * TOP_K), lambda i: (0, 0, 0), memory_space=pltpu.SMEM),
                  pl.BlockSpec(memory_space=pltpu.SMEM),
                  pl.BlockSpec(memory_space=pltpu.SMEM),
                  pl.BlockSpec(memory_space=pl.ANY),
                  pl.BlockSpec(memory_space=pl.ANY)],
        out_specs=pl.BlockSpec(memory_space=pl.ANY),
        out_shape=jax.ShapeDtypeStruct((n_rows, SUBLANES, LANES), U32),
        scratch_shapes=[pltpu.SemaphoreType.DMA(())],
        compiler_params=_params(("arbitrary",)),
        name="row_scatter",
    )(pos_p.reshape(steps, 1, tile_rows), pos_s.reshape(1, 1, n_s * TOP_K), pad_base, pad_cnt, hp_p, hp_s)


def _moe_up_kernel(te_ref, nv_ref, nu_ref, x_ref, wg_ref, wu_ref, bg_ref, bu_ref, h_ref, wg_s, wu_s, xb_s,
                   pair_ref):
    t = pl.program_id(1)
    e = te_ref[t]
    prev = te_ref[jnp.maximum(t - 1, 0)]

    @pl.when((t == 0) | (e != prev))
    def _():
        wg_s[...] = wg_ref[0].astype(BF16)
        wu_s[...] = wu_ref[0].astype(BF16)

    def compute(rows):
        for s in range(SUBLANES):
            lo, hi = _unpack_pair(x_ref[pl.ds(s, rows, stride=SUBLANES), :], pair_ref)
            xb_s[0:rows, (2 * s) * LANES:(2 * s + 1) * LANES] = lo
            xb_s[0:rows, (2 * s + 1) * LANES:(2 * s + 2) * LANES] = hi
        x = xb_s[0:rows, :]
        gt = jnp.dot(x, wg_s[...], preferred_element_type=F32) + bg_ref[0]
        up = jnp.dot(x, wu_s[...], preferred_element_type=F32) + bu_ref[0]
        gt = jnp.minimum(gt, SWIGLU_LIMIT)
        up = jnp.clip(up, -SWIGLU_LIMIT, SWIGLU_LIMIT)
        h_ref[0:rows, :] = ((up + 1.0) * gt * jax.nn.sigmoid(SWIGLU_ALPHA * gt)).astype(h_ref.dtype)

    active = t < nu_ref[0]
    full = nv_ref[t] > MOE_HALF

    @pl.when(active & full)
    def _():
        compute(MOE_TILE)

    @pl.when(active & jnp.logical_not(full))
    def _():
        compute(MOE_HALF)
        h_ref[MOE_HALF:, :] = jnp.zeros((MOE_TILE - MOE_HALF, h_ref.shape[1]), h_ref.dtype)

    @pl.when(jnp.logical_not(active))
    def _():
        h_ref[...] = jnp.zeros(h_ref.shape, h_ref.dtype)


def _moe_down_kernel(te_ref, nv_ref, nu_ref, h_ref, wd_ref, bd_ref, y_ref, wd_s):
    t = pl.program_id(1)
    e = te_ref[t]
    prev = te_ref[jnp.maximum(t - 1, 0)]
    n_sub = y_ref.shape[2]

    @pl.when((t == 0) | (e != prev))
    def _():
        wd_s[...] = wd_ref[0].astype(BF16)

    def store(rows, y):
        for c in range(n_sub):
            y_ref[0:rows, 0, c, :] = y[:, c * LANES:(c + 1) * LANES]

    def compute(rows):
        store(rows, jnp.dot(h_ref[0:rows, :], wd_s[...], preferred_element_type=F32) + bd_ref[0])

    active = t < nu_ref[0]
    full = nv_ref[t] > MOE_HALF

    @pl.when(active & full)
    def _():
        compute(MOE_TILE)

    @pl.when(active & jnp.logical_not(full))
    def _():
        compute(MOE_HALF)
        y_ref[MOE_HALF:] = jnp.zeros((MOE_TILE - MOE_HALF,) + y_ref.shape[1:], F32)

    @pl.when(jnp.logical_not(active))
    def _():
        y_ref[...] = jnp.zeros(y_ref.shape, F32)


def _moe_experts(x_sorted, tile_expert, tile_valid, n_used, w_gate, b_gate, w_up, b_up, w_down, b_down):
    rows = x_sorted.shape[0]
    ne, d, f = w_gate.shape
    nt = rows // MOE_TILE
    tf = 512
    tn = SUBLANES * LANES
    x2 = x_sorted.reshape(rows * SUBLANES, LANES)

    def row_map(j, t, te, nv, nu):
        return (jnp.minimum(t, nu[0] - 1), 0)

    def w_map(j, t, te, nv, nu):
        return (te[t], 0, j)

    h = pl.pallas_call(
        _moe_up_kernel,
        grid_spec=pltpu.PrefetchScalarGridSpec(
            num_scalar_prefetch=3,
            grid=(f // tf, nt),
            in_specs=[pl.BlockSpec((MOE_TILE * SUBLANES, LANES), row_map),
                      pl.BlockSpec((1, d, tf), w_map),
                      pl.BlockSpec((1, d, tf), w_map),
                      pl.BlockSpec((1, 1, tf), w_map),
                      pl.BlockSpec((1, 1, tf), w_map)],
            out_specs=pl.BlockSpec((MOE_TILE, tf), lambda j, t, te, nv, nu: (t, j)),
            scratch_shapes=[pltpu.VMEM((d, tf), BF16), pltpu.VMEM((d, tf), BF16), pltpu.VMEM((MOE_TILE, d), BF16),
                            pltpu.VMEM((2 * MOE_TILE, LANES), F32)]),
        out_shape=jax.ShapeDtypeStruct((rows, f), BF16),
        compiler_params=_params(("arbitrary", "arbitrary")),
        name="moe_up",
    )(tile_expert, tile_valid, n_used, x2, w_gate, w_up, b_gate.reshape(ne, 1, f), b_up.reshape(ne, 1, f))

    y = pl.pallas_call(
        _moe_down_kernel,
        grid_spec=pltpu.PrefetchScalarGridSpec(
            num_scalar_prefetch=3,
            grid=(d // tn, nt),
            in_specs=[pl.BlockSpec((MOE_TILE, f), row_map),
                      pl.BlockSpec((1, f, tn), w_map),
                      pl.BlockSpec((1, 1, tn), w_map)],
            out_specs=pl.BlockSpec((MOE_TILE, 1, SUBLANES, LANES), lambda j, t, te, nv, nu: (t, j, 0, 0)),
            scratch_shapes=[pltpu.VMEM((f, tn), BF16)]),
        out_shape=jax.ShapeDtypeStruct((rows, d // tn, SUBLANES, LANES), F32),
        compiler_params=_params(("arbitrary", "arbitrary")),
        name="moe_down",
    )(tile_expert, tile_valid, n_used, h, w_down, b_down.reshape(ne, 1, d))
    return y.reshape(rows, d // LANES, LANES)


def _row_copy(src_hbm, buf, sem, slot, src_row, dst_row, n_sub):
    dst = pl.ds(pl.multiple_of(dst_row * n_sub, SUBLANES), n_sub)
    return pltpu.make_async_copy(src_hbm.at[src_row], buf.at[slot, dst], sem.at[slot])


def _issue_rows(idx_ref, src_hbm, buf, sem, slot, n_rows, n_sub):
    def body(r, carry):
        _row_copy(src_hbm, buf, sem, slot, idx_ref[0, 0, r], r, n_sub).start()
        return carry
    lax.fori_loop(0, n_rows, body, 0, unroll=8)


def _moe_combine_kernel(idx_ref, nxt_ref, y_hbm, w_ref, x_ref, gt_ref, g_ref, o_ref, buf, sem, *, tok):
    i = pl.program_id(0)
    n = pl.num_programs(0)
    n_rows = TOP_K * tok
    n_sub = y_hbm.shape[1]
    slot = lax.rem(i, 2)

    @pl.when(i == 0)
    def _():
        _issue_rows(idx_ref, y_hbm, buf, sem, 0, n_rows, n_sub)

    for s in range(2):
        @pl.when((i + 1 < n) & (slot == s))
        def _():
            _issue_rows(nxt_ref, y_hbm, buf, sem, 1 - s, n_rows, n_sub)

    pltpu.make_async_copy(buf.at[slot], buf.at[slot], sem.at[slot]).wait()

    w = w_ref[...]
    sumsq = jnp.zeros((tok, 1), F32)
    for c in range(n_sub):
        cs = slice(c * LANES, (c + 1) * LANES)
        acc = jnp.zeros((tok, LANES), F32)
        for k in range(TOP_K):
            part = buf[slot, pl.ds(k * tok * n_sub + c, tok, stride=n_sub), :]
            acc = acc + w[:, k:k + 1] * part
        x = x_ref[:, cs] + gt_ref[0, :, cs] * acc
        o_ref[:, cs] = x
        sumsq = sumsq + jnp.sum(x * x, axis=1, keepdims=True)
    o_ref[...] = o_ref[...] * lax.rsqrt(sumsq / (n_sub * LANES) + EPS) * g_ref[...]


def _moe_combine(y_rows, pos, top_w, x1, gt, g_final, *, rows_per_mod, tok):
    t, d = x1.shape
    n_sub = y_rows.shape[1]
    steps = t // tok
    n_rows = TOP_K * tok
    idx3 = pos[:, :TOP_K].reshape(steps, tok, TOP_K).transpose(0, 2, 1).reshape(steps, 1, n_rows)
    lanes = top_w.shape[1]
    mod_rows = gt.shape[1]
    idx_blk = (1, 1, n_rows)
    return pl.pallas_call(
        functools.partial(_moe_combine_kernel, tok=tok),
        grid=(steps,),
        in_specs=[pl.BlockSpec(idx_blk, lambda i: (i, 0, 0), memory_space=pltpu.SMEM),
                  pl.BlockSpec(idx_blk, lambda i: (jnp.minimum(i + 1, steps - 1), 0, 0), memory_space=pltpu.SMEM),
                  pl.BlockSpec(memory_space=pl.ANY),
                  pl.BlockSpec((tok, lanes), lambda i: (i, 0)),
                  pl.BlockSpec((tok, d), lambda i: (i, 0)),
                  pl.BlockSpec((1, mod_rows, d), lambda i: (i * tok // rows_per_mod, 0, 0)),
                  pl.BlockSpec((1, d), lambda i: (0, 0))],
        out_specs=pl.BlockSpec((tok, d), lambda i: (i, 0)),
        out_shape=jax.ShapeDtypeStruct((t, d), F32),
        scratch_shapes=[pltpu.VMEM((2, n_rows * n_sub, LANES), F32), pltpu.SemaphoreType.DMA((2,))],
        compiler_params=_params(("arbitrary",)),
        name="moe_combine",
    )(idx3, idx3, y_rows, top_w, x1, gt, g_final.reshape(1, d))


def _tile_table(counts, n_tiles):
    tiles_e = (counts + MOE_TILE - 1) // MOE_TILE
    tile_end = jnp.cumsum(tiles_e)
    tile_start = tile_end - tiles_e
    n_used = tile_end[-1]
    tile_ids = jnp.minimum(jnp.arange(n_tiles, dtype=I32), n_used - 1)
    tile_expert = jnp.minimum(jnp.sum(tile_end[None, :] <= tile_ids[:, None], axis=1), N_EXPERTS - 1).astype(I32)
    tile_valid = jnp.clip(counts[tile_expert] - (tile_ids - tile_start[tile_expert]) * MOE_TILE, 0, MOE_TILE)
    pad_base = jnp.concatenate([tile_start * MOE_TILE + counts, (n_used * MOE_TILE).reshape(1)])
    pad_cnt = jnp.concatenate([tiles_e * MOE_TILE - counts, ((n_tiles - n_used) * MOE_TILE).reshape(1)])
    return (tile_expert, tile_valid.astype(I32), n_used.astype(I32).reshape(1), pad_base.astype(I32),
            pad_cnt.astype(I32))


def _row_tile(t, want):
    tm = min(t, want)
    assert t % tm == 0
    return tm


def kernel(x_prompt, x_sample, cache_kv_g0, cache_kv_g1, cache_kv_g2, c_prompt, c_sample, w_ada, b_ada, g_mix, w_in, ln_g, ln_b, w_s, b_s, p_a, p_b, w_o, g_ffn, w_router, b_router, w_gate, b_gate, w_up, b_up, w_down, b_down, g_final):
    depth = w_ada.shape[0]
    assert depth == 1, "single-layer trunk"
    bsz, seq, d = x_prompt.shape
    n_s, dec_seq, _ = x_sample.shape
    assert dec_seq == 1, "one new position per sample"
    caches = (cache_kv_g0, cache_kv_g1, cache_kv_g2)
    l = 0
    cols = w_in.shape[2]
    t_p = bsz * seq
    aw = A_GROUPS * CHUNK

    n_c = bsz + n_s
    n_c_pad = -(-n_c // SUBLANES) * SUBLANES
    c_all = jnp.concatenate([c_prompt, c_sample, jnp.zeros((n_c_pad - n_c, d), F32)], axis=0)
    mod = _ada(c_all, w_ada[l], b_ada[l])
    mod_p = mod[:bsz].reshape(bsz, 1, N_ADA, d)
    mod_s = mod[bsz:n_c].reshape(1, n_s, N_ADA, d)
    sh1_p, sc1_p, gt1_p, sh2_p, sc2_p, gt2_p = (mod_p[:, :, k] for k in range(N_ADA))
    sh1_s, sc1_s, gt1_s, sh2_s, sc2_s, gt2_s = (mod_s[:, :, k] for k in range(N_ADA))

    xp = x_prompt.reshape(t_p, d)
    xs = x_sample.reshape(n_s, d)

    tm_big = _row_tile(seq, 1024)
    tm_mid = _row_tile(seq, 512)
    z_p = _in_proj(xp, sc1_p, sh1_p, g_mix[l], w_in[l], rows_per_mod=seq, tm=tm_big, hi=False)
    b_p = _dil_attn(z_p, bsz, seq)
    a_p = _chunk_gate(z_p, ln_g[l], ln_b[l], w_s[l], b_s[l], tm=tm_mid)
    merged_p = _merge(a_p, b_p, z_p, p_a[l], p_b[l], tm=tm_big, hi=False)
    x1_p = _out_proj(merged_p, w_o[l], xp, gt1_p, rows_per_mod=seq, tm=tm_big, hi=False)
    hp_p, e_p, w_p = _ffn_norm(x1_p, sc2_p, sh2_p, g_ffn[l], w_router[l], b_router[l], rows_per_mod=seq, tm=tm_mid)

    z_s = _in_proj(xs, sc1_s, sh1_s, g_mix[l], w_in[l], rows_per_mod=n_s, tm=n_s, hi=True)
    b_s_out = _sample_attn(z_s, tuple(c[l] for c in caches))
    a_s, vn_s = _sample_gate(z_s, ln_g[l], ln_b[l], w_s[l], b_s[l])
    merged_s = _merge(a_s, b_s_out, z_s, p_a[l], p_b[l], tm=n_s, hi=True)
    x1_s = _out_proj(merged_s, w_o[l], xs, gt1_s, rows_per_mod=n_s, tm=n_s, hi=True)
    hp_s, e_s, w_s_top = _ffn_norm(x1_s, sc2_s, sh2_s, g_ffn[l], w_router[l], b_router[l], rows_per_mod=n_s, tm=n_s)

    assert t_p % ROUTE_TILE == 0
    t_all = t_p + n_s
    t_pad = -(-t_all // ROUTE_TILE) * ROUTE_TILE
    e_all = jnp.concatenate([e_p, e_s, jnp.full((t_pad - t_all, LANES), -1, I32)], axis=0)
    pos, cnt = _route(e_all)
    counts = cnt[0, :N_EXPERTS].astype(I32)
    n_tiles = (t_all * TOP_K + N_EXPERTS * (MOE_TILE - 1)) // MOE_TILE
    tile_expert, tile_valid, n_used, pad_base, pad_cnt = _tile_table(counts, n_tiles)
    pos_p = pos[:t_p, :TOP_K]
    pos_s = pos[t_p:t_all, :TOP_K]
    x_sorted = _row_scatter(hp_p.reshape(t_p, SUBLANES, LANES), hp_s.reshape(n_s, SUBLANES, LANES),
                            pos_p.reshape(-1), pos_s.reshape(-1), pad_base, pad_cnt, n_tiles * MOE_TILE)
    y_rows = _moe_experts(x_sorted, tile_expert, tile_valid, n_used, w_gate[l], b_gate[l], w_up[l], b_up[l],
                          w_down[l], b_down[l])
    y_p = _moe_combine(y_rows, pos_p, w_p, x1_p, gt2_p, g_final, rows_per_mod=seq, tok=_row_tile(seq, 128))
    y_s = _moe_combine(y_rows, pos_s, w_s_top, x1_s, gt2_s, g_final, rows_per_mod=n_s, tok=n_s)

    k0 = 2 * aw + N_GROUPS * GROUP_COLS
    v0 = k0 + N_GROUPS * GROUP_COLS
    z_p3 = z_p.reshape(bsz, seq, cols)
    kv_prompt, kv_sample = [], []
    for g, (win, dil) in enumerate(DIL_GROUPS):
        keep = min(win, seq)
        kc = slice(k0 + g * GROUP_COLS, k0 + (g + 1) * GROUP_COLS)
        vc = slice(v0 + g * GROUP_COLS, v0 + (g + 1) * GROUP_COLS)
        kv = jnp.stack([z_p3[:, seq - keep:, kc], z_p3[:, seq - keep:, vc]], axis=2)
        kv_prompt.append(kv.reshape(1, bsz, keep, 2, HEADS_PER_GROUP, HEAD_DIM))
        kvs = jnp.stack([z_s[:, kc], z_s[:, vc]], axis=1)
        kv_sample.append(kvs.reshape(1, n_s, 1, 2, HEADS_PER_GROUP, HEAD_DIM))
    return (y_p.reshape(bsz, seq, d), y_s.reshape(n_s, 1, d),
            kv_prompt[0], kv_prompt[1], kv_prompt[2],
            kv_sample[0], kv_sample[1], kv_sample[2],
            vn_s.reshape(1, n_s, 1, aw))
```

```python
import functools

import jax
import jax.numpy as jnp
from jax import lax
from jax.experimental import pallas as pl
from jax.experimental.pallas import tpu as pltpu

F32 = jnp.float32
BF16 = jnp.bfloat16
U32 = jnp.uint32
I32 = jnp.int32
HIGHEST = lax.Precision.HIGHEST

EPS = 1e-6
A_GROUPS = 8
CHUNK = 128
HEAD_DIM = 128
HEADS_PER_GROUP = 4
DIL_GROUPS = ((128, 1), (512, 4), (2048, 16))
N_GROUPS = len(DIL_GROUPS)
N_HEADS = HEADS_PER_GROUP * N_GROUPS
ALIBI_SLOPES = tuple(2.0 ** (-8.0 * (h + 1) / N_HEADS) for h in range(N_HEADS))
GROUP_COLS = HEADS_PER_GROUP * HEAD_DIM
N_EXPERTS = 32
TOP_K = 4
SWIGLU_LIMIT = 7.0
SWIGLU_ALPHA = 1.702
N_ADA = 6
NEG = -1e30

LANES = 128
SUBLANES = 8
COL_TILE = 512
ATTN_BLOCK = 2048
MOE_TILE = 512
MOE_HALF = MOE_TILE // 2
ROUTE_TILE = 512
VMEM_LIMIT = 56 * 1024 * 1024


def _params(sem, vmem=VMEM_LIMIT):
    return pltpu.CompilerParams(dimension_semantics=sem, vmem_limit_bytes=vmem)


def _dot(a, b, hi):
    if hi:
        return jnp.dot(a, b, precision=HIGHEST, preferred_element_type=F32)
    return jnp.dot(a.astype(BF16), b.astype(BF16), preferred_element_type=F32)


def _dot_t(a, b, hi):
    dn = (((1,), (1,)), ((), ()))
    if hi:
        return lax.dot_general(a, b, dn, precision=HIGHEST, preferred_element_type=F32)
    return lax.dot_general(a.astype(BF16), b.astype(BF16), dn, preferred_element_type=F32)


def _mod_rms(x, g, sc, sh):
    y = x * lax.rsqrt(jnp.mean(x * x, axis=-1, keepdims=True) + EPS)
    return y * g * (1.0 + sc) + sh


def _ada_kernel(c_ref, w_ref, b_ref, o_ref):
    c = c_ref[...]
    o_ref[...] = _dot(c * jax.nn.sigmoid(c), w_ref[...], True) + b_ref[...]


def _ada(c_all, w_ada, b_ada):
    n, d = c_all.shape
    cols = w_ada.shape[1]
    tn = 1024
    return pl.pallas_call(
        _ada_kernel,
        grid=(cols // tn,),
        in_specs=[pl.BlockSpec((n, d), lambda j: (0, 0)),
                  pl.BlockSpec((d, tn), lambda j: (0, j)),
                  pl.BlockSpec((1, tn), lambda j: (0, j))],
        out_specs=pl.BlockSpec((n, tn), lambda j: (0, j)),
        out_shape=jax.ShapeDtypeStruct((n, cols), F32),
        compiler_params=_params(("arbitrary",)),
        name="ada",
    )(c_all, w_ada, b_ada.reshape(1, cols))


def _in_proj_kernel(x_ref, sc_ref, sh_ref, g_ref, w_ref, o_ref, h_ref, *, hi, act):
    @pl.when(pl.program_id(1) == 0)
    def _():
        h_ref[...] = _mod_rms(x_ref[...], g_ref[...], sc_ref[0], sh_ref[0]).astype(h_ref.dtype)

    z = _dot(h_ref[...], w_ref[...], hi)
    o_ref[...] = z if act is None else act(z)


def _in_proj(x, sc, sh, g, w_in, col0, n_cols, act, *, rows_per_mod, tm, hi):
    t, d = x.shape
    assert col0 % COL_TILE == 0 and n_cols % COL_TILE == 0
    blk0 = col0 // COL_TILE
    mod_rows = sc.shape[1]
    return pl.pallas_call(
        functools.partial(_in_proj_kernel, hi=hi, act=act),
        grid=(t // tm, n_cols // COL_TILE),
        in_specs=[pl.BlockSpec((tm, d), lambda i, j: (i, 0)),
                  pl.BlockSpec((1, mod_rows, d), lambda i, j: (i * tm // rows_per_mod, 0, 0)),
                  pl.BlockSpec((1, mod_rows, d), lambda i, j: (i * tm // rows_per_mod, 0, 0)),
                  pl.BlockSpec((1, d), lambda i, j: (0, 0)),
                  pl.BlockSpec((d, COL_TILE), lambda i, j: (0, blk0 + j))],
        out_specs=pl.BlockSpec((tm, COL_TILE), lambda i, j: (i, j)),
        out_shape=jax.ShapeDtypeStruct((t, n_cols), F32),
        scratch_shapes=[pltpu.VMEM((tm, d), F32 if hi else BF16)],
        compiler_params=_params(("arbitrary", "arbitrary")),
        name="in_proj",
    )(x, sc, sh, g.reshape(1, d), w_in)


def _in_proj_all(x, sc, sh, g, w_in, **kw):
    d = x.shape[1]
    aw2 = 2 * A_GROUPS * CHUNK
    qkv = 3 * N_HEADS * HEAD_DIM
    z_a = _in_proj(x, sc, sh, g, w_in, 0, aw2, jax.nn.gelu, **kw)
    z_qkv = _in_proj(x, sc, sh, g, w_in, aw2, qkv, None, **kw)
    z_gate = _in_proj(x, sc, sh, g, w_in, aw2 + qkv, 2 * d, jax.nn.sigmoid, **kw)
    return z_a, z_qkv, z_gate


def _attend(q, kc, kp, vc, vp, slope, dil, prev_bias):
    n = q.shape[0]
    row = lax.broadcasted_iota(I32, (n, CHUNK, CHUNK), 1)
    col = lax.broadcasted_iota(I32, (n, CHUNK, CHUNK), 2)
    dist_c = ((row - col) * dil).astype(F32)
    dist_p = ((row + CHUNK - col) * dil).astype(F32)
    scale = HEAD_DIM ** -0.5
    qk = (((2,), (2,)), ((0,), (0,)))
    pv = (((2,), (1,)), ((0,), (0,)))
    s_c = lax.dot_general(q, kc, qk, preferred_element_type=F32) * scale - slope * dist_c
    s_p = lax.dot_general(q, kp, qk, preferred_element_type=F32) * scale - slope * dist_p + prev_bias
    s_c = jnp.where(col <= row, s_c, NEG)
    s_p = jnp.where(col >= row, s_p, NEG)
    m = jnp.maximum(jnp.max(s_c, axis=2, keepdims=True), jnp.max(s_p, axis=2, keepdims=True))
    p_c = jnp.exp(s_c - m)
    p_p = jnp.exp(s_p - m)
    l = jnp.sum(p_c, axis=2, keepdims=True) + jnp.sum(p_p, axis=2, keepdims=True)
    o = (lax.dot_general((p_c / l).astype(BF16), vc, pv, preferred_element_type=F32)
         + lax.dot_general((p_p / l).astype(BF16), vp, pv, preferred_element_type=F32))
    return o, m + jnp.log(l)


def _dil_attn_kernel(slope_ref, *refs):
    ins = refs[:5 * N_GROUPS]
    o_ref = refs[5 * N_GROUPS]
    acc_refs = refs[5 * N_GROUPS + 1:5 * N_GROUPS + 1 + N_GROUPS]
    lse_refs = refs[5 * N_GROUPS + 1 + N_GROUPS:]
    head = pl.program_id(1)
    i = pl.program_id(2)
    no_prev = jnp.where(i > 0, 0.0, NEG)

    for g, (win, dil) in enumerate(DIL_GROUPS):
        q_ref, k_ref, v_ref, kp_ref, vp_ref = ins[5 * g:5 * g + 5]
        acc_ref, lse_ref = acc_refs[g], lse_refs[g]
        slope = slope_ref[g * HEADS_PER_GROUP + head]
        span = CHUNK * dil
        n_blocks = ATTN_BLOCK // CHUNK

        def rows(start, dil=dil):
            return pl.ds(start, CHUNK) if dil == 1 else pl.ds(start, CHUNK, stride=dil)

        cur = [rows((b // dil) * span + b % dil) for b in range(n_blocks)]
        prev = [rows((b // dil - 1) * span + b % dil) for b in range(n_blocks)]

        def gather(ref, first_ref=None):
            tiles = []
            for b in range(n_blocks):
                if first_ref is None:
                    tiles.append(ref[cur[b], :])
                elif b < dil:
                    tiles.append(first_ref[rows(b), :])
                else:
                    tiles.append(ref[prev[b], :])
            return jnp.stack(tiles).astype(BF16)

        first_blocks = lax.broadcasted_iota(I32, (n_blocks, 1, 1), 0) < dil
        o, lse = _attend(gather(q_ref), gather(k_ref), gather(k_ref, kp_ref), gather(v_ref), gather(v_ref, vp_ref),
                         slope, dil, jnp.where(first_blocks, no_prev, 0.0))
        for b in range(n_blocks):
            acc_ref[cur[b], :] = o[b]
            lse_ref[cur[b], :] = jnp.broadcast_to(lse[b], (CHUNK, HEAD_DIM))

    ls = [r[...] for r in lse_refs]
    m = jnp.maximum(jnp.maximum(ls[0], ls[1]), ls[2])
    e = [jnp.exp(l - m) for l in ls]
    den = e[0] + e[1] + e[2]
    o_ref[...] = (acc_refs[0][...] * (e[0] / den) + acc_refs[1][...] * (e[1] / den)
                  + acc_refs[2][...] * (e[2] / den))


def _dil_attn(z, bsz, seq):
    assert seq % ATTN_BLOCK == 0
    nblk = seq // ATTN_BLOCK
    a_blocks = 0
    in_specs = [pl.BlockSpec(memory_space=pltpu.SMEM)]
    for g, (win, dil) in enumerate(DIL_GROUPS):
        span = CHUNK * dil
        per_blk = ATTN_BLOCK // span
        for which in range(3):
            cb = a_blocks + (which * N_GROUPS + g) * HEADS_PER_GROUP
            in_specs.append(pl.BlockSpec((ATTN_BLOCK, HEAD_DIM), lambda b, h, i, cb=cb: (b * nblk + i, cb + h)))
        for which in (1, 2):
            cb = a_blocks + (which * N_GROUPS + g) * HEADS_PER_GROUP
            in_specs.append(pl.BlockSpec(
                (span, HEAD_DIM),
                lambda b, h, i, cb=cb, per_blk=per_blk: (jnp.maximum((b * nblk + i) * per_blk - 1, 0), cb + h)))
    slopes = jnp.asarray(ALIBI_SLOPES, F32)
    blk = pltpu.VMEM((ATTN_BLOCK, HEAD_DIM), F32)
    return pl.pallas_call(
        _dil_attn_kernel,
        grid=(bsz, HEADS_PER_GROUP, nblk),
        in_specs=in_specs,
        out_specs=pl.BlockSpec((ATTN_BLOCK, HEAD_DIM), lambda b, h, i: (b * nblk + i, h)),
        out_shape=jax.ShapeDtypeStruct((bsz * seq, GROUP_COLS), F32),
        scratch_shapes=[blk] * (2 * N_GROUPS),
        compiler_params=_params(("arbitrary", "arbitrary", "arbitrary")),
        name="dil_attn",
    )(slopes, *([z] * (5 * N_GROUPS)))


def _sample_attn_kernel(pen_ref, q_ref, k_ref, v_ref, c0_ref, c1_ref, c2_ref, o_ref):
    caches = (c0_ref, c1_ref, c2_ref)
    scale = HEAD_DIM ** -0.5
    heads = [(g, h) for g in range(N_GROUPS) for h in range(HEADS_PER_GROUP)]

    def head_rows(ref):
        return jnp.stack([ref[0, :, i * HEAD_DIM:(i + 1) * HEAD_DIM] for i in range(N_HEADS)])

    q = head_rows(q_ref)
    k_new = head_rows(k_ref)
    v_new = head_rows(v_ref)
    k_buf = jnp.stack([caches[g][:, 0, h, :] for g, h in heads])
    v_buf = jnp.stack([caches[g][:, 1, h, :] for g, h in heads])
    q8 = jnp.broadcast_to(q, (N_HEADS, SUBLANES, HEAD_DIM))
    qk = (((2,), (2,)), ((0,), (0,)))
    pv = (((2,), (1,)), ((0,), (0,)))
    s_buf = lax.dot_general(q8, k_buf, qk, precision=HIGHEST, preferred_element_type=F32)[:, 0:1, :]
    s_buf = s_buf * scale - pen_ref[...]
    s_new = jnp.sum(q * k_new, axis=2, keepdims=True) * scale
    m = jnp.maximum(jnp.max(s_buf, axis=2, keepdims=True), s_new)
    p_buf = jnp.exp(s_buf - m)
    p_new = jnp.exp(s_new - m)
    l = jnp.sum(p_buf, axis=2, keepdims=True) + p_new
    pb8 = jnp.broadcast_to(p_buf / l, (N_HEADS, SUBLANES, CHUNK))
    o = lax.dot_general(pb8, v_buf, pv, precision=HIGHEST, preferred_element_type=F32)[:, 0:1, :]
    o = o + (p_new / l) * v_new
    lse = m + jnp.log(l)
    n = HEADS_PER_GROUP
    ls = [lse[g * n:(g + 1) * n] for g in range(N_GROUPS)]
    mm = jnp.maximum(jnp.maximum(ls[0], ls[1]), ls[2])
    e = [jnp.exp(x - mm) for x in ls]
    den = e[0] + e[1] + e[2]
    out = o[0:n] * (e[0] / den) + o[n:2 * n] * (e[1] / den) + o[2 * n:3 * n] * (e[2] / den)
    for h in range(HEADS_PER_GROUP):
        o_ref[0, :, h * HEAD_DIM:(h + 1) * HEAD_DIM] = out[h]


def _sample_attn(zs, caches):
    n, cols = zs.shape
    z3 = zs.reshape(n, 1, cols)
    a_blocks = 0
    qkv_cols = N_GROUPS * GROUP_COLS
    c_views = []
    c_specs = []
    for g, (win, dil) in enumerate(DIL_GROUPS):
        c = caches[g]
        assert c.shape[1] == win, "cache must hold exactly one window"
        c_views.append(c.reshape(n, win // dil, dil, 2, HEADS_PER_GROUP, HEAD_DIM))
        c_specs.append(pl.BlockSpec((None, win // dil, None, 2, HEADS_PER_GROUP, HEAD_DIM),
                                    lambda b: (b, 0, 0, 0, 0, 0)))
    q0 = a_blocks * GROUP_COLS
    steps = jnp.arange(CHUNK, 0, -1, dtype=F32)
    pen = jnp.stack([ALIBI_SLOPES[g * HEADS_PER_GROUP + h] * dil * steps
                     for g, (win, dil) in enumerate(DIL_GROUPS) for h in range(HEADS_PER_GROUP)])
    assert all(win // dil == CHUNK for win, dil in DIL_GROUPS)
    row_spec = pl.BlockSpec((1, 1, qkv_cols), lambda b: (b, 0, 0))
    out = pl.pallas_call(
        _sample_attn_kernel,
        grid=(n,),
        in_specs=[pl.BlockSpec((N_HEADS, 1, CHUNK), lambda b: (0, 0, 0)), row_spec, row_spec, row_spec] + c_specs,
        out_specs=pl.BlockSpec((1, 1, GROUP_COLS), lambda b: (b, 0, 0)),
        out_shape=jax.ShapeDtypeStruct((n, 1, GROUP_COLS), F32),
        compiler_params=_params(("arbitrary",)),
        name="sample_attn",
    )(pen.reshape(N_HEADS, 1, CHUNK), z3[:, :, q0:q0 + qkv_cols], z3[:, :, q0 + qkv_cols:q0 + 2 * qkv_cols],
      z3[:, :, q0 + 2 * qkv_cols:q0 + 3 * qkv_cols], *c_views)
    return out.reshape(n, GROUP_COLS)


def _layer_norm(v, g, b):
    mu = jnp.mean(v, axis=-1, keepdims=True)
    var = jnp.mean(jnp.square(v - mu), axis=-1, keepdims=True)
    return (v - mu) * lax.rsqrt(var + EPS) * g + b


def _chunk_gate_kernel(u_ref, v_ref, lg_ref, lb_ref, ws_ref, bs_ref, o_ref, *, n_chunks):
    va = _layer_norm(v_ref[...], lg_ref[...], lb_ref[...]).astype(BF16)
    row = lax.broadcasted_iota(I32, (CHUNK, CHUNK), 0)
    col = lax.broadcasted_iota(I32, (CHUNK, CHUNK), 1)
    for g in range(A_GROUPS):
        w = jnp.where(col <= row, ws_ref[g], 0.0).astype(BF16)
        gs = slice(g * CHUNK, (g + 1) * CHUNK)
        for c in range(n_chunks):
            rs = slice(c * CHUNK, (c + 1) * CHUNK)
            mixed = jnp.dot(w, va[rs, gs], preferred_element_type=F32) + bs_ref[g]
            o_ref[rs, gs] = (u_ref[rs, gs] * mixed).astype(o_ref.dtype)


def _chunk_gate(z, ln_g, ln_b, w_s, b_s, *, tm):
    t = z.shape[0]
    aw = A_GROUPS * CHUNK
    bs_full = jnp.broadcast_to(b_s[:, :, None], (A_GROUPS, CHUNK, CHUNK))
    kern = functools.partial(_chunk_gate_kernel, n_chunks=tm // CHUNK)
    return pl.pallas_call(
        kern,
        grid=(t // tm,),
        in_specs=[pl.BlockSpec((tm, aw), lambda i: (i, 0)),
                  pl.BlockSpec((tm, aw), lambda i: (i, 1)),
                  pl.BlockSpec((1, aw), lambda i: (0, 0)),
                  pl.BlockSpec((1, aw), lambda i: (0, 0)),
                  pl.BlockSpec((A_GROUPS, CHUNK, CHUNK), lambda i: (0, 0, 0)),
                  pl.BlockSpec((A_GROUPS, CHUNK, CHUNK), lambda i: (0, 0, 0))],
        out_specs=pl.BlockSpec((tm, aw), lambda i: (i, 0)),
        out_shape=jax.ShapeDtypeStruct((t, aw), BF16),
        compiler_params=_params(("arbitrary",)),
        name="chunk_gate",
    )(z, z, ln_g.reshape(1, aw), ln_b.reshape(1, aw), w_s, bs_full)


def _sample_gate_kernel(u_ref, v_ref, lg_ref, lb_ref, w0_ref, b0_ref, a_ref, vn_ref):
    va = _layer_norm(v_ref[...], lg_ref[...], lb_ref[...])
    vn_ref[...] = va
    a_ref[...] = u_ref[...] * (w0_ref[...] * va + b0_ref[...])


def _sample_gate(zs, ln_g, ln_b, w_s, b_s):
    n = zs.shape[0]
    aw = A_GROUPS * CHUNK
    w0 = jnp.repeat(w_s[:, 0, 0], CHUNK).reshape(1, aw)
    b0 = jnp.repeat(b_s[:, 0], CHUNK).reshape(1, aw)
    vec = pl.BlockSpec((1, aw), lambda i: (0, 0))
    return pl.pallas_call(
        _sample_gate_kernel,
        grid=(1,),
        in_specs=[pl.BlockSpec((n, aw), lambda i: (0, 0)), pl.BlockSpec((n, aw), lambda i: (0, 1)),
                  vec, vec, vec, vec],
        out_specs=[pl.BlockSpec((n, aw), lambda i: (0, 0)), pl.BlockSpec((n, aw), lambda i: (0, 0))],
        out_shape=[jax.ShapeDtypeStruct((n, aw), F32), jax.ShapeDtypeStruct((n, aw), F32)],
        compiler_params=_params(("arbitrary",)),
        name="sample_gate",
    )(zs, zs, ln_g.reshape(1, aw), ln_b.reshape(1, aw), w0, b0)


def _merge_kernel(a_ref, b_ref, ga_ref, gb_ref, pa_ref, pb_ref, out_ref, *, hi):
    ya = _dot(a_ref[...], pa_ref[...], hi)
    yb = _dot(b_ref[...], pb_ref[...], hi)
    out_ref[...] = (ga_ref[...] * ya + gb_ref[...] * yb).astype(out_ref.dtype)


def _merge(a_out, b_out, z, p_a, p_b, *, tm, hi):
    t, aw = a_out.shape
    d = p_a.shape[1]
    tn = COL_TILE
    ga_blk0 = 0
    gb_blk0 = d // tn
    return pl.pallas_call(
        functools.partial(_merge_kernel, hi=hi),
        grid=(t // tm, d // tn),
        in_specs=[pl.BlockSpec((tm, aw), lambda i, j: (i, 0)),
                  pl.BlockSpec((tm, GROUP_COLS), lambda i, j: (i, 0)),
                  pl.BlockSpec((tm, tn), lambda i, j: (i, ga_blk0 + j)),
                  pl.BlockSpec((tm, tn), lambda i, j: (i, gb_blk0 + j)),
                  pl.BlockSpec((aw, tn), lambda i, j: (0, j)),
                  pl.BlockSpec((GROUP_COLS, tn), lambda i, j: (0, j))],
        out_specs=pl.BlockSpec((tm, tn), lambda i, j: (i, j)),
        out_shape=jax.ShapeDtypeStruct((t, d), F32 if hi else BF16),
        compiler_params=_params(("arbitrary", "arbitrary")),
        name="merge",
    )(a_out, b_out, z, z, p_a, p_b)


def _out_proj_kernel(m_ref, w_ref, x_ref, gt_ref, o_ref, *, hi):
    o_ref[...] = x_ref[...] + gt_ref[0] * _dot(m_ref[...], w_ref[...], hi)


def _out_proj(merged, w_o, x, gt, *, rows_per_mod, tm, hi):
    t, d = x.shape
    tn = COL_TILE
    mod_rows = gt.shape[1]
    return pl.pallas_call(
        functools.partial(_out_proj_kernel, hi=hi),
        grid=(t // tm, d // tn),
        in_specs=[pl.BlockSpec((tm, d), lambda i, j: (i, 0)),
                  pl.BlockSpec((d, tn), lambda i, j: (0, j)),
                  pl.BlockSpec((tm, tn), lambda i, j: (i, j)),
                  pl.BlockSpec((1, mod_rows, tn), lambda i, j: (i * tm // rows_per_mod, 0, j))],
        out_specs=pl.BlockSpec((tm, tn), lambda i, j: (i, j)),
        out_shape=jax.ShapeDtypeStruct((t, d), F32),
        compiler_params=_params(("arbitrary", "arbitrary")),
        name="out_proj",
    )(merged, w_o, x, gt)


def _pack_pair(lo, hi, pair_ref):
    n = lo.shape[0]
    pair_ref[pl.ds(0, n, stride=2), :] = lo
    pair_ref[pl.ds(1, n, stride=2), :] = hi
    return pltpu.bitcast(pair_ref[0:2 * n, :].astype(BF16), U32)


def _unpack_pair(w, pair_ref):
    n = w.shape[0]
    pair_ref[0:2 * n, :] = pltpu.bitcast(w, BF16).astype(F32)
    return pair_ref[pl.ds(0, n, stride=2), :].astype(BF16), pair_ref[pl.ds(1, n, stride=2), :].astype(BF16)


def _ffn_norm_kernel(x_ref, sc_ref, sh_ref, g_ref, wr_ref, br_ref, hp_ref, e_ref, w_ref, pair_ref):
    h = _mod_rms(x_ref[...], g_ref[...], sc_ref[0], sh_ref[0])
    tm, d = h.shape
    for s in range(d // (2 * LANES)):
        lo = h[:, (2 * s) * LANES:(2 * s + 1) * LANES]
        hi = h[:, (2 * s + 1) * LANES:(2 * s + 2) * LANES]
        hp_ref[pl.ds(s, tm, stride=SUBLANES), :] = _pack_pair(lo, hi, pair_ref)
    logits = _dot(h, wr_ref[...], True) + br_ref[...]
    lane = lax.broadcasted_iota(I32, logits.shape, 1)
    lane_f = lane.astype(F32)
    vals, idxs = [], []
    for _ in range(TOP_K):
        m = jnp.max(logits, axis=1, keepdims=True)
        idx = jnp.min(jnp.where(logits == m, lane_f, float(logits.shape[1])), axis=1, keepdims=True)
        vals.append(m)
        idxs.append(idx)
        logits = jnp.where(lane_f == idx, 2.0 * NEG, logits)
    es = [jnp.exp(v - vals[0]) for v in vals]
    den = es[0] + es[1] + es[2] + es[3]
    e_out = jnp.full(lane.shape, -1.0, F32)
    w_out = jnp.zeros(lane.shape, F32)
    for k in range(TOP_K):
        e_out = jnp.where(lane == k, idxs[k], e_out)
        w_out = jnp.where(lane == k, es[k] / den, w_out)
    e_ref[...] = e_out.astype(I32)
    w_ref[...] = w_out


def _ffn_norm(x1, sc, sh, g, w_router, b_router, *, rows_per_mod, tm):
    t, d = x1.shape
    assert d == 2 * LANES * SUBLANES, "one packed row must be exactly one (8, 128) tile"
    ne = w_router.shape[1]
    wr = jnp.zeros((d, LANES), F32).at[:, :ne].set(w_router)
    br = jnp.full((1, LANES), NEG, F32).at[0, :ne].set(b_router)
    mod_rows = sc.shape[1]
    mod_spec = pl.BlockSpec((1, mod_rows, d), lambda i: (i * tm // rows_per_mod, 0, 0))
    return pl.pallas_call(
        _ffn_norm_kernel,
        grid=(t // tm,),
        in_specs=[pl.BlockSpec((tm, d), lambda i: (i, 0)), mod_spec, mod_spec,
                  pl.BlockSpec((1, d), lambda i: (0, 0)),
                  pl.BlockSpec((d, LANES), lambda i: (0, 0)),
                  pl.BlockSpec((1, LANES), lambda i: (0, 0))],
        out_specs=[pl.BlockSpec((tm * SUBLANES, LANES), lambda i: (i, 0)),
                   pl.BlockSpec((tm, LANES), lambda i: (i, 0)),
                   pl.BlockSpec((tm, LANES), lambda i: (i, 0))],
        out_shape=[jax.ShapeDtypeStruct((t * SUBLANES, LANES), U32),
                   jax.ShapeDtypeStruct((t, LANES), I32),
                   jax.ShapeDtypeStruct((t, LANES), F32)],
        scratch_shapes=[pltpu.VMEM((2 * tm, LANES), F32)],
        compiler_params=_params(("arbitrary",)),
        name="ffn_norm",
    )(x1, sc, sh, g.reshape(1, d), wr, br)


def _route_kernel(e_ref, pos_ref, cnt_ref, tri_ref, run_ref):
    phase = pl.program_id(0)
    i = pl.program_id(1)
    e = e_ref[...]
    tm = e.shape[0]
    lane = lax.broadcasted_iota(I32, (tm, LANES), 1)
    hits = [lane == e[:, k:k + 1] for k in range(TOP_K)]
    chosen = jnp.zeros((tm, LANES), F32)
    for k in range(TOP_K):
        chosen = jnp.where(hits[k], 1.0, chosen)
    col_count = jnp.sum(chosen, axis=0, keepdims=True)

    @pl.when((phase == 0) & (i == 0))
    def _():
        cnt_ref[...] = jnp.zeros(cnt_ref.shape, F32)
        r = lax.broadcasted_iota(I32, (tm, tm), 0)
        c = lax.broadcasted_iota(I32, (tm, tm), 1)
        tri_ref[...] = jnp.where(c < r, 1.0, 0.0).astype(BF16)

    @pl.when(phase == 0)
    def _():
        cnt_ref[...] = cnt_ref[...] + col_count

    @pl.when((phase == 1) & (i == 0))
    def _():
        tiles = jnp.floor((cnt_ref[...] + (MOE_TILE - 1)) * (1.0 / MOE_TILE))
        r = lax.broadcasted_iota(I32, (LANES, LANES), 0)
        c = lax.broadcasted_iota(I32, (LANES, LANES), 1)
        below = jnp.where(r < c, 1.0, 0.0)
        tiles8 = jnp.broadcast_to(tiles, (SUBLANES, LANES))
        run_ref[...] = _dot(tiles8, below, True)[0:1] * float(MOE_TILE)

    @pl.when(phase == 1)
    def _():
        before = jnp.dot(tri_ref[...], chosen.astype(BF16), preferred_element_type=F32)
        dest = run_ref[...] + before
        out = jnp.zeros((tm, LANES), F32)
        for k in range(TOP_K):
            p = jnp.sum(jnp.where(hits[k], dest, 0.0), axis=1, keepdims=True)
            out = jnp.where(lane == k, p, out)
        pos_ref[...] = out.astype(I32)
        run_ref[...] = run_ref[...] + col_count


def _route(e_all):
    t = e_all.shape[0]
    steps = t // ROUTE_TILE
    return pl.pallas_call(
        _route_kernel,
        grid=(2, steps),
        in_specs=[pl.BlockSpec((ROUTE_TILE, LANES), lambda p, i: (i, 0))],
        out_specs=[pl.BlockSpec((ROUTE_TILE, LANES), lambda p, i: (i * p, 0)),
                   pl.BlockSpec((1, LANES), lambda p, i: (0, 0))],
        out_shape=[jax.ShapeDtypeStruct((t, LANES), I32), jax.ShapeDtypeStruct((1, LANES), F32)],
        scratch_shapes=[pltpu.VMEM((ROUTE_TILE, ROUTE_TILE), BF16), pltpu.VMEM((1, LANES), F32)],
        compiler_params=_params(("arbitrary", "arbitrary")),
        name="route",
    )(e_all)


def _row_scatter_kernel(pos_ref, pos_s_ref, pad_base_ref, pad_cnt_ref, hp_ref, hs_ref, xs_hbm, sem, *,
                        n_prompt_steps, n_sample):
    i = pl.program_id(0)

    def rows(first, n=1):
        return pl.ds(pl.multiple_of(first * SUBLANES, SUBLANES), n * SUBLANES)

    def scatter(src_ref, idx_ref, n_tok):
        def body(tt, carry):
            for k in range(TOP_K):
                pltpu.make_async_copy(src_ref.at[rows(tt)], xs_hbm.at[rows(idx_ref[0, 0, tt * TOP_K + k])],
                                      sem).start()
            return carry
        lax.fori_loop(0, n_tok, body, 0, unroll=2)
        for _ in range(TOP_K):
            pltpu.make_async_copy(src_ref.at[rows(0, n_tok)], xs_hbm.at[rows(0, n_tok)], sem).wait()

    @pl.when(i < n_prompt_steps)
    def _():
        scatter(hp_ref, pos_ref, ROUTE_TILE)

    @pl.when(i == n_prompt_steps)
    def _():
        scatter(hs_ref, pos_s_ref, n_sample)

        def fill(dst_row, size):
            return pltpu.make_async_copy(hp_ref.at[rows(0, size)], xs_hbm.at[rows(dst_row, size)], sem)

        def per_segment(wait):
            def body(e, carry):
                cnt = pad_cnt_ref[e]
                base = pad_base_ref[e]

                def whole(r, c):
                    cp = fill(base + (cnt % MOE_TILE) + r * MOE_TILE, MOE_TILE)
                    cp.wait() if wait else cp.start()
                    return c
                lax.fori_loop(0, cnt // MOE_TILE, whole, 0)
                size = MOE_TILE // 2
                while size >= 1:
                    @pl.when((cnt & size) != 0)
                    def _(size=size):
                        cp = fill(base + (cnt & (size - 1)), size)
                        cp.wait() if wait else cp.start()
                    size //= 2
                return carry
            return body
        lax.fori_loop(0, pad_cnt_ref.shape[0], per_segment(False), 0)
        lax.fori_loop(0, pad_cnt_ref.shape[0], per_segment(True), 0)


def _row_scatter(hp_p, hp_s, pos_p, pos_s, pad_base, pad_cnt, n_rows):
    t_p = hp_p.shape[0] // SUBLANES
    n_s = hp_s.shape[0] // SUBLANES
    assert ROUTE_TILE >= MOE_TILE, "padding blocks are copied from one token tile"
    steps = t_p // ROUTE_TILE
    tile_rows = ROUTE_TILE * TOP_K
    return pl.pallas_call(
        functools.partial(_row_scatter_kernel, n_prompt_steps=steps, n_sample=n_s),
        grid=(steps + 1,),
        in_specs=[pl.BlockSpec((1, 1, tile_rows), lambda i: (jnp.minimum(i, steps - 1), 0, 0),
                               memory_space=pltpu.SMEM),
                  pl.BlockSpec((1, 1, n_s * TOP_K), lambda i: (0, 0, 0), memory_space=pltpu.SMEM),
                  pl.BlockSpec(memory_space=pltpu.SMEM),
                  pl.BlockSpec(memory_space=pltpu.SMEM),
                  pl.BlockSpec((ROUTE_TILE * SUBLANES, LANES), lambda i: (jnp.minimum(i, steps - 1), 0)),
                  pl.BlockSpec((n_s * SUBLANES, LANES), lambda i: (0, 0))],
        out_specs=pl.BlockSpec(memory_space=pl.ANY),
        out_shape=jax.ShapeDtypeStruct((n_rows * SUBLANES, LANES), U32),
        scratch_shapes=[pltpu.SemaphoreType.DMA(())],
        compiler_params=_params(("arbitrary",)),
        name="row_scatter",
    )(pos_p.reshape(steps, 1, tile_rows), pos_s.reshape(1, 1, n_s * TOP_K), pad_base, pad_cnt, hp_p, hp_s)


def _moe_up_kernel(te_ref, nv_ref, nu_ref, x_ref, wg_ref, wu_ref, bg_ref, bu_ref, h_ref, wg_s, wu_s, xb_s,
                   pair_ref):
    t = pl.program_id(1)
    e = te_ref[t]
    prev = te_ref[jnp.maximum(t - 1, 0)]

    @pl.when((t == 0) | (e != prev))
    def _():
        wg_s[...] = wg_ref[0].astype(BF16)
        wu_s[...] = wu_ref[0].astype(BF16)

    def compute(rows):
        for s in range(SUBLANES):
            lo, hi = _unpack_pair(x_ref[pl.ds(s, rows, stride=SUBLANES), :], pair_ref)
            xb_s[0:rows, (2 * s) * LANES:(2 * s + 1) * LANES] = lo
            xb_s[0:rows, (2 * s + 1) * LANES:(2 * s + 2) * LANES] = hi
        x = xb_s[0:rows, :]
        gt = jnp.dot(x, wg_s[...], preferred_element_type=F32) + bg_ref[0]
        up = jnp.dot(x, wu_s[...], preferred_element_type=F32) + bu_ref[0]
        gt = jnp.minimum(gt, SWIGLU_LIMIT)
        up = jnp.clip(up, -SWIGLU_LIMIT, SWIGLU_LIMIT)
        h_ref[0:rows, :] = ((up + 1.0) * gt * jax.nn.sigmoid(SWIGLU_ALPHA * gt)).astype(h_ref.dtype)

    active = t < nu_ref[0]
    full = nv_ref[t] > MOE_HALF

    @pl.when(active & full)
    def _():
        compute(MOE_TILE)

    @pl.when(active & jnp.logical_not(full))
    def _():
        compute(MOE_HALF)
        h_ref[MOE_HALF:, :] = jnp.zeros((MOE_TILE - MOE_HALF, h_ref.shape[1]), h_ref.dtype)

    @pl.when(jnp.logical_not(active))
    def _():
        h_ref[...] = jnp.zeros(h_ref.shape, h_ref.dtype)


def _moe_down_kernel(te_ref, nv_ref, nu_ref, h_ref, wd_ref, bd_ref, y_ref, wd_s):
    t = pl.program_id(1)
    e = te_ref[t]
    prev = te_ref[jnp.maximum(t - 1, 0)]
    n_sub = y_ref.shape[2]

    @pl.when((t == 0) | (e != prev))
    def _():
        wd_s[...] = wd_ref[0].astype(BF16)

    def store(rows, y):
        for c in range(n_sub):
            y_ref[0:rows, 0, c, :] = y[:, c * LANES:(c + 1) * LANES]

    def compute(rows):
        store(rows, jnp.dot(h_ref[0:rows, :], wd_s[...], preferred_element_type=F32) + bd_ref[0])

    active = t < nu_ref[0]
    full = nv_ref[t] > MOE_HALF

    @pl.when(active & full)
    def _():
        compute(MOE_TILE)

    @pl.when(active & jnp.logical_not(full))
    def _():
        compute(MOE_HALF)
        y_ref[MOE_HALF:] = jnp.zeros((MOE_TILE - MOE_HALF,) + y_ref.shape[1:], F32)

    @pl.when(jnp.logical_not(active))
    def _():
        y_ref[...] = jnp.zeros(y_ref.shape, F32)


def _moe_experts(x_sorted, tile_expert, tile_valid, n_used, w_gate, b_gate, w_up, b_up, w_down, b_down):
    rows = x_sorted.shape[0] // SUBLANES
    ne, d, f = w_gate.shape
    nt = rows // MOE_TILE
    tf = 1024
    tn = SUBLANES * LANES
    x2 = x_sorted

    def row_map(j, t, te, nv, nu):
        return (jnp.minimum(t, nu[0] - 1), 0)

    def w_map(j, t, te, nv, nu):
        return (te[t], 0, j)

    h = pl.pallas_call(
        _moe_up_kernel,
        grid_spec=pltpu.PrefetchScalarGridSpec(
            num_scalar_prefetch=3,
            grid=(f // tf, nt),
            in_specs=[pl.BlockSpec((MOE_TILE * SUBLANES, LANES), row_map),
                      pl.BlockSpec((1, d, tf), w_map),
                      pl.BlockSpec((1, d, tf), w_map),
                      pl.BlockSpec((1, 1, tf), w_map),
                      pl.BlockSpec((1, 1, tf), w_map)],
            out_specs=pl.BlockSpec((MOE_TILE, tf), lambda j, t, te, nv, nu: (t, j)),
            scratch_shapes=[pltpu.VMEM((d, tf), BF16), pltpu.VMEM((d, tf), BF16), pltpu.VMEM((MOE_TILE, d), BF16),
                            pltpu.VMEM((2 * MOE_TILE, LANES), F32)]),
        out_shape=jax.ShapeDtypeStruct((rows, f), BF16),
        compiler_params=_params(("arbitrary", "arbitrary")),
        name="moe_up",
    )(tile_expert, tile_valid, n_used, x2, w_gate, w_up, b_gate.reshape(ne, 1, f), b_up.reshape(ne, 1, f))

    y = pl.pallas_call(
        _moe_down_kernel,
        grid_spec=pltpu.PrefetchScalarGridSpec(
            num_scalar_prefetch=3,
            grid=(d // tn, nt),
            in_specs=[pl.BlockSpec((MOE_TILE, f), row_map),
                      pl.BlockSpec((1, f, tn), w_map),
                      pl.BlockSpec((1, 1, tn), w_map)],
            out_specs=pl.BlockSpec((MOE_TILE, 1, SUBLANES, LANES), lambda j, t, te, nv, nu: (t, j, 0, 0)),
            scratch_shapes=[pltpu.VMEM((f, tn), BF16)]),
        out_shape=jax.ShapeDtypeStruct((rows, d // tn, SUBLANES, LANES), F32),
        compiler_params=_params(("arbitrary", "arbitrary")),
        name="moe_down",
    )(tile_expert, tile_valid, n_used, h, w_down, b_down.reshape(ne, 1, d))
    return y.reshape(rows, d // LANES, LANES)


def _row_copy(src_hbm, buf, sem, slot, src_row, dst_row, n_sub):
    dst = pl.ds(pl.multiple_of(dst_row * n_sub, SUBLANES), n_sub)
    return pltpu.make_async_copy(src_hbm.at[src_row], buf.at[slot, dst], sem.at[slot])


def _issue_rows(idx_ref, src_hbm, buf, sem, slot, n_rows, n_sub):
    def body(r, carry):
        _row_copy(src_hbm, buf, sem, slot, idx_ref[0, 0, r], r, n_sub).start()
        return carry
    lax.fori_loop(0, n_rows, body, 0, unroll=8)


def _moe_combine_kernel(idx_ref, nxt_ref, y_hbm, w_ref, x_ref, gt_ref, g_ref, o_ref, buf, sem, *, tok):
    i = pl.program_id(0)
    n = pl.num_programs(0)
    n_rows = TOP_K * tok
    n_sub = y_hbm.shape[1]
    slot = lax.rem(i, 2)

    @pl.when(i == 0)
    def _():
        _issue_rows(idx_ref, y_hbm, buf, sem, 0, n_rows, n_sub)

    for s in range(2):
        @pl.when((i + 1 < n) & (slot == s))
        def _():
            _issue_rows(nxt_ref, y_hbm, buf, sem, 1 - s, n_rows, n_sub)

    pltpu.make_async_copy(buf.at[slot], buf.at[slot], sem.at[slot]).wait()

    w = w_ref[...]
    sumsq = jnp.zeros((tok, 1), F32)
    for c in range(n_sub):
        cs = slice(c * LANES, (c + 1) * LANES)
        acc = jnp.zeros((tok, LANES), F32)
        for k in range(TOP_K):
            part = buf[slot, pl.ds(k * tok * n_sub + c, tok, stride=n_sub), :]
            acc = acc + w[:, k:k + 1] * part
        x = x_ref[:, cs] + gt_ref[0, :, cs] * acc
        o_ref[:, cs] = x
        sumsq = sumsq + jnp.sum(x * x, axis=1, keepdims=True)
    o_ref[...] = o_ref[...] * lax.rsqrt(sumsq / (n_sub * LANES) + EPS) * g_ref[...]


def _moe_combine(y_rows, pos, top_w, x1, gt, g_final, *, rows_per_mod, tok):
    t, d = x1.shape
    n_sub = y_rows.shape[1]
    steps = t // tok
    n_rows = TOP_K * tok
    idx3 = pos[:, :TOP_K].reshape(steps, tok, TOP_K).transpose(0, 2, 1).reshape(steps, 1, n_rows)
    lanes = top_w.shape[1]
    mod_rows = gt.shape[1]
    idx_blk = (1, 1, n_rows)
    return pl.pallas_call(
        functools.partial(_moe_combine_kernel, tok=tok),
        grid=(steps,),
        in_specs=[pl.BlockSpec(idx_blk, lambda i: (i, 0, 0), memory_space=pltpu.SMEM),
                  pl.BlockSpec(idx_blk, lambda i: (jnp.minimum(i + 1, steps - 1), 0, 0), memory_space=pltpu.SMEM),
                  pl.BlockSpec(memory_space=pl.ANY),
                  pl.BlockSpec((tok, lanes), lambda i: (i, 0)),
                  pl.BlockSpec((tok, d), lambda i: (i, 0)),
                  pl.BlockSpec((1, mod_rows, d), lambda i: (i * tok // rows_per_mod, 0, 0)),
                  pl.BlockSpec((1, d), lambda i: (0, 0))],
        out_specs=pl.BlockSpec((tok, d), lambda i: (i, 0)),
        out_shape=jax.ShapeDtypeStruct((t, d), F32),
        scratch_shapes=[pltpu.VMEM((2, n_rows * n_sub, LANES), F32), pltpu.SemaphoreType.DMA((2,))],
        compiler_params=_params(("arbitrary",)),
        name="moe_combine",
    )(idx3, idx3, y_rows, top_w, x1, gt, g_final.reshape(1, d))


def _tile_table(counts, n_tiles):
    tiles_e = (counts + MOE_TILE - 1) // MOE_TILE
    tile_end = jnp.cumsum(tiles_e)
    tile_start = tile_end - tiles_e
    n_used = tile_end[-1]
    tile_ids = jnp.minimum(jnp.arange(n_tiles, dtype=I32), n_used - 1)
    tile_expert = jnp.minimum(jnp.sum(tile_end[None, :] <= tile_ids[:, None], axis=1), N_EXPERTS - 1).astype(I32)
    tile_valid = jnp.clip(counts[tile_expert] - (tile_ids - tile_start[tile_expert]) * MOE_TILE, 0, MOE_TILE)
    pad_base = jnp.concatenate([tile_start * MOE_TILE + counts, (n_used * MOE_TILE).reshape(1)])
    pad_cnt = jnp.concatenate([tiles_e * MOE_TILE - counts, ((n_tiles - n_used) * MOE_TILE).reshape(1)])
    return (tile_expert, tile_valid.astype(I32), n_used.astype(I32).reshape(1), pad_base.astype(I32),
            pad_cnt.astype(I32))


def _row_tile(t, want):
    tm = min(t, want)
    assert t % tm == 0
    return tm


def kernel(x_prompt, x_sample, cache_kv_g0, cache_kv_g1, cache_kv_g2, c_prompt, c_sample, w_ada, b_ada, g_mix, w_in, ln_g, ln_b, w_s, b_s, p_a, p_b, w_o, g_ffn, w_router, b_router, w_gate, b_gate, w_up, b_up, w_down, b_down, g_final):
    depth = w_ada.shape[0]
    assert depth == 1, "single-layer trunk"
    bsz, seq, d = x_prompt.shape
    n_s, dec_seq, _ = x_sample.shape
    assert dec_seq == 1, "one new position per sample"
    caches = (cache_kv_g0, cache_kv_g1, cache_kv_g2)
    l = 0
    cols = w_in.shape[2]
    t_p = bsz * seq
    aw = A_GROUPS * CHUNK

    n_c = bsz + n_s
    n_c_pad = -(-n_c // SUBLANES) * SUBLANES
    c_all = jnp.concatenate([c_prompt, c_sample, jnp.zeros((n_c_pad - n_c, d), F32)], axis=0)
    mod = _ada(c_all, w_ada[l], b_ada[l])
    mod_p = mod[:bsz].reshape(bsz, 1, N_ADA, d)
    mod_s = mod[bsz:n_c].reshape(1, n_s, N_ADA, d)
    sh1_p, sc1_p, gt1_p, sh2_p, sc2_p, gt2_p = (mod_p[:, :, k] for k in range(N_ADA))
    sh1_s, sc1_s, gt1_s, sh2_s, sc2_s, gt2_s = (mod_s[:, :, k] for k in range(N_ADA))

    xp = x_prompt.reshape(t_p, d)
    xs = x_sample.reshape(n_s, d)

    tm_big = _row_tile(seq, 1024)
    tm_mid = _row_tile(seq, 512)
    za_p, zqkv_p, zg_p = _in_proj_all(xp, sc1_p, sh1_p, g_mix[l], w_in[l], rows_per_mod=seq, tm=tm_big, hi=False)
    b_p = _dil_attn(zqkv_p, bsz, seq)
    a_p = _chunk_gate(za_p, ln_g[l], ln_b[l], w_s[l], b_s[l], tm=tm_mid)
    merged_p = _merge(a_p, b_p, zg_p, p_a[l], p_b[l], tm=tm_big, hi=False)
    x1_p = _out_proj(merged_p, w_o[l], xp, gt1_p, rows_per_mod=seq, tm=tm_big, hi=False)
    hp_p, e_p, w_p = _ffn_norm(x1_p, sc2_p, sh2_p, g_ffn[l], w_router[l], b_router[l], rows_per_mod=seq, tm=tm_mid)

    za_s, zqkv_s, zg_s = _in_proj_all(xs, sc1_s, sh1_s, g_mix[l], w_in[l], rows_per_mod=n_s, tm=n_s, hi=True)
    b_s_out = _sample_attn(zqkv_s, tuple(c[l] for c in caches))
    a_s, vn_s = _sample_gate(za_s, ln_g[l], ln_b[l], w_s[l], b_s[l])
    merged_s = _merge(a_s, b_s_out, zg_s, p_a[l], p_b[l], tm=n_s, hi=True)
    x1_s = _out_proj(merged_s, w_o[l], xs, gt1_s, rows_per_mod=n_s, tm=n_s, hi=True)
    hp_s, e_s, w_s_top = _ffn_norm(x1_s, sc2_s, sh2_s, g_ffn[l], w_router[l], b_router[l], rows_per_mod=n_s, tm=n_s)

    assert t_p % ROUTE_TILE == 0
    t_all = t_p + n_s
    t_pad = -(-t_all // ROUTE_TILE) * ROUTE_TILE
    e_all = jnp.concatenate([e_p, e_s, jnp.full((t_pad - t_all, LANES), -1, I32)], axis=0)
    pos, cnt = _route(e_all)
    counts = cnt[0, :N_EXPERTS].astype(I32)
    n_tiles = (t_all * TOP_K + N_EXPERTS * (MOE_TILE - 1)) // MOE_TILE
    tile_expert, tile_valid, n_used, pad_base, pad_cnt = _tile_table(counts, n_tiles)
    pos_p = pos[:t_p, :TOP_K]
    pos_s = pos[t_p:t_all, :TOP_K]
    x_sorted = _row_scatter(hp_p, hp_s, pos_p.reshape(-1), pos_s.reshape(-1), pad_base, pad_cnt,
                            n_tiles * MOE_TILE)
    y_rows = _moe_experts(x_sorted, tile_expert, tile_valid, n_used, w_gate[l], b_gate[l], w_up[l], b_up[l],
                          w_down[l], b_down[l])
    y_p = _moe_combine(y_rows, pos_p, w_p, x1_p, gt2_p, g_final, rows_per_mod=seq, tok=_row_tile(seq, 128))
    y_s = _moe_combine(y_rows, pos_s, w_s_top, x1_s, gt2_s, g_final, rows_per_mod=n_s, tok=n_s)

    k0 = N_GROUPS * GROUP_COLS
    v0 = k0 + N_GROUPS * GROUP_COLS
    z_p3 = zqkv_p.reshape(bsz, seq, 3 * N_GROUPS * GROUP_COLS)
    z_s = zqkv_s
    kv_prompt, kv_sample = [], []
    for g, (win, dil) in enumerate(DIL_GROUPS):
        keep = min(win, seq)
        kc = slice(k0 + g * GROUP_COLS, k0 + (g + 1) * GROUP_COLS)
        vc = slice(v0 + g * GROUP_COLS, v0 + (g + 1) * GROUP_COLS)
        kv = jnp.stack([z_p3[:, seq - keep:, kc], z_p3[:, seq - keep:, vc]], axis=2)
        kv_prompt.append(kv.reshape(1, bsz, keep, 2, HEADS_PER_GROUP, HEAD_DIM))
        kvs = jnp.stack([z_s[:, kc], z_s[:, vc]], axis=1)
        kv_sample.append(kvs.reshape(1, n_s, 1, 2, HEADS_PER_GROUP, HEAD_DIM))
    return (y_p.reshape(bsz, seq, d), y_s.reshape(n_s, 1, d),
            kv_prompt[0], kv_prompt[1], kv_prompt[2],
            kv_sample[0], kv_sample[1], kv_sample[2],
            vn_s.reshape(1, n_s, 1, aw))
```

```python
import functools

import jax
import jax.numpy as jnp
from jax import lax
from jax.experimental import pallas as pl
from jax.experimental.pallas import tpu as pltpu

F32 = jnp.float32
BF16 = jnp.bfloat16
U32 = jnp.uint32
I32 = jnp.int32
HIGHEST = lax.Precision.HIGHEST

EPS = 1e-6
A_GROUPS = 8
CHUNK = 128
HEAD_DIM = 128
HEADS_PER_GROUP = 4
DIL_GROUPS = ((128, 1), (512, 4), (2048, 16))
N_GROUPS = len(DIL_GROUPS)
N_HEADS = HEADS_PER_GROUP * N_GROUPS
ALIBI_SLOPES = tuple(2.0 ** (-8.0 * (h + 1) / N_HEADS) for h in range(N_HEADS))
GROUP_COLS = HEADS_PER_GROUP * HEAD_DIM
N_EXPERTS = 32
TOP_K = 4
SWIGLU_LIMIT = 7.0
SWIGLU_ALPHA = 1.702
N_ADA = 6
NEG = -1e30

LANES = 128
SUBLANES = 8
COL_TILE = 512
ATTN_BLOCK = 2048
MOE_TILE = 512
MOE_HALF = MOE_TILE // 2
ROUTE_TILE = 512
VMEM_LIMIT = 56 * 1024 * 1024


def _params(sem, vmem=VMEM_LIMIT):
    return pltpu.CompilerParams(dimension_semantics=sem, vmem_limit_bytes=vmem)


def _dot(a, b, hi):
    if hi:
        return jnp.dot(a, b, precision=HIGHEST, preferred_element_type=F32)
    return jnp.dot(a.astype(BF16), b.astype(BF16), preferred_element_type=F32)


def _dot_t(a, b, hi):
    dn = (((1,), (1,)), ((), ()))
    if hi:
        return lax.dot_general(a, b, dn, precision=HIGHEST, preferred_element_type=F32)
    return lax.dot_general(a.astype(BF16), b.astype(BF16), dn, preferred_element_type=F32)


def _mod_rms(x, g, sc, sh):
    y = x * lax.rsqrt(jnp.mean(x * x, axis=-1, keepdims=True) + EPS)
    return y * g * (1.0 + sc) + sh


def _ada_kernel(c_ref, w_ref, b_ref, o_ref):
    c = c_ref[...]
    o_ref[...] = _dot(c * jax.nn.sigmoid(c), w_ref[...], True) + b_ref[...]


def _ada(c_all, w_ada, b_ada):
    n, d = c_all.shape
    cols = w_ada.shape[1]
    tn = 1024
    return pl.pallas_call(
        _ada_kernel,
        grid=(cols // tn,),
        in_specs=[pl.BlockSpec((n, d), lambda j: (0, 0)),
                  pl.BlockSpec((d, tn), lambda j: (0, j)),
                  pl.BlockSpec((1, tn), lambda j: (0, j))],
        out_specs=pl.BlockSpec((n, tn), lambda j: (0, j)),
        out_shape=jax.ShapeDtypeStruct((n, cols), F32),
        compiler_params=_params(("arbitrary",)),
        name="ada",
    )(c_all, w_ada, b_ada.reshape(1, cols))


def _mix_norm_kernel(x_ref, sc_ref, sh_ref, g_ref, h_ref):
    h_ref[...] = _mod_rms(x_ref[...], g_ref[...], sc_ref[0], sh_ref[0]).astype(h_ref.dtype)


def _mix_norm(x, sc, sh, g, *, rows_per_mod, tm, hi):
    t, d = x.shape
    mod_rows = sc.shape[1]
    mod_spec = pl.BlockSpec((1, mod_rows, d), lambda i: (i * tm // rows_per_mod, 0, 0))
    return pl.pallas_call(
        _mix_norm_kernel,
        grid=(t // tm,),
        in_specs=[pl.BlockSpec((tm, d), lambda i: (i, 0)), mod_spec, mod_spec, pl.BlockSpec((1, d), lambda i: (0, 0))],
        out_specs=pl.BlockSpec((tm, d), lambda i: (i, 0)),
        out_shape=jax.ShapeDtypeStruct((t, d), F32 if hi else BF16),
        compiler_params=_params(("arbitrary",)),
        name="mix_norm",
    )(x, sc, sh, g.reshape(1, d))


def _in_proj_kernel(h_ref, w_ref, o_ref, *, hi, act):
    z = _dot(h_ref[...], w_ref[...], hi)
    o_ref[...] = z if act is None else act(z)


def _in_proj(h, w_in, col0, n_cols, act, *, tm, hi):
    t, d = h.shape
    assert col0 % COL_TILE == 0 and n_cols % COL_TILE == 0
    blk0 = col0 // COL_TILE
    return pl.pallas_call(
        functools.partial(_in_proj_kernel, hi=hi, act=act),
        grid=(t // tm, n_cols // COL_TILE),
        in_specs=[pl.BlockSpec((tm, d), lambda i, j: (i, 0)),
                  pl.BlockSpec((d, COL_TILE), lambda i, j: (0, blk0 + j))],
        out_specs=pl.BlockSpec((tm, COL_TILE), lambda i, j: (i, j)),
        out_shape=jax.ShapeDtypeStruct((t, n_cols), F32),
        compiler_params=_params(("arbitrary", "arbitrary")),
        name="in_proj",
    )(h, w_in)


def _in_proj_all(x, sc, sh, g, w_in, *, rows_per_mod, tm_norm, tm, hi):
    d = x.shape[1]
    aw2 = 2 * A_GROUPS * CHUNK
    qkv = 3 * N_HEADS * HEAD_DIM
    h = _mix_norm(x, sc, sh, g, rows_per_mod=rows_per_mod, tm=tm_norm, hi=hi)
    z_a = _in_proj(h, w_in, 0, aw2, jax.nn.gelu, tm=tm, hi=hi)
    z_qkv = _in_proj(h, w_in, aw2, qkv, None, tm=tm, hi=hi)
    z_gate = _in_proj(h, w_in, aw2 + qkv, 2 * d, jax.nn.sigmoid, tm=tm, hi=hi)
    return z_a, z_qkv, z_gate


def _attend(q, kc, kp, vc, vp, slope, dil, prev_bias):
    n = q.shape[0]
    row = lax.broadcasted_iota(I32, (n, CHUNK, CHUNK), 1)
    col = lax.broadcasted_iota(I32, (n, CHUNK, CHUNK), 2)
    dist_c = ((row - col) * dil).astype(F32)
    dist_p = ((row + CHUNK - col) * dil).astype(F32)
    scale = HEAD_DIM ** -0.5
    qk = (((2,), (2,)), ((0,), (0,)))
    pv = (((2,), (1,)), ((0,), (0,)))
    s_c = lax.dot_general(q, kc, qk, preferred_element_type=F32) * scale - slope * dist_c
    s_p = lax.dot_general(q, kp, qk, preferred_element_type=F32) * scale - slope * dist_p + prev_bias
    s_c = jnp.where(col <= row, s_c, NEG)
    s_p = jnp.where(col >= row, s_p, NEG)
    m = jnp.maximum(jnp.max(s_c, axis=2, keepdims=True), jnp.max(s_p, axis=2, keepdims=True))
    p_c = jnp.exp(s_c - m)
    p_p = jnp.exp(s_p - m)
    l = jnp.sum(p_c, axis=2, keepdims=True) + jnp.sum(p_p, axis=2, keepdims=True)
    o = (lax.dot_general((p_c / l).astype(BF16), vc, pv, preferred_element_type=F32)
         + lax.dot_general((p_p / l).astype(BF16), vp, pv, preferred_element_type=F32))
    return o, m + jnp.log(l)


def _dil_attn_kernel(slope_ref, *refs):
    ins = refs[:5 * N_GROUPS]
    o_ref = refs[5 * N_GROUPS]
    acc_refs = refs[5 * N_GROUPS + 1:5 * N_GROUPS + 1 + N_GROUPS]
    lse_refs = refs[5 * N_GROUPS + 1 + N_GROUPS:]
    head = pl.program_id(1)
    i = pl.program_id(2)
    no_prev = jnp.where(i > 0, 0.0, NEG)

    for g, (win, dil) in enumerate(DIL_GROUPS):
        q_ref, k_ref, v_ref, kp_ref, vp_ref = ins[5 * g:5 * g + 5]
        acc_ref, lse_ref = acc_refs[g], lse_refs[g]
        slope = slope_ref[g * HEADS_PER_GROUP + head]
        span = CHUNK * dil
        n_blocks = ATTN_BLOCK // CHUNK

        def rows(start, dil=dil):
            return pl.ds(start, CHUNK) if dil == 1 else pl.ds(start, CHUNK, stride=dil)

        cur = [rows((b // dil) * span + b % dil) for b in range(n_blocks)]
        prev = [rows((b // dil - 1) * span + b % dil) for b in range(n_blocks)]

        def gather(ref, first_ref=None):
            tiles = []
            for b in range(n_blocks):
                if first_ref is None:
                    tiles.append(ref[cur[b], :])
                elif b < dil:
                    tiles.append(first_ref[rows(b), :])
                else:
                    tiles.append(ref[prev[b], :])
            return jnp.stack(tiles).astype(BF16)

        first_blocks = lax.broadcasted_iota(I32, (n_blocks, 1, 1), 0) < dil
        o, lse = _attend(gather(q_ref), gather(k_ref), gather(k_ref, kp_ref), gather(v_ref), gather(v_ref, vp_ref),
                         slope, dil, jnp.where(first_blocks, no_prev, 0.0))
        for b in range(n_blocks):
            acc_ref[cur[b], :] = o[b]
            lse_ref[cur[b], :] = jnp.broadcast_to(lse[b], (CHUNK, HEAD_DIM))

    ls = [r[...] for r in lse_refs]
    m = jnp.maximum(jnp.maximum(ls[0], ls[1]), ls[2])
    e = [jnp.exp(l - m) for l in ls]
    den = e[0] + e[1] + e[2]
    o_ref[...] = (acc_refs[0][...] * (e[0] / den) + acc_refs[1][...] * (e[1] / den)
                  + acc_refs[2][...] * (e[2] / den))


def _dil_attn(z, bsz, seq):
    assert seq % ATTN_BLOCK == 0
    nblk = seq // ATTN_BLOCK
    a_blocks = 0
    in_specs = [pl.BlockSpec(memory_space=pltpu.SMEM)]
    for g, (win, dil) in enumerate(DIL_GROUPS):
        span = CHUNK * dil
        per_blk = ATTN_BLOCK // span
        for which in range(3):
            cb = a_blocks + (which * N_GROUPS + g) * HEADS_PER_GROUP
            in_specs.append(pl.BlockSpec((ATTN_BLOCK, HEAD_DIM), lambda b, h, i, cb=cb: (b * nblk + i, cb + h)))
        for which in (1, 2):
            cb = a_blocks + (which * N_GROUPS + g) * HEADS_PER_GROUP
            in_specs.append(pl.BlockSpec(
                (span, HEAD_DIM),
                lambda b, h, i, cb=cb, per_blk=per_blk: (jnp.maximum((b * nblk + i) * per_blk - 1, 0), cb + h)))
    slopes = jnp.asarray(ALIBI_SLOPES, F32)
    blk = pltpu.VMEM((ATTN_BLOCK, HEAD_DIM), F32)
    return pl.pallas_call(
        _dil_attn_kernel,
        grid=(bsz, HEADS_PER_GROUP, nblk),
        in_specs=in_specs,
        out_specs=pl.BlockSpec((ATTN_BLOCK, HEAD_DIM), lambda b, h, i: (b * nblk + i, h)),
        out_shape=jax.ShapeDtypeStruct((bsz * seq, GROUP_COLS), F32),
        scratch_shapes=[blk] * (2 * N_GROUPS),
        compiler_params=_params(("arbitrary", "arbitrary", "arbitrary")),
        name="dil_attn",
    )(slopes, *([z] * (5 * N_GROUPS)))


def _sample_attn_kernel(pen_ref, q_ref, k_ref, v_ref, c0_ref, c1_ref, c2_ref, o_ref):
    caches = (c0_ref, c1_ref, c2_ref)
    scale = HEAD_DIM ** -0.5
    heads = [(g, h) for g in range(N_GROUPS) for h in range(HEADS_PER_GROUP)]

    def head_rows(ref):
        return jnp.stack([ref[0, :, i * HEAD_DIM:(i + 1) * HEAD_DIM] for i in range(N_HEADS)])

    q = head_rows(q_ref)
    k_new = head_rows(k_ref)
    v_new = head_rows(v_ref)
    k_buf = jnp.stack([caches[g][:, 0, h, :] for g, h in heads])
    v_buf = jnp.stack([caches[g][:, 1, h, :] for g, h in heads])
    q8 = jnp.broadcast_to(q, (N_HEADS, SUBLANES, HEAD_DIM))
    qk = (((2,), (2,)), ((0,), (0,)))
    pv = (((2,), (1,)), ((0,), (0,)))
    s_buf = lax.dot_general(q8, k_buf, qk, precision=HIGHEST, preferred_element_type=F32)[:, 0:1, :]
    s_buf = s_buf * scale - pen_ref[...]
    s_new = jnp.sum(q * k_new, axis=2, keepdims=True) * scale
    m = jnp.maximum(jnp.max(s_buf, axis=2, keepdims=True), s_new)
    p_buf = jnp.exp(s_buf - m)
    p_new = jnp.exp(s_new - m)
    l = jnp.sum(p_buf, axis=2, keepdims=True) + p_new
    pb8 = jnp.broadcast_to(p_buf / l, (N_HEADS, SUBLANES, CHUNK))
    o = lax.dot_general(pb8, v_buf, pv, precision=HIGHEST, preferred_element_type=F32)[:, 0:1, :]
    o = o + (p_new / l) * v_new
    lse = m + jnp.log(l)
    n = HEADS_PER_GROUP
    ls = [lse[g * n:(g + 1) * n] for g in range(N_GROUPS)]
    mm = jnp.maximum(jnp.maximum(ls[0], ls[1]), ls[2])
    e = [jnp.exp(x - mm) for x in ls]
    den = e[0] + e[1] + e[2]
    out = o[0:n] * (e[0] / den) + o[n:2 * n] * (e[1] / den) + o[2 * n:3 * n] * (e[2] / den)
    for h in range(HEADS_PER_GROUP):
        o_ref[0, :, h * HEAD_DIM:(h + 1) * HEAD_DIM] = out[h]


def _sample_attn(zs, caches):
    n, cols = zs.shape
    z3 = zs.reshape(n, 1, cols)
    a_blocks = 0
    qkv_cols = N_GROUPS * GROUP_COLS
    c_views = []
    c_specs = []
    for g, (win, dil) in enumerate(DIL_GROUPS):
        c = caches[g]
        assert c.shape[1] == win, "cache must hold exactly one window"
        c_views.append(c.reshape(n, win // dil, dil, 2, HEADS_PER_GROUP, HEAD_DIM))
        c_specs.append(pl.BlockSpec((None, win // dil, None, 2, HEADS_PER_GROUP, HEAD_DIM),
                                    lambda b: (b, 0, 0, 0, 0, 0)))
    q0 = a_blocks * GROUP_COLS
    steps = jnp.arange(CHUNK, 0, -1, dtype=F32)
    pen = jnp.stack([ALIBI_SLOPES[g * HEADS_PER_GROUP + h] * dil * steps
                     for g, (win, dil) in enumerate(DIL_GROUPS) for h in range(HEADS_PER_GROUP)])
    assert all(win // dil == CHUNK for win, dil in DIL_GROUPS)
    row_spec = pl.BlockSpec((1, 1, qkv_cols), lambda b: (b, 0, 0))
    out = pl.pallas_call(
        _sample_attn_kernel,
        grid=(n,),
        in_specs=[pl.BlockSpec((N_HEADS, 1, CHUNK), lambda b: (0, 0, 0)), row_spec, row_spec, row_spec] + c_specs,
        out_specs=pl.BlockSpec((1, 1, GROUP_COLS), lambda b: (b, 0, 0)),
        out_shape=jax.ShapeDtypeStruct((n, 1, GROUP_COLS), F32),
        compiler_params=_params(("arbitrary",)),
        name="sample_attn",
    )(pen.reshape(N_HEADS, 1, CHUNK), z3[:, :, q0:q0 + qkv_cols], z3[:, :, q0 + qkv_cols:q0 + 2 * qkv_cols],
      z3[:, :, q0 + 2 * qkv_cols:q0 + 3 * qkv_cols], *c_views)
    return out.reshape(n, GROUP_COLS)


def _layer_norm(v, g, b):
    mu = jnp.mean(v, axis=-1, keepdims=True)
    var = jnp.mean(jnp.square(v - mu), axis=-1, keepdims=True)
    return (v - mu) * lax.rsqrt(var + EPS) * g + b


def _chunk_gate_kernel(u_ref, v_ref, lg_ref, lb_ref, ws_ref, bs_ref, o_ref, *, n_chunks):
    va = _layer_norm(v_ref[...], lg_ref[...], lb_ref[...]).astype(BF16)
    row = lax.broadcasted_iota(I32, (CHUNK, CHUNK), 0)
    col = lax.broadcasted_iota(I32, (CHUNK, CHUNK), 1)
    for g in range(A_GROUPS):
        w = jnp.where(col <= row, ws_ref[g], 0.0).astype(BF16)
        gs = slice(g * CHUNK, (g + 1) * CHUNK)
        for c in range(n_chunks):
            rs = slice(c * CHUNK, (c + 1) * CHUNK)
            mixed = jnp.dot(w, va[rs, gs], preferred_element_type=F32) + bs_ref[g]
            o_ref[rs, gs] = (u_ref[rs, gs] * mixed).astype(o_ref.dtype)


def _chunk_gate(z, ln_g, ln_b, w_s, b_s, *, tm):
    t = z.shape[0]
    aw = A_GROUPS * CHUNK
    bs_full = jnp.broadcast_to(b_s[:, :, None], (A_GROUPS, CHUNK, CHUNK))
    kern = functools.partial(_chunk_gate_kernel, n_chunks=tm // CHUNK)
    return pl.pallas_call(
        kern,
        grid=(t // tm,),
        in_specs=[pl.BlockSpec((tm, aw), lambda i: (i, 0)),
                  pl.BlockSpec((tm, aw), lambda i: (i, 1)),
                  pl.BlockSpec((1, aw), lambda i: (0, 0)),
                  pl.BlockSpec((1, aw), lambda i: (0, 0)),
                  pl.BlockSpec((A_GROUPS, CHUNK, CHUNK), lambda i: (0, 0, 0)),
                  pl.BlockSpec((A_GROUPS, CHUNK, CHUNK), lambda i: (0, 0, 0))],
        out_specs=pl.BlockSpec((tm, aw), lambda i: (i, 0)),
        out_shape=jax.ShapeDtypeStruct((t, aw), BF16),
        compiler_params=_params(("arbitrary",)),
        name="chunk_gate",
    )(z, z, ln_g.reshape(1, aw), ln_b.reshape(1, aw), w_s, bs_full)


def _sample_gate_kernel(u_ref, v_ref, lg_ref, lb_ref, w0_ref, b0_ref, a_ref, vn_ref):
    va = _layer_norm(v_ref[...], lg_ref[...], lb_ref[...])
    vn_ref[...] = va
    a_ref[...] = u_ref[...] * (w0_ref[...] * va + b0_ref[...])


def _sample_gate(zs, ln_g, ln_b, w_s, b_s):
    n = zs.shape[0]
    aw = A_GROUPS * CHUNK
    w0 = jnp.repeat(w_s[:, 0, 0], CHUNK).reshape(1, aw)
    b0 = jnp.repeat(b_s[:, 0], CHUNK).reshape(1, aw)
    vec = pl.BlockSpec((1, aw), lambda i: (0, 0))
    return pl.pallas_call(
        _sample_gate_kernel,
        grid=(1,),
        in_specs=[pl.BlockSpec((n, aw), lambda i: (0, 0)), pl.BlockSpec((n, aw), lambda i: (0, 1)),
                  vec, vec, vec, vec],
        out_specs=[pl.BlockSpec((n, aw), lambda i: (0, 0)), pl.BlockSpec((n, aw), lambda i: (0, 0))],
        out_shape=[jax.ShapeDtypeStruct((n, aw), F32), jax.ShapeDtypeStruct((n, aw), F32)],
        compiler_params=_params(("arbitrary",)),
        name="sample_gate",
    )(zs, zs, ln_g.reshape(1, aw), ln_b.reshape(1, aw), w0, b0)


def _merge_kernel(a_ref, b_ref, ga_ref, gb_ref, pa_ref, pb_ref, out_ref, *, hi):
    ya = _dot(a_ref[...], pa_ref[...], hi)
    yb = _dot(b_ref[...], pb_ref[...], hi)
    out_ref[...] = (ga_ref[...] * ya + gb_ref[...] * yb).astype(out_ref.dtype)


def _merge(a_out, b_out, z, p_a, p_b, *, tm, hi):
    t, aw = a_out.shape
    d = p_a.shape[1]
    tn = COL_TILE
    ga_blk0 = 0
    gb_blk0 = d // tn
    return pl.pallas_call(
        functools.partial(_merge_kernel, hi=hi),
        grid=(t // tm, d // tn),
        in_specs=[pl.BlockSpec((tm, aw), lambda i, j: (i, 0)),
                  pl.BlockSpec((tm, GROUP_COLS), lambda i, j: (i, 0)),
                  pl.BlockSpec((tm, tn), lambda i, j: (i, ga_blk0 + j)),
                  pl.BlockSpec((tm, tn), lambda i, j: (i, gb_blk0 + j)),
                  pl.BlockSpec((aw, tn), lambda i, j: (0, j)),
                  pl.BlockSpec((GROUP_COLS, tn), lambda i, j: (0, j))],
        out_specs=pl.BlockSpec((tm, tn), lambda i, j: (i, j)),
        out_shape=jax.ShapeDtypeStruct((t, d), F32 if hi else BF16),
        compiler_params=_params(("arbitrary", "arbitrary")),
        name="merge",
    )(a_out, b_out, z, z, p_a, p_b)


def _out_proj_kernel(m_ref, w_ref, x_ref, gt_ref, o_ref, *, hi):
    o_ref[...] = x_ref[...] + gt_ref[0] * _dot(m_ref[...], w_ref[...], hi)


def _out_proj(merged, w_o, x, gt, *, rows_per_mod, tm, hi):
    t, d = x.shape
    tn = COL_TILE
    mod_rows = gt.shape[1]
    return pl.pallas_call(
        functools.partial(_out_proj_kernel, hi=hi),
        grid=(t // tm, d // tn),
        in_specs=[pl.BlockSpec((tm, d), lambda i, j: (i, 0)),
                  pl.BlockSpec((d, tn), lambda i, j: (0, j)),
                  pl.BlockSpec((tm, tn), lambda i, j: (i, j)),
                  pl.BlockSpec((1, mod_rows, tn), lambda i, j: (i * tm // rows_per_mod, 0, j))],
        out_specs=pl.BlockSpec((tm, tn), lambda i, j: (i, j)),
        out_shape=jax.ShapeDtypeStruct((t, d), F32),
        compiler_params=_params(("arbitrary", "arbitrary")),
        name="out_proj",
    )(merged, w_o, x, gt)


def _pack_pair(lo, hi, pair_ref):
    n = lo.shape[0]
    pair_ref[pl.ds(0, n, stride=2), :] = lo
    pair_ref[pl.ds(1, n, stride=2), :] = hi
    return pltpu.bitcast(pair_ref[0:2 * n, :].astype(BF16), U32)


def _unpack_pair(w, pair_ref):
    n = w.shape[0]
    pair_ref[0:2 * n, :] = pltpu.bitcast(w, BF16).astype(F32)
    return pair_ref[pl.ds(0, n, stride=2), :].astype(BF16), pair_ref[pl.ds(1, n, stride=2), :].astype(BF16)


def _ffn_norm_kernel(x_ref, sc_ref, sh_ref, g_ref, wr_ref, br_ref, hp_ref, e_ref, w_ref, pair_ref):
    h = _mod_rms(x_ref[...], g_ref[...], sc_ref[0], sh_ref[0])
    tm, d = h.shape
    for s in range(d // (2 * LANES)):
        lo = h[:, (2 * s) * LANES:(2 * s + 1) * LANES]
        hi = h[:, (2 * s + 1) * LANES:(2 * s + 2) * LANES]
        hp_ref[pl.ds(s, tm, stride=SUBLANES), :] = _pack_pair(lo, hi, pair_ref)
    logits = _dot(h, wr_ref[...], True) + br_ref[...]
    lane = lax.broadcasted_iota(I32, logits.shape, 1)
    lane_f = lane.astype(F32)
    vals, idxs = [], []
    for _ in range(TOP_K):
        m = jnp.max(logits, axis=1, keepdims=True)
        idx = jnp.min(jnp.where(logits == m, lane_f, float(logits.shape[1])), axis=1, keepdims=True)
        vals.append(m)
        idxs.append(idx)
        logits = jnp.where(lane_f == idx, 2.0 * NEG, logits)
    es = [jnp.exp(v - vals[0]) for v in vals]
    den = es[0] + es[1] + es[2] + es[3]
    e_out = jnp.full(lane.shape, -1.0, F32)
    w_out = jnp.zeros(lane.shape, F32)
    for k in range(TOP_K):
        e_out = jnp.where(lane == k, idxs[k], e_out)
        w_out = jnp.where(lane == k, es[k] / den, w_out)
    e_ref[...] = e_out.astype(I32)
    w_ref[...] = w_out


def _ffn_norm(x1, sc, sh, g, w_router, b_router, *, rows_per_mod, tm):
    t, d = x1.shape
    assert d == 2 * LANES * SUBLANES, "one packed row must be exactly one (8, 128) tile"
    ne = w_router.shape[1]
    wr = jnp.zeros((d, LANES), F32).at[:, :ne].set(w_router)
    br = jnp.full((1, LANES), NEG, F32).at[0, :ne].set(b_router)
    mod_rows = sc.shape[1]
    mod_spec = pl.BlockSpec((1, mod_rows, d), lambda i: (i * tm // rows_per_mod, 0, 0))
    return pl.pallas_call(
        _ffn_norm_kernel,
        grid=(t // tm,),
        in_specs=[pl.BlockSpec((tm, d), lambda i: (i, 0)), mod_spec, mod_spec,
                  pl.BlockSpec((1, d), lambda i: (0, 0)),
                  pl.BlockSpec((d, LANES), lambda i: (0, 0)),
                  pl.BlockSpec((1, LANES), lambda i: (0, 0))],
        out_specs=[pl.BlockSpec((tm * SUBLANES, LANES), lambda i: (i, 0)),
                   pl.BlockSpec((tm, LANES), lambda i: (i, 0)),
                   pl.BlockSpec((tm, LANES), lambda i: (i, 0))],
        out_shape=[jax.ShapeDtypeStruct((t * SUBLANES, LANES), U32),
                   jax.ShapeDtypeStruct((t, LANES), I32),
                   jax.ShapeDtypeStruct((t, LANES), F32)],
        scratch_shapes=[pltpu.VMEM((2 * tm, LANES), F32)],
        compiler_params=_params(("arbitrary",)),
        name="ffn_norm",
    )(x1, sc, sh, g.reshape(1, d), wr, br)


def _route_kernel(e_ref, pos_ref, cnt_ref, tri_ref, run_ref):
    phase = pl.program_id(0)
    i = pl.program_id(1)
    e = e_ref[...]
    tm = e.shape[0]
    lane = lax.broadcasted_iota(I32, (tm, LANES), 1)
    hits = [lane == e[:, k:k + 1] for k in range(TOP_K)]
    chosen = jnp.zeros((tm, LANES), F32)
    for k in range(TOP_K):
        chosen = jnp.where(hits[k], 1.0, chosen)
    col_count = jnp.sum(chosen, axis=0, keepdims=True)

    @pl.when((phase == 0) & (i == 0))
    def _():
        cnt_ref[...] = jnp.zeros(cnt_ref.shape, F32)
        r = lax.broadcasted_iota(I32, (tm, tm), 0)
        c = lax.broadcasted_iota(I32, (tm, tm), 1)
        tri_ref[...] = jnp.where(c < r, 1.0, 0.0).astype(BF16)

    @pl.when(phase == 0)
    def _():
        cnt_ref[...] = cnt_ref[...] + col_count

    @pl.when((phase == 1) & (i == 0))
    def _():
        tiles = jnp.floor((cnt_ref[...] + (MOE_TILE - 1)) * (1.0 / MOE_TILE))
        r = lax.broadcasted_iota(I32, (LANES, LANES), 0)
        c = lax.broadcasted_iota(I32, (LANES, LANES), 1)
        below = jnp.where(r < c, 1.0, 0.0)
        tiles8 = jnp.broadcast_to(tiles, (SUBLANES, LANES))
        run_ref[...] = _dot(tiles8, below, True)[0:1] * float(MOE_TILE)

    @pl.when(phase == 1)
    def _():
        before = jnp.dot(tri_ref[...], chosen.astype(BF16), preferred_element_type=F32)
        dest = run_ref[...] + before
        out = jnp.zeros((tm, LANES), F32)
        for k in range(TOP_K):
            p = jnp.sum(jnp.where(hits[k], dest, 0.0), axis=1, keepdims=True)
            out = jnp.where(lane == k, p, out)
        pos_ref[...] = out.astype(I32)
        run_ref[...] = run_ref[...] + col_count


def _route(e_all):
    t = e_all.shape[0]
    steps = t // ROUTE_TILE
    return pl.pallas_call(
        _route_kernel,
        grid=(2, steps),
        in_specs=[pl.BlockSpec((ROUTE_TILE, LANES), lambda p, i: (i, 0))],
        out_specs=[pl.BlockSpec((ROUTE_TILE, LANES), lambda p, i: (i * p, 0)),
                   pl.BlockSpec((1, LANES), lambda p, i: (0, 0))],
        out_shape=[jax.ShapeDtypeStruct((t, LANES), I32), jax.ShapeDtypeStruct((1, LANES), F32)],
        scratch_shapes=[pltpu.VMEM((ROUTE_TILE, ROUTE_TILE), BF16), pltpu.VMEM((1, LANES), F32)],
        compiler_params=_params(("arbitrary", "arbitrary")),
        name="route",
    )(e_all)


def _row_scatter_kernel(pos_ref, pos_s_ref, pad_base_ref, pad_cnt_ref, hp_ref, hs_ref, xs_hbm, sem, *,
                        n_prompt_steps, n_sample):
    i = pl.program_id(0)

    def rows(first, n=1):
        return pl.ds(pl.multiple_of(first * SUBLANES, SUBLANES), n * SUBLANES)

    def scatter(src_ref, idx_ref, n_tok):
        def body(tt, carry):
            for k in range(TOP_K):
                pltpu.make_async_copy(src_ref.at[rows(tt)], xs_hbm.at[rows(idx_ref[0, 0, tt * TOP_K + k])],
                                      sem).start()
            return carry
        lax.fori_loop(0, n_tok, body, 0, unroll=2)
        for _ in range(TOP_K):
            pltpu.make_async_copy(src_ref.at[rows(0, n_tok)], xs_hbm.at[rows(0, n_tok)], sem).wait()

    @pl.when(i < n_prompt_steps)
    def _():
        scatter(hp_ref, pos_ref, ROUTE_TILE)

    @pl.when(i == n_prompt_steps)
    def _():
        scatter(hs_ref, pos_s_ref, n_sample)

        def fill(dst_row, size):
            return pltpu.make_async_copy(hp_ref.at[rows(0, size)], xs_hbm.at[rows(dst_row, size)], sem)

        def per_segment(wait):
            def body(e, carry):
                cnt = pad_cnt_ref[e]
                base = pad_base_ref[e]

                def whole(r, c):
                    cp = fill(base + (cnt % MOE_TILE) + r * MOE_TILE, MOE_TILE)
                    cp.wait() if wait else cp.start()
                    return c
                lax.fori_loop(0, cnt // MOE_TILE, whole, 0)
                size = MOE_TILE // 2
                while size >= 1:
                    @pl.when((cnt & size) != 0)
                    def _(size=size):
                        cp = fill(base + (cnt & (size - 1)), size)
                        cp.wait() if wait else cp.start()
                    size //= 2
                return carry
            return body
        lax.fori_loop(0, pad_cnt_ref.shape[0], per_segment(False), 0)
        lax.fori_loop(0, pad_cnt_ref.shape[0], per_segment(True), 0)


def _row_scatter(hp_p, hp_s, pos_p, pos_s, pad_base, pad_cnt, n_rows):
    t_p = hp_p.shape[0] // SUBLANES
    n_s = hp_s.shape[0] // SUBLANES
    assert ROUTE_TILE >= MOE_TILE, "padding blocks are copied from one token tile"
    steps = t_p // ROUTE_TILE
    tile_rows = ROUTE_TILE * TOP_K
    return pl.pallas_call(
        functools.partial(_row_scatter_kernel, n_prompt_steps=steps, n_sample=n_s),
        grid=(steps + 1,),
        in_specs=[pl.BlockSpec((1, 1, tile_rows), lambda i: (jnp.minimum(i, steps - 1), 0, 0),
                               memory_space=pltpu.SMEM),
                  pl.BlockSpec((1, 1, n_s * TOP_K), lambda i: (0, 0, 0), memory_space=pltpu.SMEM),
                  pl.BlockSpec(memory_space=pltpu.SMEM),
                  pl.BlockSpec(memory_space=pltpu.SMEM),
                  pl.BlockSpec((ROUTE_TILE * SUBLANES, LANES), lambda i: (jnp.minimum(i, steps - 1), 0)),
                  pl.BlockSpec((n_s * SUBLANES, LANES), lambda i: (0, 0))],
        out_specs=pl.BlockSpec(memory_space=pl.ANY),
        out_shape=jax.ShapeDtypeStruct((n_rows * SUBLANES, LANES), U32),
        scratch_shapes=[pltpu.SemaphoreType.DMA(())],
        compiler_params=_params(("arbitrary",)),
        name="row_scatter",
    )(pos_p.reshape(steps, 1, tile_rows), pos_s.reshape(1, 1, n_s * TOP_K), pad_base, pad_cnt, hp_p, hp_s)


def _moe_up_kernel(te_ref, nv_ref, nu_ref, x_ref, wg_ref, wu_ref, bg_ref, bu_ref, h_ref, wg_s, wu_s, xb_s,
                   pair_ref):
    t = pl.program_id(1)
    e = te_ref[t]
    prev = te_ref[jnp.maximum(t - 1, 0)]

    @pl.when((t == 0) | (e != prev))
    def _():
        wg_s[...] = wg_ref[0].astype(BF16)
        wu_s[...] = wu_ref[0].astype(BF16)

    def compute(rows):
        for s in range(SUBLANES):
            lo, hi = _unpack_pair(x_ref[pl.ds(s, rows, stride=SUBLANES), :], pair_ref)
            xb_s[0:rows, (2 * s) * LANES:(2 * s + 1) * LANES] = lo
            xb_s[0:rows, (2 * s + 1) * LANES:(2 * s + 2) * LANES] = hi
        x = xb_s[0:rows, :]
        gt = jnp.dot(x, wg_s[...], preferred_element_type=F32) + bg_ref[0]
        up = jnp.dot(x, wu_s[...], preferred_element_type=F32) + bu_ref[0]
        gt = jnp.minimum(gt, SWIGLU_LIMIT)
        up = jnp.clip(up, -SWIGLU_LIMIT, SWIGLU_LIMIT)
        h_ref[0:rows, :] = ((up + 1.0) * gt * jax.nn.sigmoid(SWIGLU_ALPHA * gt)).astype(h_ref.dtype)

    active = t < nu_ref[0]
    full = nv_ref[t] > MOE_HALF

    @pl.when(active & full)
    def _():
        compute(MOE_TILE)

    @pl.when(active & jnp.logical_not(full))
    def _():
        compute(MOE_HALF)
        h_ref[MOE_HALF:, :] = jnp.zeros((MOE_TILE - MOE_HALF, h_ref.shape[1]), h_ref.dtype)

    @pl.when(jnp.logical_not(active))
    def _():
        h_ref[...] = jnp.zeros(h_ref.shape, h_ref.dtype)


def _moe_down_kernel(te_ref, nv_ref, nu_ref, h_ref, wd_ref, bd_ref, y_ref, wd_s):
    t = pl.program_id(1)
    e = te_ref[t]
    prev = te_ref[jnp.maximum(t - 1, 0)]
    n_sub = y_ref.shape[0] // MOE_TILE

    @pl.when((t == 0) | (e != prev))
    def _():
        wd_s[...] = wd_ref[0].astype(BF16)

    def store(rows, y):
        for c in range(n_sub):
            y_ref[pl.ds(c, rows, stride=n_sub), :] = y[:, c * LANES:(c + 1) * LANES]

    def compute(rows):
        store(rows, jnp.dot(h_ref[0:rows, :], wd_s[...], preferred_element_type=F32) + bd_ref[0])

    active = t < nu_ref[0]
    full = nv_ref[t] > MOE_HALF

    @pl.when(active & full)
    def _():
        compute(MOE_TILE)

    @pl.when(active & jnp.logical_not(full))
    def _():
        compute(MOE_HALF)
        y_ref[MOE_HALF * n_sub:, :] = jnp.zeros(((MOE_TILE - MOE_HALF) * n_sub, LANES), F32)

    @pl.when(jnp.logical_not(active))
    def _():
        y_ref[...] = jnp.zeros(y_ref.shape, F32)


def _moe_experts(x_sorted, tile_expert, tile_valid, n_used, w_gate, b_gate, w_up, b_up, w_down, b_down):
    rows = x_sorted.shape[0] // SUBLANES
    ne, d, f = w_gate.shape
    nt = rows // MOE_TILE
    tf = 1024
    tn = d
    n_sub = d // LANES
    x2 = x_sorted

    def row_map(j, t, te, nv, nu):
        return (jnp.minimum(t, nu[0] - 1), 0)

    def w_map(j, t, te, nv, nu):
        return (te[t], 0, j)

    h = pl.pallas_call(
        _moe_up_kernel,
        grid_spec=pltpu.PrefetchScalarGridSpec(
            num_scalar_prefetch=3,
            grid=(f // tf, nt),
            in_specs=[pl.BlockSpec((MOE_TILE * SUBLANES, LANES), row_map),
                      pl.BlockSpec((1, d, tf), w_map),
                      pl.BlockSpec((1, d, tf), w_map),
                      pl.BlockSpec((1, 1, tf), w_map),
                      pl.BlockSpec((1, 1, tf), w_map)],
            out_specs=pl.BlockSpec((MOE_TILE, tf), lambda j, t, te, nv, nu: (t, j)),
            scratch_shapes=[pltpu.VMEM((d, tf), BF16), pltpu.VMEM((d, tf), BF16), pltpu.VMEM((MOE_TILE, d), BF16),
                            pltpu.VMEM((2 * MOE_TILE, LANES), F32)]),
        out_shape=jax.ShapeDtypeStruct((rows, f), BF16),
        compiler_params=_params(("arbitrary", "arbitrary")),
        name="moe_up",
    )(tile_expert, tile_valid, n_used, x2, w_gate, w_up, b_gate.reshape(ne, 1, f), b_up.reshape(ne, 1, f))

    y = pl.pallas_call(
        _moe_down_kernel,
        grid_spec=pltpu.PrefetchScalarGridSpec(
            num_scalar_prefetch=3,
            grid=(d // tn, nt),
            in_specs=[pl.BlockSpec((MOE_TILE, f), row_map),
                      pl.BlockSpec((1, f, tn), w_map),
                      pl.BlockSpec((1, 1, tn), w_map)],
            out_specs=pl.BlockSpec((MOE_TILE * n_sub, LANES), lambda j, t, te, nv, nu: (t, 0)),
            scratch_shapes=[pltpu.VMEM((f, tn), BF16)]),
        out_shape=jax.ShapeDtypeStruct((rows * n_sub, LANES), F32),
        compiler_params=_params(("arbitrary", "arbitrary")),
        name="moe_down",
    )(tile_expert, tile_valid, n_used, h, w_down, b_down.reshape(ne, 1, d))
    return y.reshape(rows, d // LANES, LANES)


def _row_copy(src_hbm, buf, sem, slot, src_row, dst_row, n_sub):
    dst = pl.ds(pl.multiple_of(dst_row * n_sub, SUBLANES), n_sub)
    return pltpu.make_async_copy(src_hbm.at[src_row], buf.at[slot, dst], sem.at[slot])


def _issue_rows(idx_ref, src_hbm, buf, sem, slot, n_rows, n_sub):
    def body(r, carry):
        _row_copy(src_hbm, buf, sem, slot, idx_ref[0, 0, r], r, n_sub).start()
        return carry
    lax.fori_loop(0, n_rows, body, 0, unroll=8)


def _moe_combine_kernel(idx_ref, nxt_ref, y_hbm, w_ref, x_ref, gt_ref, g_ref, o_ref, buf, sem, *, tok):
    i = pl.program_id(0)
    n = pl.num_programs(0)
    n_rows = TOP_K * tok
    n_sub = y_hbm.shape[1]
    slot = lax.rem(i, 2)

    @pl.when(i == 0)
    def _():
        _issue_rows(idx_ref, y_hbm, buf, sem, 0, n_rows, n_sub)

    for s in range(2):
        @pl.when((i + 1 < n) & (slot == s))
        def _():
            _issue_rows(nxt_ref, y_hbm, buf, sem, 1 - s, n_rows, n_sub)

    pltpu.make_async_copy(buf.at[slot], buf.at[slot], sem.at[slot]).wait()

    w = w_ref[...]
    sumsq = jnp.zeros((tok, 1), F32)
    for c in range(n_sub):
        cs = slice(c * LANES, (c + 1) * LANES)
        acc = jnp.zeros((tok, LANES), F32)
        for k in range(TOP_K):
            part = buf[slot, pl.ds(k * tok * n_sub + c, tok, stride=n_sub), :]
            acc = acc + w[:, k:k + 1] * part
        x = x_ref[:, cs] + gt_ref[0, :, cs] * acc
        o_ref[:, cs] = x
        sumsq = sumsq + jnp.sum(x * x, axis=1, keepdims=True)
    o_ref[...] = o_ref[...] * lax.rsqrt(sumsq / (n_sub * LANES) + EPS) * g_ref[...]


def _moe_combine(y_rows, pos, top_w, x1, gt, g_final, *, rows_per_mod, tok):
    t, d = x1.shape
    n_sub = y_rows.shape[1]
    steps = t // tok
    n_rows = TOP_K * tok
    idx3 = pos[:, :TOP_K].reshape(steps, tok, TOP_K).transpose(0, 2, 1).reshape(steps, 1, n_rows)
    lanes = top_w.shape[1]
    mod_rows = gt.shape[1]
    idx_blk = (1, 1, n_rows)
    return pl.pallas_call(
        functools.partial(_moe_combine_kernel, tok=tok),
        grid=(steps,),
        in_specs=[pl.BlockSpec(idx_blk, lambda i: (i, 0, 0), memory_space=pltpu.SMEM),
                  pl.BlockSpec(idx_blk, lambda i: (jnp.minimum(i + 1, steps - 1), 0, 0), memory_space=pltpu.SMEM),
                  pl.BlockSpec(memory_space=pl.ANY),
                  pl.BlockSpec((tok, lanes), lambda i: (i, 0)),
                  pl.BlockSpec((tok, d), lambda i: (i, 0)),
                  pl.BlockSpec((1, mod_rows, d), lambda i: (i * tok // rows_per_mod, 0, 0)),
                  pl.BlockSpec((1, d), lambda i: (0, 0))],
        out_specs=pl.BlockSpec((tok, d), lambda i: (i, 0)),
        out_shape=jax.ShapeDtypeStruct((t, d), F32),
        scratch_shapes=[pltpu.VMEM((2, n_rows * n_sub, LANES), F32), pltpu.SemaphoreType.DMA((2,))],
        compiler_params=_params(("arbitrary",)),
        name="moe_combine",
    )(idx3, idx3, y_rows, top_w, x1, gt, g_final.reshape(1, d))


def _tile_table(counts, n_tiles):
    tiles_e = (counts + MOE_TILE - 1) // MOE_TILE
    tile_end = jnp.cumsum(tiles_e)
    tile_start = tile_end - tiles_e
    n_used = tile_end[-1]
    tile_ids = jnp.minimum(jnp.arange(n_tiles, dtype=I32), n_used - 1)
    tile_expert = jnp.minimum(jnp.sum(tile_end[None, :] <= tile_ids[:, None], axis=1), N_EXPERTS - 1).astype(I32)
    tile_valid = jnp.clip(counts[tile_expert] - (tile_ids - tile_start[tile_expert]) * MOE_TILE, 0, MOE_TILE)
    pad_base = jnp.concatenate([tile_start * MOE_TILE + counts, (n_used * MOE_TILE).reshape(1)])
    pad_cnt = jnp.concatenate([tiles_e * MOE_TILE - counts, ((n_tiles - n_used) * MOE_TILE).reshape(1)])
    return (tile_expert, tile_valid.astype(I32), n_used.astype(I32).reshape(1), pad_base.astype(I32),
            pad_cnt.astype(I32))


def _row_tile(t, want):
    tm = min(t, want)
    assert t % tm == 0
    return tm


def kernel(x_prompt, x_sample, cache_kv_g0, cache_kv_g1, cache_kv_g2, c_prompt, c_sample, w_ada, b_ada, g_mix, w_in, ln_g, ln_b, w_s, b_s, p_a, p_b, w_o, g_ffn, w_router, b_router, w_gate, b_gate, w_up, b_up, w_down, b_down, g_final):
    depth = w_ada.shape[0]
    assert depth == 1, "single-layer trunk"
    bsz, seq, d = x_prompt.shape
    n_s, dec_seq, _ = x_sample.shape
    assert dec_seq == 1, "one new position per sample"
    caches = (cache_kv_g0, cache_kv_g1, cache_kv_g2)
    l = 0
    cols = w_in.shape[2]
    t_p = bsz * seq
    aw = A_GROUPS * CHUNK

    n_c = bsz + n_s
    n_c_pad = -(-n_c // SUBLANES) * SUBLANES
    c_all = jnp.concatenate([c_prompt, c_sample, jnp.zeros((n_c_pad - n_c, d), F32)], axis=0)
    mod = _ada(c_all, w_ada[l], b_ada[l])
    mod_p = mod[:bsz].reshape(bsz, 1, N_ADA, d)
    mod_s = mod[bsz:n_c].reshape(1, n_s, N_ADA, d)
    sh1_p, sc1_p, gt1_p, sh2_p, sc2_p, gt2_p = (mod_p[:, :, k] for k in range(N_ADA))
    sh1_s, sc1_s, gt1_s, sh2_s, sc2_s, gt2_s = (mod_s[:, :, k] for k in range(N_ADA))

    xp = x_prompt.reshape(t_p, d)
    xs = x_sample.reshape(n_s, d)

    tm_big = _row_tile(seq, 1024)
    tm_mid = _row_tile(seq, 512)
    tm_huge = _row_tile(seq, 2048)
    za_p, zqkv_p, zg_p = _in_proj_all(xp, sc1_p, sh1_p, g_mix[l], w_in[l], rows_per_mod=seq, tm_norm=tm_mid,
                                      tm=tm_huge, hi=False)
    b_p = _dil_attn(zqkv_p, bsz, seq)
    a_p = _chunk_gate(za_p, ln_g[l], ln_b[l], w_s[l], b_s[l], tm=tm_mid)
    merged_p = _merge(a_p, b_p, zg_p, p_a[l], p_b[l], tm=tm_big, hi=False)
    x1_p = _out_proj(merged_p, w_o[l], xp, gt1_p, rows_per_mod=seq, tm=tm_huge, hi=False)
    hp_p, e_p, w_p = _ffn_norm(x1_p, sc2_p, sh2_p, g_ffn[l], w_router[l], b_router[l], rows_per_mod=seq, tm=tm_mid)

    za_s, zqkv_s, zg_s = _in_proj_all(xs, sc1_s, sh1_s, g_mix[l], w_in[l], rows_per_mod=n_s, tm_norm=n_s, tm=n_s,
                                      hi=True)
    b_s_out = _sample_attn(zqkv_s, tuple(c[l] for c in caches))
    a_s, vn_s = _sample_gate(za_s, ln_g[l], ln_b[l], w_s[l], b_s[l])
    merged_s = _merge(a_s, b_s_out, zg_s, p_a[l], p_b[l], tm=n_s, hi=True)
    x1_s = _out_proj(merged_s, w_o[l], xs, gt1_s, rows_per_mod=n_s, tm=n_s, hi=True)
    hp_s, e_s, w_s_top = _ffn_norm(x1_s, sc2_s, sh2_s, g_ffn[l], w_router[l], b_router[l], rows_per_mod=n_s, tm=n_s)

    assert t_p % ROUTE_TILE == 0
    t_all = t_p + n_s
    t_pad = -(-t_all // ROUTE_TILE) * ROUTE_TILE
    e_all = jnp.concatenate([e_p, e_s, jnp.full((t_pad - t_all, LANES), -1, I32)], axis=0)
    pos, cnt = _route(e_all)
    counts = cnt[0, :N_EXPERTS].astype(I32)
    n_tiles = (t_all * TOP_K + N_EXPERTS * (MOE_TILE - 1)) // MOE_TILE
    tile_expert, tile_valid, n_used, pad_base, pad_cnt = _tile_table(counts, n_tiles)
    pos_p = pos[:t_p, :TOP_K]
    pos_s = pos[t_p:t_all, :TOP_K]
    x_sorted = _row_scatter(hp_p, hp_s, pos_p.reshape(-1), pos_s.reshape(-1), pad_base, pad_cnt,
                            n_tiles * MOE_TILE)
    y_rows = _moe_experts(x_sorted, tile_expert, tile_valid, n_used, w_gate[l], b_gate[l], w_up[l], b_up[l],
                          w_down[l], b_down[l])
    y_p = _moe_combine(y_rows, pos_p, w_p, x1_p, gt2_p, g_final, rows_per_mod=seq, tok=_row_tile(seq, 128))
    y_s = _moe_combine(y_rows, pos_s, w_s_top, x1_s, gt2_s, g_final, rows_per_mod=n_s, tok=n_s)

    k0 = N_GROUPS * GROUP_COLS
    v0 = k0 + N_GROUPS * GROUP_COLS
    z_p3 = zqkv_p.reshape(bsz, seq, 3 * N_GROUPS * GROUP_COLS)
    z_s = zqkv_s
    kv_prompt, kv_sample = [], []
    for g, (win, dil) in enumerate(DIL_GROUPS):
        keep = min(win, seq)
        kc = slice(k0 + g * GROUP_COLS, k0 + (g + 1) * GROUP_COLS)
        vc = slice(v0 + g * GROUP_COLS, v0 + (g + 1) * GROUP_COLS)
        kv = jnp.stack([z_p3[:, seq - keep:, kc], z_p3[:, seq - keep:, vc]], axis=2)
        kv_prompt.append(kv.reshape(1, bsz, keep, 2, HEADS_PER_GROUP, HEAD_DIM))
        kvs = jnp.stack([z_s[:, kc], z_s[:, vc]], axis=1)
        kv_sample.append(kvs.reshape(1, n_s, 1, 2, HEADS_PER_GROUP, HEAD_DIM))
    return (y_p.reshape(bsz, seq, d), y_s.reshape(n_s, 1, d),
            kv_prompt[0], kv_prompt[1], kv_prompt[2],
            kv_sample[0], kv_sample[1], kv_sample[2],
            vn_s.reshape(1, n_s, 1, aw))
```

```python
import functools

import jax
import jax.numpy as jnp
from jax import lax
from jax.experimental import pallas as pl
from jax.experimental.pallas import tpu as pltpu

F32 = jnp.float32
BF16 = jnp.bfloat16
U32 = jnp.uint32
I32 = jnp.int32
HIGHEST = lax.Precision.HIGHEST

EPS = 1e-6
A_GROUPS = 8
CHUNK = 128
HEAD_DIM = 128
HEADS_PER_GROUP = 4
DIL_GROUPS = ((128, 1), (512, 4), (2048, 16))
N_GROUPS = len(DIL_GROUPS)
N_HEADS = HEADS_PER_GROUP * N_GROUPS
ALIBI_SLOPES = tuple(2.0 ** (-8.0 * (h + 1) / N_HEADS) for h in range(N_HEADS))
GROUP_COLS = HEADS_PER_GROUP * HEAD_DIM
N_EXPERTS = 32
TOP_K = 4
SWIGLU_LIMIT = 7.0
SWIGLU_ALPHA = 1.702
N_ADA = 6
NEG = -1e30

LANES = 128
SUBLANES = 8
COL_TILE = 512
ATTN_BLOCK = 2048
MOE_TILE = 512
MOE_HALF = MOE_TILE // 2
ROUTE_TILE = 512
VMEM_LIMIT = 56 * 1024 * 1024


def _params(sem, vmem=VMEM_LIMIT):
    return pltpu.CompilerParams(dimension_semantics=sem, vmem_limit_bytes=vmem)


def _dot(a, b, hi):
    if hi:
        return jnp.dot(a, b, precision=HIGHEST, preferred_element_type=F32)
    return jnp.dot(a.astype(BF16), b.astype(BF16), preferred_element_type=F32)


def _dot_t(a, b, hi):
    dn = (((1,), (1,)), ((), ()))
    if hi:
        return lax.dot_general(a, b, dn, precision=HIGHEST, preferred_element_type=F32)
    return lax.dot_general(a.astype(BF16), b.astype(BF16), dn, preferred_element_type=F32)


def _mod_rms(x, g, sc, sh):
    y = x * lax.rsqrt(jnp.mean(x * x, axis=-1, keepdims=True) + EPS)
    return y * g * (1.0 + sc) + sh


def _ada_kernel(c_ref, w_ref, b_ref, o_ref):
    c = c_ref[...]
    o_ref[...] = _dot(c * jax.nn.sigmoid(c), w_ref[...], True) + b_ref[...]


def _ada(c_all, w_ada, b_ada):
    n, d = c_all.shape
    cols = w_ada.shape[1]
    tn = 1024
    return pl.pallas_call(
        _ada_kernel,
        grid=(cols // tn,),
        in_specs=[pl.BlockSpec((n, d), lambda j: (0, 0)),
                  pl.BlockSpec((d, tn), lambda j: (0, j)),
                  pl.BlockSpec((1, tn), lambda j: (0, j))],
        out_specs=pl.BlockSpec((n, tn), lambda j: (0, j)),
        out_shape=jax.ShapeDtypeStruct((n, cols), F32),
        compiler_params=_params(("arbitrary",)),
        name="ada",
    )(c_all, w_ada, b_ada.reshape(1, cols))


def _mix_norm_kernel(x_ref, sc_ref, sh_ref, g_ref, h_ref):
    h_ref[...] = _mod_rms(x_ref[...], g_ref[...], sc_ref[0], sh_ref[0]).astype(h_ref.dtype)


def _mix_norm(x, sc, sh, g, *, rows_per_mod, tm, hi):
    t, d = x.shape
    mod_rows = sc.shape[1]
    mod_spec = pl.BlockSpec((1, mod_rows, d), lambda i: (i * tm // rows_per_mod, 0, 0))
    return pl.pallas_call(
        _mix_norm_kernel,
        grid=(t // tm,),
        in_specs=[pl.BlockSpec((tm, d), lambda i: (i, 0)), mod_spec, mod_spec, pl.BlockSpec((1, d), lambda i: (0, 0))],
        out_specs=pl.BlockSpec((tm, d), lambda i: (i, 0)),
        out_shape=jax.ShapeDtypeStruct((t, d), F32 if hi else BF16),
        compiler_params=_params(("arbitrary",)),
        name="mix_norm",
    )(x, sc, sh, g.reshape(1, d))


def _in_proj_kernel(h_ref, w_ref, o_ref, *, hi, act):
    z = _dot(h_ref[...], w_ref[...], hi)
    o_ref[...] = z if act is None else act(z)


def _in_proj(h, w_in, col0, n_cols, act, *, tm, hi):
    t, d = h.shape
    assert col0 % COL_TILE == 0 and n_cols % COL_TILE == 0
    blk0 = col0 // COL_TILE
    return pl.pallas_call(
        functools.partial(_in_proj_kernel, hi=hi, act=act),
        grid=(t // tm, n_cols // COL_TILE),
        in_specs=[pl.BlockSpec((tm, d), lambda i, j: (i, 0)),
                  pl.BlockSpec((d, COL_TILE), lambda i, j: (0, blk0 + j))],
        out_specs=pl.BlockSpec((tm, COL_TILE), lambda i, j: (i, j)),
        out_shape=jax.ShapeDtypeStruct((t, n_cols), F32),
        compiler_params=_params(("arbitrary", "arbitrary")),
        name="in_proj",
    )(h, w_in)


def _in_proj_all(x, sc, sh, g, w_in, *, rows_per_mod, tm_norm, tm, hi):
    d = x.shape[1]
    aw2 = 2 * A_GROUPS * CHUNK
    qkv = 3 * N_HEADS * HEAD_DIM
    h = _mix_norm(x, sc, sh, g, rows_per_mod=rows_per_mod, tm=tm_norm, hi=hi)
    z_a = _in_proj(h, w_in, 0, aw2, jax.nn.gelu, tm=tm, hi=hi)
    z_qkv = _in_proj(h, w_in, aw2, qkv, None, tm=tm, hi=hi)
    z_gate = _in_proj(h, w_in, aw2 + qkv, 2 * d, jax.nn.sigmoid, tm=tm, hi=hi)
    return z_a, z_qkv, z_gate


def _attend(q, kc, kp, vc, vp, slope, dil, prev_bias):
    n = q.shape[0]
    row = lax.broadcasted_iota(I32, (n, CHUNK, CHUNK), 1)
    col = lax.broadcasted_iota(I32, (n, CHUNK, CHUNK), 2)
    dist_c = ((row - col) * dil).astype(F32)
    dist_p = ((row + CHUNK - col) * dil).astype(F32)
    scale = HEAD_DIM ** -0.5
    qk = (((2,), (2,)), ((0,), (0,)))
    pv = (((2,), (1,)), ((0,), (0,)))
    s_c = lax.dot_general(q, kc, qk, preferred_element_type=F32) * scale - slope * dist_c
    s_p = lax.dot_general(q, kp, qk, preferred_element_type=F32) * scale - slope * dist_p + prev_bias
    s_c = jnp.where(col <= row, s_c, NEG)
    s_p = jnp.where(col >= row, s_p, NEG)
    m = jnp.maximum(jnp.max(s_c, axis=2, keepdims=True), jnp.max(s_p, axis=2, keepdims=True))
    p_c = jnp.exp(s_c - m)
    p_p = jnp.exp(s_p - m)
    l = jnp.sum(p_c, axis=2, keepdims=True) + jnp.sum(p_p, axis=2, keepdims=True)
    o = (lax.dot_general((p_c / l).astype(BF16), vc, pv, preferred_element_type=F32)
         + lax.dot_general((p_p / l).astype(BF16), vp, pv, preferred_element_type=F32))
    return o, m + jnp.log(l)


def _dil_attn_kernel(slope_ref, *refs):
    ins = refs[:5 * N_GROUPS]
    o_ref = refs[5 * N_GROUPS]
    acc_refs = refs[5 * N_GROUPS + 1:5 * N_GROUPS + 1 + N_GROUPS]
    lse_refs = refs[5 * N_GROUPS + 1 + N_GROUPS:]
    head = pl.program_id(1)
    i = pl.program_id(2)
    no_prev = jnp.where(i > 0, 0.0, NEG)

    for g, (win, dil) in enumerate(DIL_GROUPS):
        q_ref, k_ref, v_ref, kp_ref, vp_ref = ins[5 * g:5 * g + 5]
        acc_ref, lse_ref = acc_refs[g], lse_refs[g]
        slope = slope_ref[g * HEADS_PER_GROUP + head]
        span = CHUNK * dil
        n_blocks = ATTN_BLOCK // CHUNK

        def rows(start, dil=dil):
            return pl.ds(start, CHUNK) if dil == 1 else pl.ds(start, CHUNK, stride=dil)

        cur = [rows((b // dil) * span + b % dil) for b in range(n_blocks)]
        prev = [rows((b // dil - 1) * span + b % dil) for b in range(n_blocks)]

        def gather(ref, first_ref=None):
            tiles = []
            for b in range(n_blocks):
                if first_ref is None:
                    tiles.append(ref[cur[b], :])
                elif b < dil:
                    tiles.append(first_ref[rows(b), :])
                else:
                    tiles.append(ref[prev[b], :])
            return jnp.stack(tiles).astype(BF16)

        first_blocks = lax.broadcasted_iota(I32, (n_blocks, 1, 1), 0) < dil
        o, lse = _attend(gather(q_ref), gather(k_ref), gather(k_ref, kp_ref), gather(v_ref), gather(v_ref, vp_ref),
                         slope, dil, jnp.where(first_blocks, no_prev, 0.0))
        for b in range(n_blocks):
            acc_ref[cur[b], :] = o[b]
            lse_ref[cur[b], :] = jnp.broadcast_to(lse[b], (CHUNK, HEAD_DIM))

    ls = [r[...] for r in lse_refs]
    m = jnp.maximum(jnp.maximum(ls[0], ls[1]), ls[2])
    e = [jnp.exp(l - m) for l in ls]
    den = e[0] + e[1] + e[2]
    o_ref[...] = (acc_refs[0][...] * (e[0] / den) + acc_refs[1][...] * (e[1] / den)
                  + acc_refs[2][...] * (e[2] / den))


def _dil_attn(z, bsz, seq):
    assert seq % ATTN_BLOCK == 0
    nblk = seq // ATTN_BLOCK
    a_blocks = 0
    in_specs = [pl.BlockSpec(memory_space=pltpu.SMEM)]
    for g, (win, dil) in enumerate(DIL_GROUPS):
        span = CHUNK * dil
        per_blk = ATTN_BLOCK // span
        for which in range(3):
            cb = a_blocks + (which * N_GROUPS + g) * HEADS_PER_GROUP
            in_specs.append(pl.BlockSpec((ATTN_BLOCK, HEAD_DIM), lambda b, h, i, cb=cb: (b * nblk + i, cb + h)))
        for which in (1, 2):
            cb = a_blocks + (which * N_GROUPS + g) * HEADS_PER_GROUP
            in_specs.append(pl.BlockSpec(
                (span, HEAD_DIM),
                lambda b, h, i, cb=cb, per_blk=per_blk: (jnp.maximum((b * nblk + i) * per_blk - 1, 0), cb + h)))
    slopes = jnp.asarray(ALIBI_SLOPES, F32)
    blk = pltpu.VMEM((ATTN_BLOCK, HEAD_DIM), F32)
    return pl.pallas_call(
        _dil_attn_kernel,
        grid=(bsz, HEADS_PER_GROUP, nblk),
        in_specs=in_specs,
        out_specs=pl.BlockSpec((ATTN_BLOCK, HEAD_DIM), lambda b, h, i: (b * nblk + i, h)),
        out_shape=jax.ShapeDtypeStruct((bsz * seq, GROUP_COLS), F32),
        scratch_shapes=[blk] * (2 * N_GROUPS),
        compiler_params=_params(("arbitrary", "arbitrary", "arbitrary")),
        name="dil_attn",
    )(slopes, *([z] * (5 * N_GROUPS)))


def _sample_attn_kernel(pen_ref, q_ref, k_ref, v_ref, c0_ref, c1_ref, c2_ref, o_ref):
    caches = (c0_ref, c1_ref, c2_ref)
    scale = HEAD_DIM ** -0.5
    heads = [(g, h) for g in range(N_GROUPS) for h in range(HEADS_PER_GROUP)]

    def head_rows(ref):
        return jnp.stack([ref[0, :, i * HEAD_DIM:(i + 1) * HEAD_DIM] for i in range(N_HEADS)])

    q = head_rows(q_ref)
    k_new = head_rows(k_ref)
    v_new = head_rows(v_ref)
    k_buf = jnp.stack([caches[g][:, 0, h, :] for g, h in heads])
    v_buf = jnp.stack([caches[g][:, 1, h, :] for g, h in heads])
    q8 = jnp.broadcast_to(q, (N_HEADS, SUBLANES, HEAD_DIM))
    qk = (((2,), (2,)), ((0,), (0,)))
    pv = (((2,), (1,)), ((0,), (0,)))
    s_buf = lax.dot_general(q8, k_buf, qk, precision=HIGHEST, preferred_element_type=F32)[:, 0:1, :]
    s_buf = s_buf * scale - pen_ref[...]
    s_new = jnp.sum(q * k_new, axis=2, keepdims=True) * scale
    m = jnp.maximum(jnp.max(s_buf, axis=2, keepdims=True), s_new)
    p_buf = jnp.exp(s_buf - m)
    p_new = jnp.exp(s_new - m)
    l = jnp.sum(p_buf, axis=2, keepdims=True) + p_new
    pb8 = jnp.broadcast_to(p_buf / l, (N_HEADS, SUBLANES, CHUNK))
    o = lax.dot_general(pb8, v_buf, pv, precision=HIGHEST, preferred_element_type=F32)[:, 0:1, :]
    o = o + (p_new / l) * v_new
    lse = m + jnp.log(l)
    n = HEADS_PER_GROUP
    ls = [lse[g * n:(g + 1) * n] for g in range(N_GROUPS)]
    mm = jnp.maximum(jnp.maximum(ls[0], ls[1]), ls[2])
    e = [jnp.exp(x - mm) for x in ls]
    den = e[0] + e[1] + e[2]
    out = o[0:n] * (e[0] / den) + o[n:2 * n] * (e[1] / den) + o[2 * n:3 * n] * (e[2] / den)
    for h in range(HEADS_PER_GROUP):
        o_ref[0, :, h * HEAD_DIM:(h + 1) * HEAD_DIM] = out[h]


def _sample_attn(zs, caches):
    n, cols = zs.shape
    z3 = zs.reshape(n, 1, cols)
    a_blocks = 0
    qkv_cols = N_GROUPS * GROUP_COLS
    c_views = []
    c_specs = []
    for g, (win, dil) in enumerate(DIL_GROUPS):
        c = caches[g]
        assert c.shape[1] == win, "cache must hold exactly one window"
        c_views.append(c.reshape(n, win // dil, dil, 2, HEADS_PER_GROUP, HEAD_DIM))
        c_specs.append(pl.BlockSpec((None, win // dil, None, 2, HEADS_PER_GROUP, HEAD_DIM),
                                    lambda b: (b, 0, 0, 0, 0, 0)))
    q0 = a_blocks * GROUP_COLS
    steps = jnp.arange(CHUNK, 0, -1, dtype=F32)
    pen = jnp.stack([ALIBI_SLOPES[g * HEADS_PER_GROUP + h] * dil * steps
                     for g, (win, dil) in enumerate(DIL_GROUPS) for h in range(HEADS_PER_GROUP)])
    assert all(win // dil == CHUNK for win, dil in DIL_GROUPS)
    row_spec = pl.BlockSpec((1, 1, qkv_cols), lambda b: (b, 0, 0))
    out = pl.pallas_call(
        _sample_attn_kernel,
        grid=(n,),
        in_specs=[pl.BlockSpec((N_HEADS, 1, CHUNK), lambda b: (0, 0, 0)), row_spec, row_spec, row_spec] + c_specs,
        out_specs=pl.BlockSpec((1, 1, GROUP_COLS), lambda b: (b, 0, 0)),
        out_shape=jax.ShapeDtypeStruct((n, 1, GROUP_COLS), F32),
        compiler_params=_params(("arbitrary",)),
        name="sample_attn",
    )(pen.reshape(N_HEADS, 1, CHUNK), z3[:, :, q0:q0 + qkv_cols], z3[:, :, q0 + qkv_cols:q0 + 2 * qkv_cols],
      z3[:, :, q0 + 2 * qkv_cols:q0 + 3 * qkv_cols], *c_views)
    return out.reshape(n, GROUP_COLS)


def _layer_norm(v, g, b):
    mu = jnp.mean(v, axis=-1, keepdims=True)
    var = jnp.mean(jnp.square(v - mu), axis=-1, keepdims=True)
    return (v - mu) * lax.rsqrt(var + EPS) * g + b


def _chunk_gate_kernel(u_ref, v_ref, lg_ref, lb_ref, ws_ref, bs_ref, o_ref, *, n_chunks):
    va = _layer_norm(v_ref[...], lg_ref[...], lb_ref[...]).astype(BF16)
    row = lax.broadcasted_iota(I32, (CHUNK, CHUNK), 0)
    col = lax.broadcasted_iota(I32, (CHUNK, CHUNK), 1)
    for g in range(A_GROUPS):
        w = jnp.where(col <= row, ws_ref[g], 0.0).astype(BF16)
        gs = slice(g * CHUNK, (g + 1) * CHUNK)
        for c in range(n_chunks):
            rs = slice(c * CHUNK, (c + 1) * CHUNK)
            mixed = jnp.dot(w, va[rs, gs], preferred_element_type=F32) + bs_ref[g]
            o_ref[rs, gs] = (u_ref[rs, gs] * mixed).astype(o_ref.dtype)


def _chunk_gate(z, ln_g, ln_b, w_s, b_s, *, tm):
    t = z.shape[0]
    aw = A_GROUPS * CHUNK
    bs_full = jnp.broadcast_to(b_s[:, :, None], (A_GROUPS, CHUNK, CHUNK))
    kern = functools.partial(_chunk_gate_kernel, n_chunks=tm // CHUNK)
    return pl.pallas_call(
        kern,
        grid=(t // tm,),
        in_specs=[pl.BlockSpec((tm, aw), lambda i: (i, 0)),
                  pl.BlockSpec((tm, aw), lambda i: (i, 1)),
                  pl.BlockSpec((1, aw), lambda i: (0, 0)),
                  pl.BlockSpec((1, aw), lambda i: (0, 0)),
                  pl.BlockSpec((A_GROUPS, CHUNK, CHUNK), lambda i: (0, 0, 0)),
                  pl.BlockSpec((A_GROUPS, CHUNK, CHUNK), lambda i: (0, 0, 0))],
        out_specs=pl.BlockSpec((tm, aw), lambda i: (i, 0)),
        out_shape=jax.ShapeDtypeStruct((t, aw), BF16),
        compiler_params=_params(("arbitrary",)),
        name="chunk_gate",
    )(z, z, ln_g.reshape(1, aw), ln_b.reshape(1, aw), w_s, bs_full)


def _sample_gate_kernel(u_ref, v_ref, lg_ref, lb_ref, w0_ref, b0_ref, a_ref, vn_ref):
    va = _layer_norm(v_ref[...], lg_ref[...], lb_ref[...])
    vn_ref[...] = va
    a_ref[...] = u_ref[...] * (w0_ref[...] * va + b0_ref[...])


def _sample_gate(zs, ln_g, ln_b, w_s, b_s):
    n = zs.shape[0]
    aw = A_GROUPS * CHUNK
    w0 = jnp.repeat(w_s[:, 0, 0], CHUNK).reshape(1, aw)
    b0 = jnp.repeat(b_s[:, 0], CHUNK).reshape(1, aw)
    vec = pl.BlockSpec((1, aw), lambda i: (0, 0))
    return pl.pallas_call(
        _sample_gate_kernel,
        grid=(1,),
        in_specs=[pl.BlockSpec((n, aw), lambda i: (0, 0)), pl.BlockSpec((n, aw), lambda i: (0, 1)),
                  vec, vec, vec, vec],
        out_specs=[pl.BlockSpec((n, aw), lambda i: (0, 0)), pl.BlockSpec((n, aw), lambda i: (0, 0))],
        out_shape=[jax.ShapeDtypeStruct((n, aw), F32), jax.ShapeDtypeStruct((n, aw), F32)],
        compiler_params=_params(("arbitrary",)),
        name="sample_gate",
    )(zs, zs, ln_g.reshape(1, aw), ln_b.reshape(1, aw), w0, b0)


def _merge_kernel(a_ref, b_ref, ga_ref, gb_ref, pa_ref, pb_ref, out_ref, *, hi):
    ya = _dot(a_ref[...], pa_ref[...], hi)
    yb = _dot(b_ref[...], pb_ref[...], hi)
    out_ref[...] = (ga_ref[...] * ya + gb_ref[...] * yb).astype(out_ref.dtype)


def _merge(a_out, b_out, z, p_a, p_b, *, tm, hi):
    t, aw = a_out.shape
    d = p_a.shape[1]
    tn = COL_TILE
    ga_blk0 = 0
    gb_blk0 = d // tn
    return pl.pallas_call(
        functools.partial(_merge_kernel, hi=hi),
        grid=(t // tm, d // tn),
        in_specs=[pl.BlockSpec((tm, aw), lambda i, j: (i, 0)),
                  pl.BlockSpec((tm, GROUP_COLS), lambda i, j: (i, 0)),
                  pl.BlockSpec((tm, tn), lambda i, j: (i, ga_blk0 + j)),
                  pl.BlockSpec((tm, tn), lambda i, j: (i, gb_blk0 + j)),
                  pl.BlockSpec((aw, tn), lambda i, j: (0, j)),
                  pl.BlockSpec((GROUP_COLS, tn), lambda i, j: (0, j))],
        out_specs=pl.BlockSpec((tm, tn), lambda i, j: (i, j)),
        out_shape=jax.ShapeDtypeStruct((t, d), F32 if hi else BF16),
        compiler_params=_params(("arbitrary", "arbitrary")),
        name="merge",
    )(a_out, b_out, z, z, p_a, p_b)


def _out_proj_kernel(m_ref, w_ref, x_ref, gt_ref, o_ref, *, hi):
    o_ref[...] = x_ref[...] + gt_ref[0] * _dot(m_ref[...], w_ref[...], hi)


def _out_proj(merged, w_o, x, gt, *, rows_per_mod, tm, hi):
    t, d = x.shape
    tn = COL_TILE
    mod_rows = gt.shape[1]
    return pl.pallas_call(
        functools.partial(_out_proj_kernel, hi=hi),
        grid=(t // tm, d // tn),
        in_specs=[pl.BlockSpec((tm, d), lambda i, j: (i, 0)),
                  pl.BlockSpec((d, tn), lambda i, j: (0, j)),
                  pl.BlockSpec((tm, tn), lambda i, j: (i, j)),
                  pl.BlockSpec((1, mod_rows, tn), lambda i, j: (i * tm // rows_per_mod, 0, j))],
        out_specs=pl.BlockSpec((tm, tn), lambda i, j: (i, j)),
        out_shape=jax.ShapeDtypeStruct((t, d), F32),
        compiler_params=_params(("arbitrary", "arbitrary")),
        name="out_proj",
    )(merged, w_o, x, gt)


def _pack_pair(lo, hi, pair_ref):
    n = lo.shape[0]
    pair_ref[pl.ds(0, n, stride=2), :] = lo
    pair_ref[pl.ds(1, n, stride=2), :] = hi
    return pltpu.bitcast(pair_ref[0:2 * n, :].astype(BF16), U32)


def _unpack_pair(w, pair_ref):
    n = w.shape[0]
    pair_ref[0:2 * n, :] = pltpu.bitcast(w, BF16).astype(F32)
    return pair_ref[pl.ds(0, n, stride=2), :].astype(BF16), pair_ref[pl.ds(1, n, stride=2), :].astype(BF16)


def _ffn_norm_kernel(x_ref, sc_ref, sh_ref, g_ref, wr_ref, br_ref, hp_ref, e_ref, w_ref, pair_ref):
    h = _mod_rms(x_ref[...], g_ref[...], sc_ref[0], sh_ref[0])
    tm, d = h.shape
    for s in range(d // (2 * LANES)):
        lo = h[:, (2 * s) * LANES:(2 * s + 1) * LANES]
        hi = h[:, (2 * s + 1) * LANES:(2 * s + 2) * LANES]
        hp_ref[pl.ds(s, tm, stride=SUBLANES), :] = _pack_pair(lo, hi, pair_ref)
    logits = _dot(h, wr_ref[...], True) + br_ref[...]
    lane = lax.broadcasted_iota(I32, logits.shape, 1)
    lane_f = lane.astype(F32)
    vals, idxs = [], []
    for _ in range(TOP_K):
        m = jnp.max(logits, axis=1, keepdims=True)
        idx = jnp.min(jnp.where(logits == m, lane_f, float(logits.shape[1])), axis=1, keepdims=True)
        vals.append(m)
        idxs.append(idx)
        logits = jnp.where(lane_f == idx, 2.0 * NEG, logits)
    es = [jnp.exp(v - vals[0]) for v in vals]
    den = es[0] + es[1] + es[2] + es[3]
    e_out = jnp.full(lane.shape, -1.0, F32)
    w_out = jnp.zeros(lane.shape, F32)
    for k in range(TOP_K):
        e_out = jnp.where(lane == k, idxs[k], e_out)
        w_out = jnp.where(lane == k, es[k] / den, w_out)
    e_ref[...] = e_out.astype(I32)
    w_ref[...] = w_out


def _ffn_norm(x1, sc, sh, g, w_router, b_router, *, rows_per_mod, tm):
    t, d = x1.shape
    assert d == 2 * LANES * SUBLANES, "one packed row must be exactly one (8, 128) tile"
    ne = w_router.shape[1]
    wr = jnp.zeros((d, LANES), F32).at[:, :ne].set(w_router)
    br = jnp.full((1, LANES), NEG, F32).at[0, :ne].set(b_router)
    mod_rows = sc.shape[1]
    mod_spec = pl.BlockSpec((1, mod_rows, d), lambda i: (i * tm // rows_per_mod, 0, 0))
    return pl.pallas_call(
        _ffn_norm_kernel,
        grid=(t // tm,),
        in_specs=[pl.BlockSpec((tm, d), lambda i: (i, 0)), mod_spec, mod_spec,
                  pl.BlockSpec((1, d), lambda i: (0, 0)),
                  pl.BlockSpec((d, LANES), lambda i: (0, 0)),
                  pl.BlockSpec((1, LANES), lambda i: (0, 0))],
        out_specs=[pl.BlockSpec((tm * SUBLANES, LANES), lambda i: (i, 0)),
                   pl.BlockSpec((tm, LANES), lambda i: (i, 0)),
                   pl.BlockSpec((tm, LANES), lambda i: (i, 0))],
        out_shape=[jax.ShapeDtypeStruct((t * SUBLANES, LANES), U32),
                   jax.ShapeDtypeStruct((t, LANES), I32),
                   jax.ShapeDtypeStruct((t, LANES), F32)],
        scratch_shapes=[pltpu.VMEM((2 * tm, LANES), F32)],
        compiler_params=_params(("arbitrary",)),
        name="ffn_norm",
    )(x1, sc, sh, g.reshape(1, d), wr, br)


def _route_kernel(e_ref, pos_ref, cnt_ref, tri_ref, run_ref):
    phase = pl.program_id(0)
    i = pl.program_id(1)
    e = e_ref[...]
    tm = e.shape[0]
    lane = lax.broadcasted_iota(I32, (tm, LANES), 1)
    hits = [lane == e[:, k:k + 1] for k in range(TOP_K)]
    chosen = jnp.zeros((tm, LANES), F32)
    for k in range(TOP_K):
        chosen = jnp.where(hits[k], 1.0, chosen)
    col_count = jnp.sum(chosen, axis=0, keepdims=True)

    @pl.when((phase == 0) & (i == 0))
    def _():
        cnt_ref[...] = jnp.zeros(cnt_ref.shape, F32)
        r = lax.broadcasted_iota(I32, (tm, tm), 0)
        c = lax.broadcasted_iota(I32, (tm, tm), 1)
        tri_ref[...] = jnp.where(c < r, 1.0, 0.0).astype(BF16)

    @pl.when(phase == 0)
    def _():
        cnt_ref[...] = cnt_ref[...] + col_count

    @pl.when((phase == 1) & (i == 0))
    def _():
        tiles = jnp.floor((cnt_ref[...] + (MOE_TILE - 1)) * (1.0 / MOE_TILE))
        r = lax.broadcasted_iota(I32, (LANES, LANES), 0)
        c = lax.broadcasted_iota(I32, (LANES, LANES), 1)
        below = jnp.where(r < c, 1.0, 0.0)
        tiles8 = jnp.broadcast_to(tiles, (SUBLANES, LANES))
        run_ref[...] = _dot(tiles8, below, True)[0:1] * float(MOE_TILE)

    @pl.when(phase == 1)
    def _():
        before = jnp.dot(tri_ref[...], chosen.astype(BF16), preferred_element_type=F32)
        dest = run_ref[...] + before
        out = jnp.zeros((tm, LANES), F32)
        for k in range(TOP_K):
            p = jnp.sum(jnp.where(hits[k], dest, 0.0), axis=1, keepdims=True)
            out = jnp.where(lane == k, p, out)
        pos_ref[...] = out.astype(I32)
        run_ref[...] = run_ref[...] + col_count


def _route(e_all):
    t = e_all.shape[0]
    steps = t // ROUTE_TILE
    return pl.pallas_call(
        _route_kernel,
        grid=(2, steps),
        in_specs=[pl.BlockSpec((ROUTE_TILE, LANES), lambda p, i: (i, 0))],
        out_specs=[pl.BlockSpec((ROUTE_TILE, LANES), lambda p, i: (i * p, 0)),
                   pl.BlockSpec((1, LANES), lambda p, i: (0, 0))],
        out_shape=[jax.ShapeDtypeStruct((t, LANES), I32), jax.ShapeDtypeStruct((1, LANES), F32)],
        scratch_shapes=[pltpu.VMEM((ROUTE_TILE, ROUTE_TILE), BF16), pltpu.VMEM((1, LANES), F32)],
        compiler_params=_params(("arbitrary", "arbitrary")),
        name="route",
    )(e_all)


def _row_scatter_kernel(pos_ref, pos_s_ref, pad_base_ref, pad_cnt_ref, hp_ref, hs_ref, xs_hbm, sem, *,
                        n_prompt_steps, n_sample):
    i = pl.program_id(0)

    def rows(first, n=1):
        return pl.ds(pl.multiple_of(first * SUBLANES, SUBLANES), n * SUBLANES)

    def scatter(src_ref, idx_ref, n_tok):
        def body(tt, carry):
            for k in range(TOP_K):
                pltpu.make_async_copy(src_ref.at[rows(tt)], xs_hbm.at[rows(idx_ref[0, 0, tt * TOP_K + k])],
                                      sem).start()
            return carry
        lax.fori_loop(0, n_tok, body, 0, unroll=2)
        for _ in range(TOP_K):
            pltpu.make_async_copy(src_ref.at[rows(0, n_tok)], xs_hbm.at[rows(0, n_tok)], sem).wait()

    @pl.when(i < n_prompt_steps)
    def _():
        scatter(hp_ref, pos_ref, ROUTE_TILE)

    @pl.when(i == n_prompt_steps)
    def _():
        scatter(hs_ref, pos_s_ref, n_sample)

        def fill(dst_row, size):
            return pltpu.make_async_copy(hp_ref.at[rows(0, size)], xs_hbm.at[rows(dst_row, size)], sem)

        def per_segment(wait):
            def body(e, carry):
                cnt = pad_cnt_ref[e]
                base = pad_base_ref[e]

                def whole(r, c):
                    cp = fill(base + (cnt % MOE_TILE) + r * MOE_TILE, MOE_TILE)
                    cp.wait() if wait else cp.start()
                    return c
                lax.fori_loop(0, cnt // MOE_TILE, whole, 0)
                size = MOE_TILE // 2
                while size >= 1:
                    @pl.when((cnt & size) != 0)
                    def _(size=size):
                        cp = fill(base + (cnt & (size - 1)), size)
                        cp.wait() if wait else cp.start()
                    size //= 2
                return carry
            return body
        lax.fori_loop(0, pad_cnt_ref.shape[0], per_segment(False), 0)
        lax.fori_loop(0, pad_cnt_ref.shape[0], per_segment(True), 0)


def _row_scatter(hp_p, hp_s, pos_p, pos_s, pad_base, pad_cnt, n_rows):
    t_p = hp_p.shape[0] // SUBLANES
    n_s = hp_s.shape[0] // SUBLANES
    assert ROUTE_TILE >= MOE_TILE, "padding blocks are copied from one token tile"
    steps = t_p // ROUTE_TILE
    tile_rows = ROUTE_TILE * TOP_K
    return pl.pallas_call(
        functools.partial(_row_scatter_kernel, n_prompt_steps=steps, n_sample=n_s),
        grid=(steps + 1,),
        in_specs=[pl.BlockSpec((1, 1, tile_rows), lambda i: (jnp.minimum(i, steps - 1), 0, 0),
                               memory_space=pltpu.SMEM),
                  pl.BlockSpec((1, 1, n_s * TOP_K), lambda i: (0, 0, 0), memory_space=pltpu.SMEM),
                  pl.BlockSpec(memory_space=pltpu.SMEM),
                  pl.BlockSpec(memory_space=pltpu.SMEM),
                  pl.BlockSpec((ROUTE_TILE * SUBLANES, LANES), lambda i: (jnp.minimum(i, steps - 1), 0)),
                  pl.BlockSpec((n_s * SUBLANES, LANES), lambda i: (0, 0))],
        out_specs=pl.BlockSpec(memory_space=pl.ANY),
        out_shape=jax.ShapeDtypeStruct((n_rows * SUBLANES, LANES), U32),
        scratch_shapes=[pltpu.SemaphoreType.DMA(())],
        compiler_params=_params(("arbitrary",)),
        name="row_scatter",
    )(pos_p.reshape(steps, 1, tile_rows), pos_s.reshape(1, 1, n_s * TOP_K), pad_base, pad_cnt, hp_p, hp_s)


def _expert_weights(seg_ref, sege_ref, nu_ref, fetch, on_ready):
    j = pl.program_id(0)
    t = pl.program_id(1)
    n_j = pl.num_programs(0)
    seg = seg_ref[t]
    n_seg = nu_ref[1]
    first = (t < nu_ref[0]) & ((t == 0) | (seg != seg_ref[jnp.maximum(t - 1, 0)]))
    k = j * n_seg + seg
    slot = lax.rem(k, 2)

    @pl.when(first)
    def _():
        @pl.when(k == 0)
        def _():
            for cp in fetch(sege_ref[0], 0, 0):
                cp.start()

        more = seg + 1 < n_seg

        @pl.when(more | (j + 1 < n_j))
        def _():
            for cp in fetch(sege_ref[jnp.where(more, seg + 1, 0)], jnp.where(more, j, j + 1), 1 - slot):
                cp.start()

        for cp in fetch(sege_ref[seg], j, slot):
            cp.wait()
        on_ready(slot)


def _moe_up_kernel(te_ref, nv_ref, seg_ref, sege_ref, nu_ref, x_ref, wg_hbm, wu_hbm, bg_ref, bu_ref, h_ref,
                   wg_buf, wu_buf, sem, wg_s, wu_s, xb_s, pair_ref):
    t = pl.program_id(1)
    tf = wg_s.shape[1]

    def fetch(e, j, slot):
        cols = pl.ds(pl.multiple_of(j * tf, tf), tf)
        return [pltpu.make_async_copy(wg_hbm.at[e, :, cols], wg_buf.at[slot], sem.at[slot]),
                pltpu.make_async_copy(wu_hbm.at[e, :, cols], wu_buf.at[slot], sem.at[slot])]

    def on_ready(slot):
        wg_s[...] = wg_buf[slot].astype(BF16)
        wu_s[...] = wu_buf[slot].astype(BF16)

    _expert_weights(seg_ref, sege_ref, nu_ref, fetch, on_ready)

    def compute(rows):
        for s in range(SUBLANES):
            lo, hi = _unpack_pair(x_ref[pl.ds(s, rows, stride=SUBLANES), :], pair_ref)
            xb_s[0:rows, (2 * s) * LANES:(2 * s + 1) * LANES] = lo
            xb_s[0:rows, (2 * s + 1) * LANES:(2 * s + 2) * LANES] = hi
        x = xb_s[0:rows, :]
        gt = jnp.dot(x, wg_s[...], preferred_element_type=F32) + bg_ref[0]
        up = jnp.dot(x, wu_s[...], preferred_element_type=F32) + bu_ref[0]
        gt = jnp.minimum(gt, SWIGLU_LIMIT)
        up = jnp.clip(up, -SWIGLU_LIMIT, SWIGLU_LIMIT)
        h_ref[0:rows, :] = ((up + 1.0) * gt * jax.nn.sigmoid(SWIGLU_ALPHA * gt)).astype(h_ref.dtype)

    active = t < nu_ref[0]
    full = nv_ref[t] > MOE_HALF

    @pl.when(active & full)
    def _():
        compute(MOE_TILE)

    @pl.when(active & jnp.logical_not(full))
    def _():
        compute(MOE_HALF)
        h_ref[MOE_HALF:, :] = jnp.zeros((MOE_TILE - MOE_HALF, h_ref.shape[1]), h_ref.dtype)

    @pl.when(jnp.logical_not(active))
    def _():
        h_ref[...] = jnp.zeros(h_ref.shape, h_ref.dtype)


def _moe_down_kernel(te_ref, nv_ref, seg_ref, sege_ref, nu_ref, h_ref, wd_hbm, bd_ref, y_ref, wd_buf, sem, wd_s):
    t = pl.program_id(1)
    n_sub = y_ref.shape[0] // MOE_TILE
    tn = wd_s.shape[1]

    def fetch(e, j, slot):
        cols = pl.ds(pl.multiple_of(j * tn, tn), tn)
        return [pltpu.make_async_copy(wd_hbm.at[e, :, cols], wd_buf.at[slot], sem.at[slot])]

    def on_ready(slot):
        wd_s[...] = wd_buf[slot].astype(BF16)

    _expert_weights(seg_ref, sege_ref, nu_ref, fetch, on_ready)

    def store(rows, y):
        for c in range(n_sub):
            y_ref[pl.ds(c, rows, stride=n_sub), :] = y[:, c * LANES:(c + 1) * LANES]

    def compute(rows):
        store(rows, jnp.dot(h_ref[0:rows, :], wd_s[...], preferred_element_type=F32) + bd_ref[0])

    active = t < nu_ref[0]
    full = nv_ref[t] > MOE_HALF

    @pl.when(active & full)
    def _():
        compute(MOE_TILE)

    @pl.when(active & jnp.logical_not(full))
    def _():
        compute(MOE_HALF)
        y_ref[MOE_HALF * n_sub:, :] = jnp.zeros(((MOE_TILE - MOE_HALF) * n_sub, LANES), F32)

    @pl.when(jnp.logical_not(active))
    def _():
        y_ref[...] = jnp.zeros(y_ref.shape, F32)


def _moe_experts(x_sorted, plan, w_gate, b_gate, w_up, b_up, w_down, b_down):
    rows = x_sorted.shape[0] // SUBLANES
    ne, d, f = w_gate.shape
    nt = rows // MOE_TILE
    tf = 1024
    tn = d
    n_sub = d // LANES
    n_plan = len(plan)

    def row_map(j, t, te, nv, seg, sege, nu):
        return (jnp.minimum(t, nu[0] - 1), 0)

    def b_map(j, t, te, nv, seg, sege, nu):
        return (te[t], 0, j)

    hbm = pl.BlockSpec(memory_space=pl.ANY)
    h = pl.pallas_call(
        _moe_up_kernel,
        grid_spec=pltpu.PrefetchScalarGridSpec(
            num_scalar_prefetch=n_plan,
            grid=(f // tf, nt),
            in_specs=[pl.BlockSpec((MOE_TILE * SUBLANES, LANES), row_map), hbm, hbm,
                      pl.BlockSpec((1, 1, tf), b_map),
                      pl.BlockSpec((1, 1, tf), b_map)],
            out_specs=pl.BlockSpec((MOE_TILE, tf), lambda j, t, *_: (t, j)),
            scratch_shapes=[pltpu.VMEM((2, d, tf), F32), pltpu.VMEM((2, d, tf), F32), pltpu.SemaphoreType.DMA((2,)),
                            pltpu.VMEM((d, tf), BF16), pltpu.VMEM((d, tf), BF16), pltpu.VMEM((MOE_TILE, d), BF16),
                            pltpu.VMEM((2 * MOE_TILE, LANES), F32)]),
        out_shape=jax.ShapeDtypeStruct((rows, f), BF16),
        compiler_params=_params(("arbitrary", "arbitrary")),
        name="moe_up",
    )(*plan, x_sorted, w_gate, w_up, b_gate.reshape(ne, 1, f), b_up.reshape(ne, 1, f))

    y = pl.pallas_call(
        _moe_down_kernel,
        grid_spec=pltpu.PrefetchScalarGridSpec(
            num_scalar_prefetch=n_plan,
            grid=(d // tn, nt),
            in_specs=[pl.BlockSpec((MOE_TILE, f), row_map), hbm,
                      pl.BlockSpec((1, 1, tn), b_map)],
            out_specs=pl.BlockSpec((MOE_TILE * n_sub, LANES), lambda j, t, *_: (t, 0)),
            scratch_shapes=[pltpu.VMEM((2, f, tn), F32), pltpu.SemaphoreType.DMA((2,)), pltpu.VMEM((f, tn), BF16)]),
        out_shape=jax.ShapeDtypeStruct((rows * n_sub, LANES), F32),
        compiler_params=_params(("arbitrary", "arbitrary")),
        name="moe_down",
    )(*plan, h, w_down, b_down.reshape(ne, 1, d))
    return y.reshape(rows, d // LANES, LANES)


def _row_copy(src_hbm, buf, sem, slot, src_row, dst_row, n_sub):
    dst = pl.ds(pl.multiple_of(dst_row * n_sub, SUBLANES), n_sub)
    return pltpu.make_async_copy(src_hbm.at[src_row], buf.at[slot, dst], sem.at[slot])


def _issue_rows(idx_ref, src_hbm, buf, sem, slot, n_rows, n_sub):
    def body(r, carry):
        _row_copy(src_hbm, buf, sem, slot, idx_ref[0, 0, r], r, n_sub).start()
        return carry
    lax.fori_loop(0, n_rows, body, 0, unroll=8)


def _moe_combine_kernel(idx_ref, nxt_ref, y_hbm, w_ref, x_ref, gt_ref, g_ref, o_ref, buf, sem, *, tok):
    i = pl.program_id(0)
    n = pl.num_programs(0)
    n_rows = TOP_K * tok
    n_sub = y_hbm.shape[1]
    slot = lax.rem(i, 2)

    @pl.when(i == 0)
    def _():
        _issue_rows(idx_ref, y_hbm, buf, sem, 0, n_rows, n_sub)

    for s in range(2):
        @pl.when((i + 1 < n) & (slot == s))
        def _():
            _issue_rows(nxt_ref, y_hbm, buf, sem, 1 - s, n_rows, n_sub)

    pltpu.make_async_copy(buf.at[slot], buf.at[slot], sem.at[slot]).wait()

    w = w_ref[...]
    sumsq = jnp.zeros((tok, 1), F32)
    for c in range(n_sub):
        cs = slice(c * LANES, (c + 1) * LANES)
        acc = jnp.zeros((tok, LANES), F32)
        for k in range(TOP_K):
            part = buf[slot, pl.ds(k * tok * n_sub + c, tok, stride=n_sub), :]
            acc = acc + w[:, k:k + 1] * part
        x = x_ref[:, cs] + gt_ref[0, :, cs] * acc
        o_ref[:, cs] = x
        sumsq = sumsq + jnp.sum(x * x, axis=1, keepdims=True)
    o_ref[...] = o_ref[...] * lax.rsqrt(sumsq / (n_sub * LANES) + EPS) * g_ref[...]


def _moe_combine(y_rows, pos, top_w, x1, gt, g_final, *, rows_per_mod, tok):
    t, d = x1.shape
    n_sub = y_rows.shape[1]
    steps = t // tok
    n_rows = TOP_K * tok
    idx3 = pos[:, :TOP_K].reshape(steps, tok, TOP_K).transpose(0, 2, 1).reshape(steps, 1, n_rows)
    lanes = top_w.shape[1]
    mod_rows = gt.shape[1]
    idx_blk = (1, 1, n_rows)
    return pl.pallas_call(
        functools.partial(_moe_combine_kernel, tok=tok),
        grid=(steps,),
        in_specs=[pl.BlockSpec(idx_blk, lambda i: (i, 0, 0), memory_space=pltpu.SMEM),
                  pl.BlockSpec(idx_blk, lambda i: (jnp.minimum(i + 1, steps - 1), 0, 0), memory_space=pltpu.SMEM),
                  pl.BlockSpec(memory_space=pl.ANY),
                  pl.BlockSpec((tok, lanes), lambda i: (i, 0)),
                  pl.BlockSpec((tok, d), lambda i: (i, 0)),
                  pl.BlockSpec((1, mod_rows, d), lambda i: (i * tok // rows_per_mod, 0, 0)),
                  pl.BlockSpec((1, d), lambda i: (0, 0))],
        out_specs=pl.BlockSpec((tok, d), lambda i: (i, 0)),
        out_shape=jax.ShapeDtypeStruct((t, d), F32),
        scratch_shapes=[pltpu.VMEM((2, n_rows * n_sub, LANES), F32), pltpu.SemaphoreType.DMA((2,))],
        compiler_params=_params(("arbitrary",)),
        name="moe_combine",
    )(idx3, idx3, y_rows, top_w, x1, gt, g_final.reshape(1, d))


def _tile_table(counts, n_tiles):
    tiles_e = (counts + MOE_TILE - 1) // MOE_TILE
    tile_end = jnp.cumsum(tiles_e)
    tile_start = tile_end - tiles_e
    n_used = tile_end[-1]
    tile_ids = jnp.minimum(jnp.arange(n_tiles, dtype=I32), n_used - 1)
    tile_expert = jnp.minimum(jnp.sum(tile_end[None, :] <= tile_ids[:, None], axis=1), N_EXPERTS - 1).astype(I32)
    tile_valid = jnp.clip(counts[tile_expert] - (tile_ids - tile_start[tile_expert]) * MOE_TILE, 0, MOE_TILE)
    nonempty = tiles_e > 0
    seg_of_expert = jnp.cumsum(nonempty.astype(I32)) - 1
    n_seg = jnp.sum(nonempty.astype(I32))
    experts = jnp.arange(N_EXPERTS, dtype=I32)
    hit = nonempty[None, :] & (seg_of_expert[None, :] == jnp.arange(N_EXPERTS + 1, dtype=I32)[:, None])
    seg_expert = jnp.sum(jnp.where(hit, experts[None, :], 0), axis=1).astype(I32)
    tile_seg = seg_of_expert[tile_expert].astype(I32)
    plan = (tile_expert, tile_valid.astype(I32), tile_seg, seg_expert, jnp.stack([n_used, n_seg]).astype(I32))
    pad_base = jnp.concatenate([tile_start * MOE_TILE + counts, (n_used * MOE_TILE).reshape(1)])
    pad_cnt = jnp.concatenate([tiles_e * MOE_TILE - counts, ((n_tiles - n_used) * MOE_TILE).reshape(1)])
    return plan, pad_base.astype(I32), pad_cnt.astype(I32)


def _row_tile(t, want):
    tm = min(t, want)
    assert t % tm == 0
    return tm


def kernel(x_prompt, x_sample, cache_kv_g0, cache_kv_g1, cache_kv_g2, c_prompt, c_sample, w_ada, b_ada, g_mix, w_in, ln_g, ln_b, w_s, b_s, p_a, p_b, w_o, g_ffn, w_router, b_router, w_gate, b_gate, w_up, b_up, w_down, b_down, g_final):
    depth = w_ada.shape[0]
    assert depth == 1, "single-layer trunk"
    bsz, seq, d = x_prompt.shape
    n_s, dec_seq, _ = x_sample.shape
    assert dec_seq == 1, "one new position per sample"
    caches = (cache_kv_g0, cache_kv_g1, cache_kv_g2)
    l = 0
    cols = w_in.shape[2]
    t_p = bsz * seq
    aw = A_GROUPS * CHUNK

    n_c = bsz + n_s
    n_c_pad = -(-n_c // SUBLANES) * SUBLANES
    c_all = jnp.concatenate([c_prompt, c_sample, jnp.zeros((n_c_pad - n_c, d), F32)], axis=0)
    mod = _ada(c_all, w_ada[l], b_ada[l])
    mod_p = mod[:bsz].reshape(bsz, 1, N_ADA, d)
    mod_s = mod[bsz:n_c].reshape(1, n_s, N_ADA, d)
    sh1_p, sc1_p, gt1_p, sh2_p, sc2_p, gt2_p = (mod_p[:, :, k] for k in range(N_ADA))
    sh1_s, sc1_s, gt1_s, sh2_s, sc2_s, gt2_s = (mod_s[:, :, k] for k in range(N_ADA))

    xp = x_prompt.reshape(t_p, d)
    xs = x_sample.reshape(n_s, d)

    tm_big = _row_tile(seq, 1024)
    tm_mid = _row_tile(seq, 512)
    tm_huge = _row_tile(seq, 2048)
    za_p, zqkv_p, zg_p = _in_proj_all(xp, sc1_p, sh1_p, g_mix[l], w_in[l], rows_per_mod=seq, tm_norm=tm_mid,
                                      tm=tm_huge, hi=False)
    b_p = _dil_attn(zqkv_p, bsz, seq)
    a_p = _chunk_gate(za_p, ln_g[l], ln_b[l], w_s[l], b_s[l], tm=tm_mid)
    merged_p = _merge(a_p, b_p, zg_p, p_a[l], p_b[l], tm=tm_big, hi=False)
    x1_p = _out_proj(merged_p, w_o[l], xp, gt1_p, rows_per_mod=seq, tm=tm_huge, hi=False)
    hp_p, e_p, w_p = _ffn_norm(x1_p, sc2_p, sh2_p, g_ffn[l], w_router[l], b_router[l], rows_per_mod=seq, tm=tm_mid)

    za_s, zqkv_s, zg_s = _in_proj_all(xs, sc1_s, sh1_s, g_mix[l], w_in[l], rows_per_mod=n_s, tm_norm=n_s, tm=n_s,
                                      hi=True)
    b_s_out = _sample_attn(zqkv_s, tuple(c[l] for c in caches))
    a_s, vn_s = _sample_gate(za_s, ln_g[l], ln_b[l], w_s[l], b_s[l])
    merged_s = _merge(a_s, b_s_out, zg_s, p_a[l], p_b[l], tm=n_s, hi=True)
    x1_s = _out_proj(merged_s, w_o[l], xs, gt1_s, rows_per_mod=n_s, tm=n_s, hi=True)
    hp_s, e_s, w_s_top = _ffn_norm(x1_s, sc2_s, sh2_s, g_ffn[l], w_router[l], b_router[l], rows_per_mod=n_s, tm=n_s)

    assert t_p % ROUTE_TILE == 0
    t_all = t_p + n_s
    t_pad = -(-t_all // ROUTE_TILE) * ROUTE_TILE
    e_all = jnp.concatenate([e_p, e_s, jnp.full((t_pad - t_all, LANES), -1, I32)], axis=0)
    pos, cnt = _route(e_all)
    counts = cnt[0, :N_EXPERTS].astype(I32)
    n_tiles = (t_all * TOP_K + N_EXPERTS * (MOE_TILE - 1)) // MOE_TILE
    plan, pad_base, pad_cnt = _tile_table(counts, n_tiles)
    pos_p = pos[:t_p, :TOP_K]
    pos_s = pos[t_p:t_all, :TOP_K]
    x_sorted = _row_scatter(hp_p, hp_s, pos_p.reshape(-1), pos_s.reshape(-1), pad_base, pad_cnt,
                            n_tiles * MOE_TILE)
    y_rows = _moe_experts(x_sorted, plan, w_gate[l], b_gate[l], w_up[l], b_up[l], w_down[l], b_down[l])
    y_p = _moe_combine(y_rows, pos_p, w_p, x1_p, gt2_p, g_final, rows_per_mod=seq, tok=_row_tile(seq, 128))
    y_s = _moe_combine(y_rows, pos_s, w_s_top, x1_s, gt2_s, g_final, rows_per_mod=n_s, tok=n_s)

    k0 = N_GROUPS * GROUP_COLS
    v0 = k0 + N_GROUPS * GROUP_COLS
    z_p3 = zqkv_p.reshape(bsz, seq, 3 * N_GROUPS * GROUP_COLS)
    z_s = zqkv_s
    kv_prompt, kv_sample = [], []
    for g, (win, dil) in enumerate(DIL_GROUPS):
        keep = min(win, seq)
        kc = slice(k0 + g * GROUP_COLS, k0 + (g + 1) * GROUP_COLS)
        vc = slice(v0 + g * GROUP_COLS, v0 + (g + 1) * GROUP_COLS)
        kv = jnp.stack([z_p3[:, seq - keep:, kc], z_p3[:, seq - keep:, vc]], axis=2)
        kv_prompt.append(kv.reshape(1, bsz, keep, 2, HEADS_PER_GROUP, HEAD_DIM))
        kvs = jnp.stack([z_s[:, kc], z_s[:, vc]], axis=1)
        kv_sample.append(kvs.reshape(1, n_s, 1, 2, HEADS_PER_GROUP, HEAD_DIM))
    return (y_p.reshape(bsz, seq, d), y_s.reshape(n_s, 1, d),
            kv_prompt[0], kv_prompt[1], kv_prompt[2],
            kv_sample[0], kv_sample[1], kv_sample[2],
            vn_s.reshape(1, n_s, 1, aw))
```

```python
import functools

import jax
import jax.numpy as jnp
from jax import lax
from jax.experimental import pallas as pl
from jax.experimental.pallas import tpu as pltpu

F32 = jnp.float32
BF16 = jnp.bfloat16
U32 = jnp.uint32
I32 = jnp.int32
HIGHEST = lax.Precision.HIGHEST

EPS = 1e-6
A_GROUPS = 8
CHUNK = 128
HEAD_DIM = 128
HEADS_PER_GROUP = 4
DIL_GROUPS = ((128, 1), (512, 4), (2048, 16))
N_GROUPS = len(DIL_GROUPS)
N_HEADS = HEADS_PER_GROUP * N_GROUPS
ALIBI_SLOPES = tuple(2.0 ** (-8.0 * (h + 1) / N_HEADS) for h in range(N_HEADS))
GROUP_COLS = HEADS_PER_GROUP * HEAD_DIM
N_EXPERTS = 32
TOP_K = 4
SWIGLU_LIMIT = 7.0
SWIGLU_ALPHA = 1.702
N_ADA = 6
NEG = -1e30

LANES = 128
SUBLANES = 8
COL_TILE = 512
ATTN_BLOCK = 2048
MOE_TILE = 512
MOE_QUARTER = MOE_TILE // 4
ROUTE_TILE = 512
VMEM_LIMIT = 56 * 1024 * 1024


def _params(sem, vmem=VMEM_LIMIT):
    return pltpu.CompilerParams(dimension_semantics=sem, vmem_limit_bytes=vmem)


def _dot(a, b, hi):
    if hi:
        return jnp.dot(a, b, precision=HIGHEST, preferred_element_type=F32)
    return jnp.dot(a.astype(BF16), b.astype(BF16), preferred_element_type=F32)


def _dot_t(a, b, hi):
    dn = (((1,), (1,)), ((), ()))
    if hi:
        return lax.dot_general(a, b, dn, precision=HIGHEST, preferred_element_type=F32)
    return lax.dot_general(a.astype(BF16), b.astype(BF16), dn, preferred_element_type=F32)


def _mod_rms(x, g, sc, sh):
    y = x * lax.rsqrt(jnp.mean(x * x, axis=-1, keepdims=True) + EPS)
    return y * g * (1.0 + sc) + sh


def _ada_kernel(c_ref, w_ref, b_ref, o_ref):
    c = c_ref[...]
    o_ref[...] = _dot(c * jax.nn.sigmoid(c), w_ref[...], True) + b_ref[...]


def _ada(c_all, w_ada, b_ada):
    n, d = c_all.shape
    cols = w_ada.shape[1]
    tn = 1024
    return pl.pallas_call(
        _ada_kernel,
        grid=(cols // tn,),
        in_specs=[pl.BlockSpec((n, d), lambda j: (0, 0)),
                  pl.BlockSpec((d, tn), lambda j: (0, j)),
                  pl.BlockSpec((1, tn), lambda j: (0, j))],
        out_specs=pl.BlockSpec((n, tn), lambda j: (0, j)),
        out_shape=jax.ShapeDtypeStruct((n, cols), F32),
        compiler_params=_params(("arbitrary",)),
        name="ada",
    )(c_all, w_ada, b_ada.reshape(1, cols))


def _mix_norm_kernel(x_ref, sc_ref, sh_ref, g_ref, h_ref):
    h_ref[...] = _mod_rms(x_ref[...], g_ref[...], sc_ref[0], sh_ref[0]).astype(h_ref.dtype)


def _mix_norm(x, sc, sh, g, *, rows_per_mod, tm, hi):
    t, d = x.shape
    mod_rows = sc.shape[1]
    mod_spec = pl.BlockSpec((1, mod_rows, d), lambda i: (i * tm // rows_per_mod, 0, 0))
    return pl.pallas_call(
        _mix_norm_kernel,
        grid=(t // tm,),
        in_specs=[pl.BlockSpec((tm, d), lambda i: (i, 0)), mod_spec, mod_spec, pl.BlockSpec((1, d), lambda i: (0, 0))],
        out_specs=pl.BlockSpec((tm, d), lambda i: (i, 0)),
        out_shape=jax.ShapeDtypeStruct((t, d), F32 if hi else BF16),
        compiler_params=_params(("arbitrary",)),
        name="mix_norm",
    )(x, sc, sh, g.reshape(1, d))


def _in_proj_kernel(h_ref, w_ref, o_ref, *, hi, act):
    z = _dot(h_ref[...], w_ref[...], hi)
    o_ref[...] = z if act is None else act(z)


def _in_proj(h, w_in, col0, n_cols, act, *, tm, hi):
    t, d = h.shape
    assert col0 % COL_TILE == 0 and n_cols % COL_TILE == 0
    blk0 = col0 // COL_TILE
    return pl.pallas_call(
        functools.partial(_in_proj_kernel, hi=hi, act=act),
        grid=(t // tm, n_cols // COL_TILE),
        in_specs=[pl.BlockSpec((tm, d), lambda i, j: (i, 0)),
                  pl.BlockSpec((d, COL_TILE), lambda i, j: (0, blk0 + j))],
        out_specs=pl.BlockSpec((tm, COL_TILE), lambda i, j: (i, j)),
        out_shape=jax.ShapeDtypeStruct((t, n_cols), F32),
        compiler_params=_params(("arbitrary", "arbitrary")),
        name="in_proj",
    )(h, w_in)


def _in_proj_all(x, sc, sh, g, w_in, *, rows_per_mod, tm_norm, tm, hi):
    d = x.shape[1]
    aw2 = 2 * A_GROUPS * CHUNK
    qkv = 3 * N_HEADS * HEAD_DIM
    h = _mix_norm(x, sc, sh, g, rows_per_mod=rows_per_mod, tm=tm_norm, hi=hi)
    z_a = _in_proj(h, w_in, 0, aw2, jax.nn.gelu, tm=tm, hi=hi)
    z_qkv = _in_proj(h, w_in, aw2, qkv, None, tm=tm, hi=hi)
    z_gate = _in_proj(h, w_in, aw2 + qkv, 2 * d, jax.nn.sigmoid, tm=tm, hi=hi)
    return z_a, z_qkv, z_gate


def _attend(q, kc, kp, vc, vp, slope, dil, prev_bias):
    row = lax.broadcasted_iota(I32, (CHUNK, CHUNK), 0)
    col = lax.broadcasted_iota(I32, (CHUNK, CHUNK), 1)
    pen_c = jnp.where(col <= row, -slope * ((row - col) * dil).astype(F32), NEG)
    pen_p = jnp.where(col >= row, -slope * ((row + CHUNK - col) * dil).astype(F32), NEG)
    scale = HEAD_DIM ** -0.5
    qk = (((2,), (2,)), ((0,), (0,)))
    pv = (((2,), (1,)), ((0,), (0,)))
    s_c = lax.dot_general(q, kc, qk, preferred_element_type=F32) * scale + pen_c[None]
    s_p = lax.dot_general(q, kp, qk, preferred_element_type=F32) * scale + pen_p[None] + prev_bias
    m = jnp.max(jnp.maximum(s_c, s_p), axis=2, keepdims=True)
    p_c = jnp.exp(s_c - m)
    p_p = jnp.exp(s_p - m)
    l = jnp.sum(p_c + p_p, axis=2, keepdims=True)
    o = (lax.dot_general((p_c / l).astype(BF16), vc, pv, preferred_element_type=F32)
         + lax.dot_general((p_p / l).astype(BF16), vp, pv, preferred_element_type=F32))
    return o, m + jnp.log(l)


def _dil_attn_kernel(slope_ref, *refs):
    ins = refs[:5 * N_GROUPS]
    o_ref = refs[5 * N_GROUPS]
    acc_refs = refs[5 * N_GROUPS + 1:5 * N_GROUPS + 1 + N_GROUPS]
    lse_refs = refs[5 * N_GROUPS + 1 + N_GROUPS:]
    head = pl.program_id(1)
    i = pl.program_id(2)
    no_prev = jnp.where(i > 0, 0.0, NEG)

    for g, (win, dil) in enumerate(DIL_GROUPS):
        q_ref, k_ref, v_ref, kp_ref, vp_ref = ins[5 * g:5 * g + 5]
        acc_ref, lse_ref = acc_refs[g], lse_refs[g]
        slope = slope_ref[g * HEADS_PER_GROUP + head]
        span = CHUNK * dil
        n_blocks = ATTN_BLOCK // CHUNK

        def rows(start, dil=dil):
            return pl.ds(start, CHUNK) if dil == 1 else pl.ds(start, CHUNK, stride=dil)

        cur = [rows((b // dil) * span + b % dil) for b in range(n_blocks)]
        prev = [rows((b // dil - 1) * span + b % dil) for b in range(n_blocks)]

        def gather(ref, first_ref=None):
            tiles = []
            for b in range(n_blocks):
                if first_ref is None:
                    tiles.append(ref[cur[b], :])
                elif b < dil:
                    tiles.append(first_ref[rows(b), :])
                else:
                    tiles.append(ref[prev[b], :])
            return jnp.stack(tiles).astype(BF16)

        first_blocks = lax.broadcasted_iota(I32, (n_blocks, 1, 1), 0) < dil
        o, lse = _attend(gather(q_ref), gather(k_ref), gather(k_ref, kp_ref), gather(v_ref), gather(v_ref, vp_ref),
                         slope, dil, jnp.where(first_blocks, no_prev, 0.0))
        for b in range(n_blocks):
            acc_ref[cur[b], :] = o[b]
            lse_ref[cur[b], :] = jnp.broadcast_to(lse[b], (CHUNK, HEAD_DIM))

    ls = [r[...] for r in lse_refs]
    m = jnp.maximum(jnp.maximum(ls[0], ls[1]), ls[2])
    e = [jnp.exp(l - m) for l in ls]
    den = e[0] + e[1] + e[2]
    o_ref[...] = (acc_refs[0][...] * (e[0] / den) + acc_refs[1][...] * (e[1] / den)
                  + acc_refs[2][...] * (e[2] / den))


def _dil_attn(z, bsz, seq):
    assert seq % ATTN_BLOCK == 0
    nblk = seq // ATTN_BLOCK
    a_blocks = 0
    in_specs = [pl.BlockSpec(memory_space=pltpu.SMEM)]
    for g, (win, dil) in enumerate(DIL_GROUPS):
        span = CHUNK * dil
        per_blk = ATTN_BLOCK // span
        for which in range(3):
            cb = a_blocks + (which * N_GROUPS + g) * HEADS_PER_GROUP
            in_specs.append(pl.BlockSpec((ATTN_BLOCK, HEAD_DIM), lambda b, h, i, cb=cb: (b * nblk + i, cb + h)))
        for which in (1, 2):
            cb = a_blocks + (which * N_GROUPS + g) * HEADS_PER_GROUP
            in_specs.append(pl.BlockSpec(
                (span, HEAD_DIM),
                lambda b, h, i, cb=cb, per_blk=per_blk: (jnp.maximum((b * nblk + i) * per_blk - 1, 0), cb + h)))
    slopes = jnp.asarray(ALIBI_SLOPES, F32)
    blk = pltpu.VMEM((ATTN_BLOCK, HEAD_DIM), F32)
    return pl.pallas_call(
        _dil_attn_kernel,
        grid=(bsz, HEADS_PER_GROUP, nblk),
        in_specs=in_specs,
        out_specs=pl.BlockSpec((ATTN_BLOCK, HEAD_DIM), lambda b, h, i: (b * nblk + i, h)),
        out_shape=jax.ShapeDtypeStruct((bsz * seq, GROUP_COLS), F32),
        scratch_shapes=[blk] * (2 * N_GROUPS),
        compiler_params=_params(("arbitrary", "arbitrary", "arbitrary")),
        name="dil_attn",
    )(slopes, *([z] * (5 * N_GROUPS)))


def _sample_attn_kernel(pen_ref, q_ref, k_ref, v_ref, c0_ref, c1_ref, c2_ref, o_ref):
    caches = (c0_ref, c1_ref, c2_ref)
    scale = HEAD_DIM ** -0.5
    heads = [(g, h) for g in range(N_GROUPS) for h in range(HEADS_PER_GROUP)]

    def head_rows(ref):
        return jnp.stack([ref[0, :, i * HEAD_DIM:(i + 1) * HEAD_DIM] for i in range(N_HEADS)])

    q = head_rows(q_ref)
    k_new = head_rows(k_ref)
    v_new = head_rows(v_ref)
    k_buf = jnp.stack([caches[g][:, 0, h, :] for g, h in heads])
    v_buf = jnp.stack([caches[g][:, 1, h, :] for g, h in heads])
    q8 = jnp.broadcast_to(q, (N_HEADS, SUBLANES, HEAD_DIM))
    qk = (((2,), (2,)), ((0,), (0,)))
    pv = (((2,), (1,)), ((0,), (0,)))
    s_buf = lax.dot_general(q8, k_buf, qk, precision=HIGHEST, preferred_element_type=F32)[:, 0:1, :]
    s_buf = s_buf * scale - pen_ref[...]
    s_new = jnp.sum(q * k_new, axis=2, keepdims=True) * scale
    m = jnp.maximum(jnp.max(s_buf, axis=2, keepdims=True), s_new)
    p_buf = jnp.exp(s_buf - m)
    p_new = jnp.exp(s_new - m)
    l = jnp.sum(p_buf, axis=2, keepdims=True) + p_new
    pb8 = jnp.broadcast_to(p_buf / l, (N_HEADS, SUBLANES, CHUNK))
    o = lax.dot_general(pb8, v_buf, pv, precision=HIGHEST, preferred_element_type=F32)[:, 0:1, :]
    o = o + (p_new / l) * v_new
    lse = m + jnp.log(l)
    n = HEADS_PER_GROUP
    ls = [lse[g * n:(g + 1) * n] for g in range(N_GROUPS)]
    mm = jnp.maximum(jnp.maximum(ls[0], ls[1]), ls[2])
    e = [jnp.exp(x - mm) for x in ls]
    den = e[0] + e[1] + e[2]
    out = o[0:n] * (e[0] / den) + o[n:2 * n] * (e[1] / den) + o[2 * n:3 * n] * (e[2] / den)
    for h in range(HEADS_PER_GROUP):
        o_ref[0, :, h * HEAD_DIM:(h + 1) * HEAD_DIM] = out[h]


def _sample_attn(zs, caches):
    n, cols = zs.shape
    z3 = zs.reshape(n, 1, cols)
    a_blocks = 0
    qkv_cols = N_GROUPS * GROUP_COLS
    c_views = []
    c_specs = []
    for g, (win, dil) in enumerate(DIL_GROUPS):
        c = caches[g]
        assert c.shape[1] == win, "cache must hold exactly one window"
        c_views.append(c.reshape(n, win // dil, dil, 2, HEADS_PER_GROUP, HEAD_DIM))
        c_specs.append(pl.BlockSpec((None, win // dil, None, 2, HEADS_PER_GROUP, HEAD_DIM),
                                    lambda b: (b, 0, 0, 0, 0, 0)))
    q0 = a_blocks * GROUP_COLS
    steps = jnp.arange(CHUNK, 0, -1, dtype=F32)
    pen = jnp.stack([ALIBI_SLOPES[g * HEADS_PER_GROUP + h] * dil * steps
                     for g, (win, dil) in enumerate(DIL_GROUPS) for h in range(HEADS_PER_GROUP)])
    assert all(win // dil == CHUNK for win, dil in DIL_GROUPS)
    row_spec = pl.BlockSpec((1, 1, qkv_cols), lambda b: (b, 0, 0))
    out = pl.pallas_call(
        _sample_attn_kernel,
        grid=(n,),
        in_specs=[pl.BlockSpec((N_HEADS, 1, CHUNK), lambda b: (0, 0, 0)), row_spec, row_spec, row_spec] + c_specs,
        out_specs=pl.BlockSpec((1, 1, GROUP_COLS), lambda b: (b, 0, 0)),
        out_shape=jax.ShapeDtypeStruct((n, 1, GROUP_COLS), F32),
        compiler_params=_params(("arbitrary",)),
        name="sample_attn",
    )(pen.reshape(N_HEADS, 1, CHUNK), z3[:, :, q0:q0 + qkv_cols], z3[:, :, q0 + qkv_cols:q0 + 2 * qkv_cols],
      z3[:, :, q0 + 2 * qkv_cols:q0 + 3 * qkv_cols], *c_views)
    return out.reshape(n, GROUP_COLS)


def _layer_norm(v, g, b):
    mu = jnp.mean(v, axis=-1, keepdims=True)
    var = jnp.mean(jnp.square(v - mu), axis=-1, keepdims=True)
    return (v - mu) * lax.rsqrt(var + EPS) * g + b


def _chunk_gate_kernel(u_ref, v_ref, lg_ref, lb_ref, ws_ref, bs_ref, o_ref, *, n_chunks):
    va = _layer_norm(v_ref[...], lg_ref[...], lb_ref[...]).astype(BF16)
    row = lax.broadcasted_iota(I32, (CHUNK, CHUNK), 0)
    col = lax.broadcasted_iota(I32, (CHUNK, CHUNK), 1)
    for g in range(A_GROUPS):
        w = jnp.where(col <= row, ws_ref[g], 0.0).astype(BF16)
        gs = slice(g * CHUNK, (g + 1) * CHUNK)
        for c in range(n_chunks):
            rs = slice(c * CHUNK, (c + 1) * CHUNK)
            mixed = jnp.dot(w, va[rs, gs], preferred_element_type=F32) + bs_ref[g]
            o_ref[rs, gs] = (u_ref[rs, gs] * mixed).astype(o_ref.dtype)


def _chunk_gate(z, ln_g, ln_b, w_s, b_s, *, tm):
    t = z.shape[0]
    aw = A_GROUPS * CHUNK
    bs_full = jnp.broadcast_to(b_s[:, :, None], (A_GROUPS, CHUNK, CHUNK))
    kern = functools.partial(_chunk_gate_kernel, n_chunks=tm // CHUNK)
    return pl.pallas_call(
        kern,
        grid=(t // tm,),
        in_specs=[pl.BlockSpec((tm, aw), lambda i: (i, 0)),
                  pl.BlockSpec((tm, aw), lambda i: (i, 1)),
                  pl.BlockSpec((1, aw), lambda i: (0, 0)),
                  pl.BlockSpec((1, aw), lambda i: (0, 0)),
                  pl.BlockSpec((A_GROUPS, CHUNK, CHUNK), lambda i: (0, 0, 0)),
                  pl.BlockSpec((A_GROUPS, CHUNK, CHUNK), lambda i: (0, 0, 0))],
        out_specs=pl.BlockSpec((tm, aw), lambda i: (i, 0)),
        out_shape=jax.ShapeDtypeStruct((t, aw), BF16),
        compiler_params=_params(("arbitrary",)),
        name="chunk_gate",
    )(z, z, ln_g.reshape(1, aw), ln_b.reshape(1, aw), w_s, bs_full)


def _sample_gate_kernel(u_ref, v_ref, lg_ref, lb_ref, w0_ref, b0_ref, a_ref, vn_ref):
    va = _layer_norm(v_ref[...], lg_ref[...], lb_ref[...])
    vn_ref[...] = va
    a_ref[...] = u_ref[...] * (w0_ref[...] * va + b0_ref[...])


def _sample_gate(zs, ln_g, ln_b, w_s, b_s):
    n = zs.shape[0]
    aw = A_GROUPS * CHUNK
    w0 = jnp.repeat(w_s[:, 0, 0], CHUNK).reshape(1, aw)
    b0 = jnp.repeat(b_s[:, 0], CHUNK).reshape(1, aw)
    vec = pl.BlockSpec((1, aw), lambda i: (0, 0))
    return pl.pallas_call(
        _sample_gate_kernel,
        grid=(1,),
        in_specs=[pl.BlockSpec((n, aw), lambda i: (0, 0)), pl.BlockSpec((n, aw), lambda i: (0, 1)),
                  vec, vec, vec, vec],
        out_specs=[pl.BlockSpec((n, aw), lambda i: (0, 0)), pl.BlockSpec((n, aw), lambda i: (0, 0))],
        out_shape=[jax.ShapeDtypeStruct((n, aw), F32), jax.ShapeDtypeStruct((n, aw), F32)],
        compiler_params=_params(("arbitrary",)),
        name="sample_gate",
    )(zs, zs, ln_g.reshape(1, aw), ln_b.reshape(1, aw), w0, b0)


def _merge_kernel(a_ref, b_ref, ga_ref, gb_ref, pa_ref, pb_ref, out_ref, *, hi):
    ya = _dot(a_ref[...], pa_ref[...], hi)
    yb = _dot(b_ref[...], pb_ref[...], hi)
    out_ref[...] = (ga_ref[...] * ya + gb_ref[...] * yb).astype(out_ref.dtype)


def _merge(a_out, b_out, z, p_a, p_b, *, tm, hi):
    t, aw = a_out.shape
    d = p_a.shape[1]
    tn = COL_TILE
    ga_blk0 = 0
    gb_blk0 = d // tn
    return pl.pallas_call(
        functools.partial(_merge_kernel, hi=hi),
        grid=(t // tm, d // tn),
        in_specs=[pl.BlockSpec((tm, aw), lambda i, j: (i, 0)),
                  pl.BlockSpec((tm, GROUP_COLS), lambda i, j: (i, 0)),
                  pl.BlockSpec((tm, tn), lambda i, j: (i, ga_blk0 + j)),
                  pl.BlockSpec((tm, tn), lambda i, j: (i, gb_blk0 + j)),
                  pl.BlockSpec((aw, tn), lambda i, j: (0, j)),
                  pl.BlockSpec((GROUP_COLS, tn), lambda i, j: (0, j))],
        out_specs=pl.BlockSpec((tm, tn), lambda i, j: (i, j)),
        out_shape=jax.ShapeDtypeStruct((t, d), F32 if hi else BF16),
        compiler_params=_params(("arbitrary", "arbitrary")),
        name="merge",
    )(a_out, b_out, z, z, p_a, p_b)


def _out_proj_kernel(m_ref, w_ref, x_ref, gt_ref, o_ref, *, hi):
    o_ref[...] = x_ref[...] + gt_ref[0] * _dot(m_ref[...], w_ref[...], hi)


def _out_proj(merged, w_o, x, gt, *, rows_per_mod, tm, hi):
    t, d = x.shape
    tn = COL_TILE
    mod_rows = gt.shape[1]
    return pl.pallas_call(
        functools.partial(_out_proj_kernel, hi=hi),
        grid=(t // tm, d // tn),
        in_specs=[pl.BlockSpec((tm, d), lambda i, j: (i, 0)),
                  pl.BlockSpec((d, tn), lambda i, j: (0, j)),
                  pl.BlockSpec((tm, tn), lambda i, j: (i, j)),
                  pl.BlockSpec((1, mod_rows, tn), lambda i, j: (i * tm // rows_per_mod, 0, j))],
        out_specs=pl.BlockSpec((tm, tn), lambda i, j: (i, j)),
        out_shape=jax.ShapeDtypeStruct((t, d), F32),
        compiler_params=_params(("arbitrary", "arbitrary")),
        name="out_proj",
    )(merged, w_o, x, gt)


def _pack_pair(lo, hi, pair_ref):
    n = lo.shape[0]
    pair_ref[pl.ds(0, n, stride=2), :] = lo
    pair_ref[pl.ds(1, n, stride=2), :] = hi
    return pltpu.bitcast(pair_ref[0:2 * n, :].astype(BF16), U32)


def _unpack_pair(w, pair_ref):
    n = w.shape[0]
    pair_ref[0:2 * n, :] = pltpu.bitcast(w, BF16).astype(F32)
    return pair_ref[pl.ds(0, n, stride=2), :].astype(BF16), pair_ref[pl.ds(1, n, stride=2), :].astype(BF16)


def _ffn_norm_kernel(x_ref, sc_ref, sh_ref, g_ref, wr_ref, br_ref, hp_ref, e_ref, w_ref, pair_ref, *, hi):
    h = _mod_rms(x_ref[...], g_ref[...], sc_ref[0], sh_ref[0])
    tm, d = h.shape
    for s in range(d // (2 * LANES)):
        even = h[:, (2 * s) * LANES:(2 * s + 1) * LANES]
        odd = h[:, (2 * s + 1) * LANES:(2 * s + 2) * LANES]
        hp_ref[pl.ds(s, tm, stride=SUBLANES), :] = _pack_pair(even, odd, pair_ref)
    if hi:
        logits = _dot(h, wr_ref[...], True)
    else:
        wr = wr_ref[...]
        h_hi, w_hi = h.astype(BF16), wr.astype(BF16)
        h_lo = (h - h_hi.astype(F32)).astype(BF16)
        w_lo = (wr - w_hi.astype(F32)).astype(BF16)
        logits = (jnp.dot(h_hi, w_hi, preferred_element_type=F32) + jnp.dot(h_hi, w_lo, preferred_element_type=F32)
                  + jnp.dot(h_lo, w_hi, preferred_element_type=F32))
    logits = logits + br_ref[...]
    lane = lax.broadcasted_iota(I32, logits.shape, 1)
    lane_f = lane.astype(F32)
    vals, idxs = [], []
    for _ in range(TOP_K):
        m = jnp.max(logits, axis=1, keepdims=True)
        idx = jnp.min(jnp.where(logits == m, lane_f, float(logits.shape[1])), axis=1, keepdims=True)
        vals.append(m)
        idxs.append(idx)
        logits = jnp.where(lane_f == idx, 2.0 * NEG, logits)
    es = [jnp.exp(v - vals[0]) for v in vals]
    den = es[0] + es[1] + es[2] + es[3]
    e_out = jnp.full(lane.shape, -1.0, F32)
    w_out = jnp.zeros(lane.shape, F32)
    for k in range(TOP_K):
        e_out = jnp.where(lane == k, idxs[k], e_out)
        w_out = jnp.where(lane == k, es[k] / den, w_out)
    e_ref[...] = e_out.astype(I32)
    w_ref[...] = w_out


def _ffn_norm(x1, sc, sh, g, w_router, b_router, *, rows_per_mod, tm, hi):
    t, d = x1.shape
    assert d == 2 * LANES * SUBLANES, "one packed row must be exactly one (8, 128) tile"
    ne = w_router.shape[1]
    wr = jnp.zeros((d, LANES), F32).at[:, :ne].set(w_router)
    br = jnp.full((1, LANES), NEG, F32).at[0, :ne].set(b_router)
    mod_rows = sc.shape[1]
    mod_spec = pl.BlockSpec((1, mod_rows, d), lambda i: (i * tm // rows_per_mod, 0, 0))
    return pl.pallas_call(
        functools.partial(_ffn_norm_kernel, hi=hi),
        grid=(t // tm,),
        in_specs=[pl.BlockSpec((tm, d), lambda i: (i, 0)), mod_spec, mod_spec,
                  pl.BlockSpec((1, d), lambda i: (0, 0)),
                  pl.BlockSpec((d, LANES), lambda i: (0, 0)),
                  pl.BlockSpec((1, LANES), lambda i: (0, 0))],
        out_specs=[pl.BlockSpec((tm * SUBLANES, LANES), lambda i: (i, 0)),
                   pl.BlockSpec((tm, LANES), lambda i: (i, 0)),
                   pl.BlockSpec((tm, LANES), lambda i: (i, 0))],
        out_shape=[jax.ShapeDtypeStruct((t * SUBLANES, LANES), U32),
                   jax.ShapeDtypeStruct((t, LANES), I32),
                   jax.ShapeDtypeStruct((t, LANES), F32)],
        scratch_shapes=[pltpu.VMEM((2 * tm, LANES), F32)],
        compiler_params=_params(("arbitrary",)),
        name="ffn_norm",
    )(x1, sc, sh, g.reshape(1, d), wr, br)


def _route_kernel(e_ref, pos_ref, cnt_ref, tri_ref, run_ref):
    phase = pl.program_id(0)
    i = pl.program_id(1)
    e = e_ref[...]
    tm = e.shape[0]
    lane = lax.broadcasted_iota(I32, (tm, LANES), 1)
    hits = [lane == e[:, k:k + 1] for k in range(TOP_K)]
    chosen = jnp.zeros((tm, LANES), F32)
    for k in range(TOP_K):
        chosen = jnp.where(hits[k], 1.0, chosen)
    col_count = jnp.sum(chosen, axis=0, keepdims=True)

    @pl.when((phase == 0) & (i == 0))
    def _():
        cnt_ref[...] = jnp.zeros(cnt_ref.shape, F32)
        r = lax.broadcasted_iota(I32, (tm, tm), 0)
        c = lax.broadcasted_iota(I32, (tm, tm), 1)
        tri_ref[...] = jnp.where(c < r, 1.0, 0.0).astype(BF16)

    @pl.when(phase == 0)
    def _():
        cnt_ref[...] = cnt_ref[...] + col_count

    @pl.when((phase == 1) & (i == 0))
    def _():
        tiles = jnp.floor((cnt_ref[...] + (MOE_TILE - 1)) * (1.0 / MOE_TILE))
        r = lax.broadcasted_iota(I32, (LANES, LANES), 0)
        c = lax.broadcasted_iota(I32, (LANES, LANES), 1)
        below = jnp.where(r < c, 1.0, 0.0)
        tiles8 = jnp.broadcast_to(tiles, (SUBLANES, LANES))
        run_ref[...] = _dot(tiles8, below, True)[0:1] * float(MOE_TILE)

    @pl.when(phase == 1)
    def _():
        before = jnp.dot(tri_ref[...], chosen.astype(BF16), preferred_element_type=F32)
        dest = run_ref[...] + before
        out = jnp.zeros((tm, LANES), F32)
        for k in range(TOP_K):
            p = jnp.sum(jnp.where(hits[k], dest, 0.0), axis=1, keepdims=True)
            out = jnp.where(lane == k, p, out)
        pos_ref[...] = out.astype(I32)
        run_ref[...] = run_ref[...] + col_count


def _route(e_all):
    t = e_all.shape[0]
    steps = t // ROUTE_TILE
    return pl.pallas_call(
        _route_kernel,
        grid=(2, steps),
        in_specs=[pl.BlockSpec((ROUTE_TILE, LANES), lambda p, i: (i, 0))],
        out_specs=[pl.BlockSpec((ROUTE_TILE, LANES), lambda p, i: (i * p, 0)),
                   pl.BlockSpec((1, LANES), lambda p, i: (0, 0))],
        out_shape=[jax.ShapeDtypeStruct((t, LANES), I32), jax.ShapeDtypeStruct((1, LANES), F32)],
        scratch_shapes=[pltpu.VMEM((ROUTE_TILE, ROUTE_TILE), BF16), pltpu.VMEM((1, LANES), F32)],
        compiler_params=_params(("arbitrary", "arbitrary")),
        name="route",
    )(e_all)


def _row_scatter_kernel(pos_ref, pos_s_ref, pad_base_ref, pad_cnt_ref, hp_ref, hs_ref, xs_hbm, sem, *,
                        n_prompt_steps, n_sample):
    i = pl.program_id(0)

    def rows(first, n=1):
        return pl.ds(pl.multiple_of(first * SUBLANES, SUBLANES), n * SUBLANES)

    def scatter(src_ref, idx_ref, n_tok):
        def body(tt, carry):
            for k in range(TOP_K):
                pltpu.make_async_copy(src_ref.at[rows(tt)], xs_hbm.at[rows(idx_ref[0, 0, tt * TOP_K + k])],
                                      sem).start()
            return carry
        lax.fori_loop(0, n_tok, body, 0, unroll=2)
        for _ in range(TOP_K):
            pltpu.make_async_copy(src_ref.at[rows(0, n_tok)], xs_hbm.at[rows(0, n_tok)], sem).wait()

    @pl.when(i < n_prompt_steps)
    def _():
        scatter(hp_ref, pos_ref, ROUTE_TILE)

    @pl.when(i == n_prompt_steps)
    def _():
        scatter(hs_ref, pos_s_ref, n_sample)

        def fill(dst_row, size):
            return pltpu.make_async_copy(hp_ref.at[rows(0, size)], xs_hbm.at[rows(dst_row, size)], sem)

        def per_segment(wait):
            def body(e, carry):
                cnt = pad_cnt_ref[e]
                base = pad_base_ref[e]

                def whole(r, c):
                    cp = fill(base + (cnt % MOE_TILE) + r * MOE_TILE, MOE_TILE)
                    cp.wait() if wait else cp.start()
                    return c
                lax.fori_loop(0, cnt // MOE_TILE, whole, 0)
                size = MOE_TILE // 2
                while size >= 1:
                    @pl.when((cnt & size) != 0)
                    def _(size=size):
                        cp = fill(base + (cnt & (size - 1)), size)
                        cp.wait() if wait else cp.start()
                    size //= 2
                return carry
            return body
        lax.fori_loop(0, pad_cnt_ref.shape[0], per_segment(False), 0)
        lax.fori_loop(0, pad_cnt_ref.shape[0], per_segment(True), 0)


def _row_scatter(hp_p, hp_s, pos_p, pos_s, pad_base, pad_cnt, n_rows):
    t_p = hp_p.shape[0] // SUBLANES
    n_s = hp_s.shape[0] // SUBLANES
    assert ROUTE_TILE >= MOE_TILE, "padding blocks are copied from one token tile"
    steps = t_p // ROUTE_TILE
    tile_rows = ROUTE_TILE * TOP_K
    return pl.pallas_call(
        functools.partial(_row_scatter_kernel, n_prompt_steps=steps, n_sample=n_s),
        grid=(steps + 1,),
        in_specs=[pl.BlockSpec((1, 1, tile_rows), lambda i: (jnp.minimum(i, steps - 1), 0, 0),
                               memory_space=pltpu.SMEM),
                  pl.BlockSpec((1, 1, n_s * TOP_K), lambda i: (0, 0, 0), memory_space=pltpu.SMEM),
                  pl.BlockSpec(memory_space=pltpu.SMEM),
                  pl.BlockSpec(memory_space=pltpu.SMEM),
                  pl.BlockSpec((ROUTE_TILE * SUBLANES, LANES), lambda i: (jnp.minimum(i, steps - 1), 0)),
                  pl.BlockSpec((n_s * SUBLANES, LANES), lambda i: (0, 0))],
        out_specs=pl.BlockSpec(memory_space=pl.ANY),
        out_shape=jax.ShapeDtypeStruct((n_rows * SUBLANES, LANES), U32),
        scratch_shapes=[pltpu.SemaphoreType.DMA(())],
        compiler_params=_params(("arbitrary",)),
        name="row_scatter",
    )(pos_p.reshape(steps, 1, tile_rows), pos_s.reshape(1, 1, n_s * TOP_K), pad_base, pad_cnt, hp_p, hp_s)


def _expert_weights(seg_ref, sege_ref, nu_ref, fetch, on_ready):
    j = pl.program_id(0)
    t = pl.program_id(1)
    n_j = pl.num_programs(0)
    seg = seg_ref[t]
    n_seg = nu_ref[1]
    first = (t < nu_ref[0]) & ((t == 0) | (seg != seg_ref[jnp.maximum(t - 1, 0)]))
    k = j * n_seg + seg
    slot = lax.rem(k, 2)

    @pl.when(first)
    def _():
        @pl.when(k == 0)
        def _():
            for cp in fetch(sege_ref[0], 0, 0):
                cp.start()

        more = seg + 1 < n_seg

        @pl.when(more | (j + 1 < n_j))
        def _():
            for cp in fetch(sege_ref[jnp.where(more, seg + 1, 0)], jnp.where(more, j, j + 1), 1 - slot):
                cp.start()

        for cp in fetch(sege_ref[seg], j, slot):
            cp.wait()
        on_ready(slot)


def _moe_up_kernel(te_ref, nv_ref, seg_ref, sege_ref, nu_ref, x_ref, wg_hbm, wu_hbm, bg_ref, bu_ref, h_ref,
                   wg_buf, wu_buf, sem, wg_s, wu_s, xb_s, pair_ref):
    t = pl.program_id(1)
    tf = wg_s.shape[1]

    def fetch(e, j, slot):
        cols = pl.ds(pl.multiple_of(j * tf, tf), tf)
        return [pltpu.make_async_copy(wg_hbm.at[e, :, cols], wg_buf.at[slot], sem.at[slot]),
                pltpu.make_async_copy(wu_hbm.at[e, :, cols], wu_buf.at[slot], sem.at[slot])]

    def on_ready(slot):
        wg_s[...] = wg_buf[slot].astype(BF16)
        wu_s[...] = wu_buf[slot].astype(BF16)

    _expert_weights(seg_ref, sege_ref, nu_ref, fetch, on_ready)

    def compute(rows):
        for s in range(SUBLANES):
            lo, hi = _unpack_pair(x_ref[pl.ds(s, rows, stride=SUBLANES), :], pair_ref)
            xb_s[0:rows, (2 * s) * LANES:(2 * s + 1) * LANES] = lo
            xb_s[0:rows, (2 * s + 1) * LANES:(2 * s + 2) * LANES] = hi
        x = xb_s[0:rows, :]
        gt = jnp.dot(x, wg_s[...], preferred_element_type=F32) + bg_ref[0]
        up = jnp.dot(x, wu_s[...], preferred_element_type=F32) + bu_ref[0]
        gt = jnp.minimum(gt, SWIGLU_LIMIT)
        up = jnp.clip(up, -SWIGLU_LIMIT, SWIGLU_LIMIT)
        h_ref[0:rows, :] = ((up + 1.0) * gt * jax.nn.sigmoid(SWIGLU_ALPHA * gt)).astype(h_ref.dtype)

    quarters = jnp.where(t < nu_ref[0], (nv_ref[t] + MOE_QUARTER - 1) // MOE_QUARTER, 0)
    for n_q in range(MOE_TILE // MOE_QUARTER + 1):
        @pl.when(quarters == n_q)
        def _(rows=n_q * MOE_QUARTER):
            if rows > 0:
                compute(rows)
            if rows < MOE_TILE:
                h_ref[rows:, :] = jnp.zeros((MOE_TILE - rows, h_ref.shape[1]), h_ref.dtype)


def _moe_down_kernel(te_ref, nv_ref, seg_ref, sege_ref, nu_ref, h_ref, wd_hbm, bd_ref, y_ref, wd_buf, sem, wd_s):
    t = pl.program_id(1)
    n_sub = y_ref.shape[0] // MOE_TILE
    tn = wd_s.shape[1]

    def fetch(e, j, slot):
        cols = pl.ds(pl.multiple_of(j * tn, tn), tn)
        return [pltpu.make_async_copy(wd_hbm.at[e, :, cols], wd_buf.at[slot], sem.at[slot])]

    def on_ready(slot):
        wd_s[...] = wd_buf[slot].astype(BF16)

    _expert_weights(seg_ref, sege_ref, nu_ref, fetch, on_ready)

    def store(rows, y):
        for c in range(n_sub):
            y_ref[pl.ds(c, rows, stride=n_sub), :] = y[:, c * LANES:(c + 1) * LANES]

    def compute(rows):
        store(rows, jnp.dot(h_ref[0:rows, :], wd_s[...], preferred_element_type=F32) + bd_ref[0])

    quarters = jnp.where(t < nu_ref[0], (nv_ref[t] + MOE_QUARTER - 1) // MOE_QUARTER, 0)
    for n_q in range(MOE_TILE // MOE_QUARTER + 1):
        @pl.when(quarters == n_q)
        def _(rows=n_q * MOE_QUARTER):
            if rows > 0:
                compute(rows)
            if rows < MOE_TILE:
                y_ref[rows * n_sub:, :] = jnp.zeros(((MOE_TILE - rows) * n_sub, LANES), F32)


def _moe_experts(x_sorted, plan, w_gate, b_gate, w_up, b_up, w_down, b_down):
    rows = x_sorted.shape[0] // SUBLANES
    ne, d, f = w_gate.shape
    nt = rows // MOE_TILE
    tf = 1024
    tn = d
    n_sub = d // LANES
    n_plan = len(plan)

    def row_map(j, t, te, nv, seg, sege, nu):
        return (jnp.minimum(t, nu[0] - 1), 0)

    def b_map(j, t, te, nv, seg, sege, nu):
        return (te[t], 0, j)

    hbm = pl.BlockSpec(memory_space=pl.ANY)
    h = pl.pallas_call(
        _moe_up_kernel,
        grid_spec=pltpu.PrefetchScalarGridSpec(
            num_scalar_prefetch=n_plan,
            grid=(f // tf, nt),
            in_specs=[pl.BlockSpec((MOE_TILE * SUBLANES, LANES), row_map), hbm, hbm,
                      pl.BlockSpec((1, 1, tf), b_map),
                      pl.BlockSpec((1, 1, tf), b_map)],
            out_specs=pl.BlockSpec((MOE_TILE, tf), lambda j, t, *_: (t, j)),
            scratch_shapes=[pltpu.VMEM((2, d, tf), F32), pltpu.VMEM((2, d, tf), F32), pltpu.SemaphoreType.DMA((2,)),
                            pltpu.VMEM((d, tf), BF16), pltpu.VMEM((d, tf), BF16), pltpu.VMEM((MOE_TILE, d), BF16),
                            pltpu.VMEM((2 * MOE_TILE, LANES), F32)]),
        out_shape=jax.ShapeDtypeStruct((rows, f), BF16),
        compiler_params=_params(("arbitrary", "arbitrary")),
        name="moe_up",
    )(*plan, x_sorted, w_gate, w_up, b_gate.reshape(ne, 1, f), b_up.reshape(ne, 1, f))

    y = pl.pallas_call(
        _moe_down_kernel,
        grid_spec=pltpu.PrefetchScalarGridSpec(
            num_scalar_prefetch=n_plan,
            grid=(d // tn, nt),
            in_specs=[pl.BlockSpec((MOE_TILE, f), row_map), hbm,
                      pl.BlockSpec((1, 1, tn), b_map)],
            out_specs=pl.BlockSpec((MOE_TILE * n_sub, LANES), lambda j, t, *_: (t, 0)),
            scratch_shapes=[pltpu.VMEM((2, f, tn), F32), pltpu.SemaphoreType.DMA((2,)), pltpu.VMEM((f, tn), BF16)]),
        out_shape=jax.ShapeDtypeStruct((rows * n_sub, LANES), F32),
        compiler_params=_params(("arbitrary", "arbitrary")),
        name="moe_down",
    )(*plan, h, w_down, b_down.reshape(ne, 1, d))
    return y.reshape(rows, d // LANES, LANES)


def _row_copy(src_hbm, buf, sem, slot, src_row, dst_row, n_sub):
    dst = pl.ds(pl.multiple_of(dst_row * n_sub, SUBLANES), n_sub)
    return pltpu.make_async_copy(src_hbm.at[src_row], buf.at[slot, dst], sem.at[slot])


def _issue_rows(idx_ref, src_hbm, buf, sem, slot, n_rows, n_sub):
    def body(r, carry):
        _row_copy(src_hbm, buf, sem, slot, idx_ref[0, 0, r], r, n_sub).start()
        return carry
    lax.fori_loop(0, n_rows, body, 0, unroll=8)


def _moe_combine_kernel(idx_ref, nxt_ref, y_hbm, w_ref, x_ref, gt_ref, g_ref, o_ref, buf, sem, *, tok):
    i = pl.program_id(0)
    n = pl.num_programs(0)
    n_rows = TOP_K * tok
    n_sub = y_hbm.shape[1]
    slot = lax.rem(i, 2)

    @pl.when(i == 0)
    def _():
        _issue_rows(idx_ref, y_hbm, buf, sem, 0, n_rows, n_sub)

    for s in range(2):
        @pl.when((i + 1 < n) & (slot == s))
        def _():
            _issue_rows(nxt_ref, y_hbm, buf, sem, 1 - s, n_rows, n_sub)

    pltpu.make_async_copy(buf.at[slot], buf.at[slot], sem.at[slot]).wait()

    w = w_ref[...]
    sumsq = jnp.zeros((tok, 1), F32)
    for c in range(n_sub):
        cs = slice(c * LANES, (c + 1) * LANES)
        acc = jnp.zeros((tok, LANES), F32)
        for k in range(TOP_K):
            part = buf[slot, pl.ds(k * tok * n_sub + c, tok, stride=n_sub), :]
            acc = acc + w[:, k:k + 1] * part
        x = x_ref[:, cs] + gt_ref[0, :, cs] * acc
        o_ref[:, cs] = x
        sumsq = sumsq + jnp.sum(x * x, axis=1, keepdims=True)
    o_ref[...] = o_ref[...] * lax.rsqrt(sumsq / (n_sub * LANES) + EPS) * g_ref[...]


def _moe_combine(y_rows, pos, top_w, x1, gt, g_final, *, rows_per_mod, tok):
    t, d = x1.shape
    n_sub = y_rows.shape[1]
    steps = t // tok
    n_rows = TOP_K * tok
    idx3 = pos[:, :TOP_K].reshape(steps, tok, TOP_K).transpose(0, 2, 1).reshape(steps, 1, n_rows)
    lanes = top_w.shape[1]
    mod_rows = gt.shape[1]
    idx_blk = (1, 1, n_rows)
    return pl.pallas_call(
        functools.partial(_moe_combine_kernel, tok=tok),
        grid=(steps,),
        in_specs=[pl.BlockSpec(idx_blk, lambda i: (i, 0, 0), memory_space=pltpu.SMEM),
                  pl.BlockSpec(idx_blk, lambda i: (jnp.minimum(i + 1, steps - 1), 0, 0), memory_space=pltpu.SMEM),
                  pl.BlockSpec(memory_space=pl.ANY),
                  pl.BlockSpec((tok, lanes), lambda i: (i, 0)),
                  pl.BlockSpec((tok, d), lambda i: (i, 0)),
                  pl.BlockSpec((1, mod_rows, d), lambda i: (i * tok // rows_per_mod, 0, 0)),
                  pl.BlockSpec((1, d), lambda i: (0, 0))],
        out_specs=pl.BlockSpec((tok, d), lambda i: (i, 0)),
        out_shape=jax.ShapeDtypeStruct((t, d), F32),
        scratch_shapes=[pltpu.VMEM((2, n_rows * n_sub, LANES), F32), pltpu.SemaphoreType.DMA((2,))],
        compiler_params=_params(("arbitrary",)),
        name="moe_combine",
    )(idx3, idx3, y_rows, top_w, x1, gt, g_final.reshape(1, d))


def _tile_table(counts, n_tiles):
    tiles_e = (counts + MOE_TILE - 1) // MOE_TILE
    tile_end = jnp.cumsum(tiles_e)
    tile_start = tile_end - tiles_e
    n_used = tile_end[-1]
    tile_ids = jnp.minimum(jnp.arange(n_tiles, dtype=I32), n_used - 1)
    tile_expert = jnp.minimum(jnp.sum(tile_end[None, :] <= tile_ids[:, None], axis=1), N_EXPERTS - 1).astype(I32)
    tile_valid = jnp.clip(counts[tile_expert] - (tile_ids - tile_start[tile_expert]) * MOE_TILE, 0, MOE_TILE)
    nonempty = tiles_e > 0
    seg_of_expert = jnp.cumsum(nonempty.astype(I32)) - 1
    n_seg = jnp.sum(nonempty.astype(I32))
    experts = jnp.arange(N_EXPERTS, dtype=I32)
    hit = nonempty[None, :] & (seg_of_expert[None, :] == jnp.arange(N_EXPERTS + 1, dtype=I32)[:, None])
    seg_expert = jnp.sum(jnp.where(hit, experts[None, :], 0), axis=1).astype(I32)
    tile_seg = seg_of_expert[tile_expert].astype(I32)
    plan = (tile_expert, tile_valid.astype(I32), tile_seg, seg_expert, jnp.stack([n_used, n_seg]).astype(I32))
    pad_base = jnp.concatenate([tile_start * MOE_TILE + counts, (n_used * MOE_TILE).reshape(1)])
    pad_cnt = jnp.concatenate([tiles_e * MOE_TILE - counts, ((n_tiles - n_used) * MOE_TILE).reshape(1)])
    return plan, pad_base.astype(I32), pad_cnt.astype(I32)


def _row_tile(t, want):
    tm = min(t, want)
    assert t % tm == 0
    return tm


def kernel(x_prompt, x_sample, cache_kv_g0, cache_kv_g1, cache_kv_g2, c_prompt, c_sample, w_ada, b_ada, g_mix, w_in, ln_g, ln_b, w_s, b_s, p_a, p_b, w_o, g_ffn, w_router, b_router, w_gate, b_gate, w_up, b_up, w_down, b_down, g_final):
    depth = w_ada.shape[0]
    assert depth == 1, "single-layer trunk"
    bsz, seq, d = x_prompt.shape
    n_s, dec_seq, _ = x_sample.shape
    assert dec_seq == 1, "one new position per sample"
    caches = (cache_kv_g0, cache_kv_g1, cache_kv_g2)
    l = 0
    cols = w_in.shape[2]
    t_p = bsz * seq
    aw = A_GROUPS * CHUNK

    n_c = bsz + n_s
    n_c_pad = -(-n_c // SUBLANES) * SUBLANES
    c_all = jnp.concatenate([c_prompt, c_sample, jnp.zeros((n_c_pad - n_c, d), F32)], axis=0)
    mod = _ada(c_all, w_ada[l], b_ada[l])
    mod_p = mod[:bsz].reshape(bsz, 1, N_ADA, d)
    mod_s = mod[bsz:n_c].reshape(1, n_s, N_ADA, d)
    sh1_p, sc1_p, gt1_p, sh2_p, sc2_p, gt2_p = (mod_p[:, :, k] for k in range(N_ADA))
    sh1_s, sc1_s, gt1_s, sh2_s, sc2_s, gt2_s = (mod_s[:, :, k] for k in range(N_ADA))

    xp = x_prompt.reshape(t_p, d)
    xs = x_sample.reshape(n_s, d)

    tm_big = _row_tile(seq, 1024)
    tm_mid = _row_tile(seq, 512)
    tm_huge = _row_tile(seq, 2048)
    za_p, zqkv_p, zg_p = _in_proj_all(xp, sc1_p, sh1_p, g_mix[l], w_in[l], rows_per_mod=seq, tm_norm=tm_mid,
                                      tm=tm_huge, hi=False)
    b_p = _dil_attn(zqkv_p, bsz, seq)
    a_p = _chunk_gate(za_p, ln_g[l], ln_b[l], w_s[l], b_s[l], tm=tm_mid)
    merged_p = _merge(a_p, b_p, zg_p, p_a[l], p_b[l], tm=tm_big, hi=False)
    x1_p = _out_proj(merged_p, w_o[l], xp, gt1_p, rows_per_mod=seq, tm=tm_huge, hi=False)
    hp_p, e_p, w_p = _ffn_norm(x1_p, sc2_p, sh2_p, g_ffn[l], w_router[l], b_router[l], rows_per_mod=seq, tm=tm_mid,
                               hi=False)

    za_s, zqkv_s, zg_s = _in_proj_all(xs, sc1_s, sh1_s, g_mix[l], w_in[l], rows_per_mod=n_s, tm_norm=n_s, tm=n_s,
                                      hi=True)
    b_s_out = _sample_attn(zqkv_s, tuple(c[l] for c in caches))
    a_s, vn_s = _sample_gate(za_s, ln_g[l], ln_b[l], w_s[l], b_s[l])
    merged_s = _merge(a_s, b_s_out, zg_s, p_a[l], p_b[l], tm=n_s, hi=True)
    x1_s = _out_proj(merged_s, w_o[l], xs, gt1_s, rows_per_mod=n_s, tm=n_s, hi=True)
    hp_s, e_s, w_s_top = _ffn_norm(x1_s, sc2_s, sh2_s, g_ffn[l], w_router[l], b_router[l], rows_per_mod=n_s, tm=n_s,
                                   hi=True)

    assert t_p % ROUTE_TILE == 0
    t_all = t_p + n_s
    t_pad = -(-t_all // ROUTE_TILE) * ROUTE_TILE
    e_all = jnp.concatenate([e_p, e_s, jnp.full((t_pad - t_all, LANES), -1, I32)], axis=0)
    pos, cnt = _route(e_all)
    counts = cnt[0, :N_EXPERTS].astype(I32)
    n_tiles = (t_all * TOP_K + N_EXPERTS * (MOE_TILE - 1)) // MOE_TILE
    plan, pad_base, pad_cnt = _tile_table(counts, n_tiles)
    pos_p = pos[:t_p, :TOP_K]
    pos_s = pos[t_p:t_all, :TOP_K]
    x_sorted = _row_scatter(hp_p, hp_s, pos_p.reshape(-1), pos_s.reshape(-1), pad_base, pad_cnt,
                            n_tiles * MOE_TILE)
    y_rows = _moe_experts(x_sorted, plan, w_gate[l], b_gate[l], w_up[l], b_up[l], w_down[l], b_down[l])
    y_p = _moe_combine(y_rows, pos_p, w_p, x1_p, gt2_p, g_final, rows_per_mod=seq, tok=_row_tile(seq, 256))
    y_s = _moe_combine(y_rows, pos_s, w_s_top, x1_s, gt2_s, g_final, rows_per_mod=n_s, tok=n_s)

    k0 = N_GROUPS * GROUP_COLS
    v0 = k0 + N_GROUPS * GROUP_COLS
    z_p3 = zqkv_p.reshape(bsz, seq, 3 * N_GROUPS * GROUP_COLS)
    z_s = zqkv_s
    kv_prompt, kv_sample = [], []
    for g, (win, dil) in enumerate(DIL_GROUPS):
        keep = min(win, seq)
        kc = slice(k0 + g * GROUP_COLS, k0 + (g + 1) * GROUP_COLS)
        vc = slice(v0 + g * GROUP_COLS, v0 + (g + 1) * GROUP_COLS)
        kv = jnp.stack([z_p3[:, seq - keep:, kc], z_p3[:, seq - keep:, vc]], axis=2)
        kv_prompt.append(kv.reshape(1, bsz, keep, 2, HEADS_PER_GROUP, HEAD_DIM))
        kvs = jnp.stack([z_s[:, kc], z_s[:, vc]], axis=1)
        kv_sample.append(kvs.reshape(1, n_s, 1, 2, HEADS_PER_GROUP, HEAD_DIM))
    return (y_p.reshape(bsz, seq, d), y_s.reshape(n_s, 1, d),
            kv_prompt[0], kv_prompt[1], kv_prompt[2],
            kv_sample[0], kv_sample[1], kv_sample[2],
            vn_s.reshape(1, n_s, 1, aw))
```

```python
import functools

import jax
import jax.numpy as jnp
from jax import lax
from jax.experimental import pallas as pl
from jax.experimental.pallas import tpu as pltpu

F32 = jnp.float32
BF16 = jnp.bfloat16
U32 = jnp.uint32
I32 = jnp.int32
HIGHEST = lax.Precision.HIGHEST

EPS = 1e-6
A_GROUPS = 8
CHUNK = 128
HEAD_DIM = 128
HEADS_PER_GROUP = 4
DIL_GROUPS = ((128, 1), (512, 4), (2048, 16))
N_GROUPS = len(DIL_GROUPS)
N_HEADS = HEADS_PER_GROUP * N_GROUPS
ALIBI_SLOPES = tuple(2.0 ** (-8.0 * (h + 1) / N_HEADS) for h in range(N_HEADS))
GROUP_COLS = HEADS_PER_GROUP * HEAD_DIM
N_EXPERTS = 32
TOP_K = 4
SWIGLU_LIMIT = 7.0
SWIGLU_ALPHA = 1.702
N_ADA = 6
NEG = -1e30

LANES = 128
SUBLANES = 8
COL_TILE = 512
ATTN_BLOCK = 2048
MOE_TILE = 512
MOE_QUARTER = MOE_TILE // 4
ROUTE_TILE = 512
VMEM_LIMIT = 56 * 1024 * 1024


def _params(sem, vmem=VMEM_LIMIT):
    return pltpu.CompilerParams(dimension_semantics=sem, vmem_limit_bytes=vmem)


def _dot(a, b, hi):
    if hi:
        return jnp.dot(a, b, precision=HIGHEST, preferred_element_type=F32)
    return jnp.dot(a.astype(BF16), b.astype(BF16), preferred_element_type=F32)


def _dot_t(a, b, hi):
    dn = (((1,), (1,)), ((), ()))
    if hi:
        return lax.dot_general(a, b, dn, precision=HIGHEST, preferred_element_type=F32)
    return lax.dot_general(a.astype(BF16), b.astype(BF16), dn, preferred_element_type=F32)


def _mod_rms(x, g, sc, sh):
    y = x * lax.rsqrt(jnp.mean(x * x, axis=-1, keepdims=True) + EPS)
    return y * g * (1.0 + sc) + sh


def _ada_kernel(c_ref, w_ref, b_ref, o_ref):
    c = c_ref[...]
    o_ref[...] = _dot(c * jax.nn.sigmoid(c), w_ref[...], True) + b_ref[...]


def _ada(c_all, w_ada, b_ada):
    n, d = c_all.shape
    cols = w_ada.shape[1]
    tn = 1024
    return pl.pallas_call(
        _ada_kernel,
        grid=(cols // tn,),
        in_specs=[pl.BlockSpec((n, d), lambda j: (0, 0)),
                  pl.BlockSpec((d, tn), lambda j: (0, j)),
                  pl.BlockSpec((1, tn), lambda j: (0, j))],
        out_specs=pl.BlockSpec((n, tn), lambda j: (0, j)),
        out_shape=jax.ShapeDtypeStruct((n, cols), F32),
        compiler_params=_params(("arbitrary",)),
        name="ada",
    )(c_all, w_ada, b_ada.reshape(1, cols))


def _mix_norm_kernel(x_ref, sc_ref, sh_ref, g_ref, h_ref):
    h_ref[...] = _mod_rms(x_ref[...], g_ref[...], sc_ref[0], sh_ref[0]).astype(h_ref.dtype)


def _mix_norm(x, sc, sh, g, *, rows_per_mod, tm, hi):
    t, d = x.shape
    mod_rows = sc.shape[1]
    mod_spec = pl.BlockSpec((1, mod_rows, d), lambda i: (i * tm // rows_per_mod, 0, 0))
    return pl.pallas_call(
        _mix_norm_kernel,
        grid=(t // tm,),
        in_specs=[pl.BlockSpec((tm, d), lambda i: (i, 0)), mod_spec, mod_spec, pl.BlockSpec((1, d), lambda i: (0, 0))],
        out_specs=pl.BlockSpec((tm, d), lambda i: (i, 0)),
        out_shape=jax.ShapeDtypeStruct((t, d), F32 if hi else BF16),
        compiler_params=_params(("arbitrary",)),
        name="mix_norm",
    )(x, sc, sh, g.reshape(1, d))


def _in_proj_kernel(h_ref, w_ref, o_ref, *, hi, act):
    z = _dot(h_ref[...], w_ref[...], hi)
    o_ref[...] = z if act is None else act(z)


def _in_proj(h, w_in, col0, n_cols, act, *, tm, hi):
    t, d = h.shape
    assert col0 % COL_TILE == 0 and n_cols % COL_TILE == 0
    blk0 = col0 // COL_TILE
    return pl.pallas_call(
        functools.partial(_in_proj_kernel, hi=hi, act=act),
        grid=(t // tm, n_cols // COL_TILE),
        in_specs=[pl.BlockSpec((tm, d), lambda i, j: (i, 0)),
                  pl.BlockSpec((d, COL_TILE), lambda i, j: (0, blk0 + j))],
        out_specs=pl.BlockSpec((tm, COL_TILE), lambda i, j: (i, j)),
        out_shape=jax.ShapeDtypeStruct((t, n_cols), F32),
        compiler_params=_params(("arbitrary", "arbitrary")),
        name="in_proj",
    )(h, w_in)


def _in_proj_all(x, sc, sh, g, w_in, *, rows_per_mod, tm_norm, tm, hi):
    d = x.shape[1]
    aw2 = 2 * A_GROUPS * CHUNK
    qkv = 3 * N_HEADS * HEAD_DIM
    h = _mix_norm(x, sc, sh, g, rows_per_mod=rows_per_mod, tm=tm_norm, hi=hi)
    z_a = _in_proj(h, w_in, 0, aw2, jax.nn.gelu, tm=tm, hi=hi)
    z_qkv = _in_proj(h, w_in, aw2, qkv, None, tm=tm, hi=hi)
    z_gate = _in_proj(h, w_in, aw2 + qkv, 2 * d, jax.nn.sigmoid, tm=tm, hi=hi)
    return z_a, z_qkv, z_gate


def _attend(q, kc, kp, vc, vp, slope, dil, prev_bias):
    row = lax.broadcasted_iota(I32, (CHUNK, CHUNK), 0)
    col = lax.broadcasted_iota(I32, (CHUNK, CHUNK), 1)
    pen_c = jnp.where(col <= row, -slope * ((row - col) * dil).astype(F32), NEG)
    pen_p = jnp.where(col >= row, -slope * ((row + CHUNK - col) * dil).astype(F32), NEG)
    scale = HEAD_DIM ** -0.5
    qk = (((2,), (2,)), ((0,), (0,)))
    pv = (((2,), (1,)), ((0,), (0,)))
    s_c = lax.dot_general(q, kc, qk, preferred_element_type=F32) * scale + pen_c[None]
    s_p = lax.dot_general(q, kp, qk, preferred_element_type=F32) * scale + pen_p[None] + prev_bias
    m = jnp.max(jnp.maximum(s_c, s_p), axis=2, keepdims=True)
    p_c = jnp.exp(s_c - m)
    p_p = jnp.exp(s_p - m)
    l = jnp.sum(p_c + p_p, axis=2, keepdims=True)
    o = (lax.dot_general((p_c / l).astype(BF16), vc, pv, preferred_element_type=F32)
         + lax.dot_general((p_p / l).astype(BF16), vp, pv, preferred_element_type=F32))
    return o, m + jnp.log(l)


def _dil_attn_kernel(slope_ref, *refs):
    ins = refs[:5 * N_GROUPS]
    o_ref = refs[5 * N_GROUPS]
    acc_refs = refs[5 * N_GROUPS + 1:5 * N_GROUPS + 1 + N_GROUPS]
    lse_refs = refs[5 * N_GROUPS + 1 + N_GROUPS:]
    head = pl.program_id(1)
    i = pl.program_id(2)
    no_prev = jnp.where(i > 0, 0.0, NEG)

    for g, (win, dil) in enumerate(DIL_GROUPS):
        q_ref, k_ref, v_ref, kp_ref, vp_ref = ins[5 * g:5 * g + 5]
        acc_ref, lse_ref = acc_refs[g], lse_refs[g]
        slope = slope_ref[g * HEADS_PER_GROUP + head]
        span = CHUNK * dil
        n_blocks = ATTN_BLOCK // CHUNK

        def rows(start, dil=dil):
            return pl.ds(start, CHUNK) if dil == 1 else pl.ds(start, CHUNK, stride=dil)

        cur = [rows((b // dil) * span + b % dil) for b in range(n_blocks)]
        prev = [rows((b // dil - 1) * span + b % dil) for b in range(n_blocks)]

        def gather(ref, first_ref=None):
            tiles = []
            for b in range(n_blocks):
                if first_ref is None:
                    tiles.append(ref[cur[b], :])
                elif b < dil:
                    tiles.append(first_ref[rows(b), :])
                else:
                    tiles.append(ref[prev[b], :])
            return jnp.stack(tiles).astype(BF16)

        first_blocks = lax.broadcasted_iota(I32, (n_blocks, 1, 1), 0) < dil
        o, lse = _attend(gather(q_ref), gather(k_ref), gather(k_ref, kp_ref), gather(v_ref), gather(v_ref, vp_ref),
                         slope, dil, jnp.where(first_blocks, no_prev, 0.0))
        for b in range(n_blocks):
            acc_ref[cur[b], :] = o[b]
            lse_ref[cur[b], :] = jnp.broadcast_to(lse[b], (CHUNK, HEAD_DIM))

    ls = [r[...] for r in lse_refs]
    m = jnp.maximum(jnp.maximum(ls[0], ls[1]), ls[2])
    e = [jnp.exp(l - m) for l in ls]
    den = e[0] + e[1] + e[2]
    o_ref[...] = (acc_refs[0][...] * (e[0] / den) + acc_refs[1][...] * (e[1] / den)
                  + acc_refs[2][...] * (e[2] / den))


def _dil_attn(z, bsz, seq):
    assert seq % ATTN_BLOCK == 0
    nblk = seq // ATTN_BLOCK
    a_blocks = 0
    in_specs = [pl.BlockSpec(memory_space=pltpu.SMEM)]
    for g, (win, dil) in enumerate(DIL_GROUPS):
        span = CHUNK * dil
        per_blk = ATTN_BLOCK // span
        for which in range(3):
            cb = a_blocks + (which * N_GROUPS + g) * HEADS_PER_GROUP
            in_specs.append(pl.BlockSpec((ATTN_BLOCK, HEAD_DIM), lambda b, h, i, cb=cb: (b * nblk + i, cb + h)))
        for which in (1, 2):
            cb = a_blocks + (which * N_GROUPS + g) * HEADS_PER_GROUP
            in_specs.append(pl.BlockSpec(
                (span, HEAD_DIM),
                lambda b, h, i, cb=cb, per_blk=per_blk: (jnp.maximum((b * nblk + i) * per_blk - 1, 0), cb + h)))
    slopes = jnp.asarray(ALIBI_SLOPES, F32)
    blk = pltpu.VMEM((ATTN_BLOCK, HEAD_DIM), F32)
    return pl.pallas_call(
        _dil_attn_kernel,
        grid=(bsz, HEADS_PER_GROUP, nblk),
        in_specs=in_specs,
        out_specs=pl.BlockSpec((ATTN_BLOCK, HEAD_DIM), lambda b, h, i: (b * nblk + i, h)),
        out_shape=jax.ShapeDtypeStruct((bsz * seq, GROUP_COLS), F32),
        scratch_shapes=[blk] * (2 * N_GROUPS),
        compiler_params=_params(("arbitrary", "arbitrary", "arbitrary")),
        name="dil_attn",
    )(slopes, *([z] * (5 * N_GROUPS)))


def _sample_attn_kernel(pen_ref, q_ref, k_ref, v_ref, c0_ref, c1_ref, c2_ref, o_ref):
    caches = (c0_ref, c1_ref, c2_ref)
    scale = HEAD_DIM ** -0.5
    heads = [(g, h) for g in range(N_GROUPS) for h in range(HEADS_PER_GROUP)]

    def head_rows(ref):
        return jnp.stack([ref[0, :, i * HEAD_DIM:(i + 1) * HEAD_DIM] for i in range(N_HEADS)])

    q = head_rows(q_ref)
    k_new = head_rows(k_ref)
    v_new = head_rows(v_ref)
    k_buf = jnp.stack([caches[g][:, 0, h, :] for g, h in heads])
    v_buf = jnp.stack([caches[g][:, 1, h, :] for g, h in heads])
    q8 = jnp.broadcast_to(q, (N_HEADS, SUBLANES, HEAD_DIM))
    qk = (((2,), (2,)), ((0,), (0,)))
    pv = (((2,), (1,)), ((0,), (0,)))
    s_buf = lax.dot_general(q8, k_buf, qk, precision=HIGHEST, preferred_element_type=F32)[:, 0:1, :]
    s_buf = s_buf * scale - pen_ref[...]
    s_new = jnp.sum(q * k_new, axis=2, keepdims=True) * scale
    m = jnp.maximum(jnp.max(s_buf, axis=2, keepdims=True), s_new)
    p_buf = jnp.exp(s_buf - m)
    p_new = jnp.exp(s_new - m)
    l = jnp.sum(p_buf, axis=2, keepdims=True) + p_new
    pb8 = jnp.broadcast_to(p_buf / l, (N_HEADS, SUBLANES, CHUNK))
    o = lax.dot_general(pb8, v_buf, pv, precision=HIGHEST, preferred_element_type=F32)[:, 0:1, :]
    o = o + (p_new / l) * v_new
    lse = m + jnp.log(l)
    n = HEADS_PER_GROUP
    ls = [lse[g * n:(g + 1) * n] for g in range(N_GROUPS)]
    mm = jnp.maximum(jnp.maximum(ls[0], ls[1]), ls[2])
    e = [jnp.exp(x - mm) for x in ls]
    den = e[0] + e[1] + e[2]
    out = o[0:n] * (e[0] / den) + o[n:2 * n] * (e[1] / den) + o[2 * n:3 * n] * (e[2] / den)
    for h in range(HEADS_PER_GROUP):
        o_ref[0, :, h * HEAD_DIM:(h + 1) * HEAD_DIM] = out[h]


def _sample_attn(zs, caches):
    n, cols = zs.shape
    z3 = zs.reshape(n, 1, cols)
    a_blocks = 0
    qkv_cols = N_GROUPS * GROUP_COLS
    c_views = []
    c_specs = []
    for g, (win, dil) in enumerate(DIL_GROUPS):
        c = caches[g]
        assert c.shape[1] == win, "cache must hold exactly one window"
        c_views.append(c.reshape(n, win // dil, dil, 2, HEADS_PER_GROUP, HEAD_DIM))
        c_specs.append(pl.BlockSpec((None, win // dil, None, 2, HEADS_PER_GROUP, HEAD_DIM),
                                    lambda b: (b, 0, 0, 0, 0, 0)))
    q0 = a_blocks * GROUP_COLS
    steps = jnp.arange(CHUNK, 0, -1, dtype=F32)
    pen = jnp.stack([ALIBI_SLOPES[g * HEADS_PER_GROUP + h] * dil * steps
                     for g, (win, dil) in enumerate(DIL_GROUPS) for h in range(HEADS_PER_GROUP)])
    assert all(win // dil == CHUNK for win, dil in DIL_GROUPS)
    row_spec = pl.BlockSpec((1, 1, qkv_cols), lambda b: (b, 0, 0))
    out = pl.pallas_call(
        _sample_attn_kernel,
        grid=(n,),
        in_specs=[pl.BlockSpec((N_HEADS, 1, CHUNK), lambda b: (0, 0, 0)), row_spec, row_spec, row_spec] + c_specs,
        out_specs=pl.BlockSpec((1, 1, GROUP_COLS), lambda b: (b, 0, 0)),
        out_shape=jax.ShapeDtypeStruct((n, 1, GROUP_COLS), F32),
        compiler_params=_params(("arbitrary",)),
        name="sample_attn",
    )(pen.reshape(N_HEADS, 1, CHUNK), z3[:, :, q0:q0 + qkv_cols], z3[:, :, q0 + qkv_cols:q0 + 2 * qkv_cols],
      z3[:, :, q0 + 2 * qkv_cols:q0 + 3 * qkv_cols], *c_views)
    return out.reshape(n, GROUP_COLS)


def _layer_norm(v, g, b):
    mu = jnp.mean(v, axis=-1, keepdims=True)
    var = jnp.mean(jnp.square(v - mu), axis=-1, keepdims=True)
    return (v - mu) * lax.rsqrt(var + EPS) * g + b


def _chunk_gate_kernel(u_ref, v_ref, lg_ref, lb_ref, ws_ref, bs_ref, o_ref, *, n_chunks):
    va = _layer_norm(v_ref[...], lg_ref[...], lb_ref[...]).astype(BF16)
    row = lax.broadcasted_iota(I32, (CHUNK, CHUNK), 0)
    col = lax.broadcasted_iota(I32, (CHUNK, CHUNK), 1)
    for g in range(A_GROUPS):
        w = jnp.where(col <= row, ws_ref[g], 0.0).astype(BF16)
        gs = slice(g * CHUNK, (g + 1) * CHUNK)
        for c in range(n_chunks):
            rs = slice(c * CHUNK, (c + 1) * CHUNK)
            mixed = jnp.dot(w, va[rs, gs], preferred_element_type=F32) + bs_ref[g]
            o_ref[rs, gs] = (u_ref[rs, gs] * mixed).astype(o_ref.dtype)


def _chunk_gate(z, ln_g, ln_b, w_s, b_s, *, tm):
    t = z.shape[0]
    aw = A_GROUPS * CHUNK
    bs_full = jnp.broadcast_to(b_s[:, :, None], (A_GROUPS, CHUNK, CHUNK))
    kern = functools.partial(_chunk_gate_kernel, n_chunks=tm // CHUNK)
    return pl.pallas_call(
        kern,
        grid=(t // tm,),
        in_specs=[pl.BlockSpec((tm, aw), lambda i: (i, 0)),
                  pl.BlockSpec((tm, aw), lambda i: (i, 1)),
                  pl.BlockSpec((1, aw), lambda i: (0, 0)),
                  pl.BlockSpec((1, aw), lambda i: (0, 0)),
                  pl.BlockSpec((A_GROUPS, CHUNK, CHUNK), lambda i: (0, 0, 0)),
                  pl.BlockSpec((A_GROUPS, CHUNK, CHUNK), lambda i: (0, 0, 0))],
        out_specs=pl.BlockSpec((tm, aw), lambda i: (i, 0)),
        out_shape=jax.ShapeDtypeStruct((t, aw), BF16),
        compiler_params=_params(("arbitrary",)),
        name="chunk_gate",
    )(z, z, ln_g.reshape(1, aw), ln_b.reshape(1, aw), w_s, bs_full)


def _sample_gate_kernel(u_ref, v_ref, lg_ref, lb_ref, w0_ref, b0_ref, a_ref, vn_ref):
    va = _layer_norm(v_ref[...], lg_ref[...], lb_ref[...])
    vn_ref[...] = va
    a_ref[...] = u_ref[...] * (w0_ref[...] * va + b0_ref[...])


def _sample_gate(zs, ln_g, ln_b, w_s, b_s):
    n = zs.shape[0]
    aw = A_GROUPS * CHUNK
    w0 = jnp.repeat(w_s[:, 0, 0], CHUNK).reshape(1, aw)
    b0 = jnp.repeat(b_s[:, 0], CHUNK).reshape(1, aw)
    vec = pl.BlockSpec((1, aw), lambda i: (0, 0))
    return pl.pallas_call(
        _sample_gate_kernel,
        grid=(1,),
        in_specs=[pl.BlockSpec((n, aw), lambda i: (0, 0)), pl.BlockSpec((n, aw), lambda i: (0, 1)),
                  vec, vec, vec, vec],
        out_specs=[pl.BlockSpec((n, aw), lambda i: (0, 0)), pl.BlockSpec((n, aw), lambda i: (0, 0))],
        out_shape=[jax.ShapeDtypeStruct((n, aw), F32), jax.ShapeDtypeStruct((n, aw), F32)],
        compiler_params=_params(("arbitrary",)),
        name="sample_gate",
    )(zs, zs, ln_g.reshape(1, aw), ln_b.reshape(1, aw), w0, b0)


def _merge_kernel(a_ref, b_ref, ga_ref, gb_ref, pa_ref, pb_ref, out_ref, *, hi):
    ya = _dot(a_ref[...], pa_ref[...], hi)
    yb = _dot(b_ref[...], pb_ref[...], hi)
    out_ref[...] = (ga_ref[...] * ya + gb_ref[...] * yb).astype(out_ref.dtype)


def _merge(a_out, b_out, z, p_a, p_b, *, tm, hi):
    t, aw = a_out.shape
    d = p_a.shape[1]
    tn = COL_TILE
    ga_blk0 = 0
    gb_blk0 = d // tn
    return pl.pallas_call(
        functools.partial(_merge_kernel, hi=hi),
        grid=(t // tm, d // tn),
        in_specs=[pl.BlockSpec((tm, aw), lambda i, j: (i, 0)),
                  pl.BlockSpec((tm, GROUP_COLS), lambda i, j: (i, 0)),
                  pl.BlockSpec((tm, tn), lambda i, j: (i, ga_blk0 + j)),
                  pl.BlockSpec((tm, tn), lambda i, j: (i, gb_blk0 + j)),
                  pl.BlockSpec((aw, tn), lambda i, j: (0, j)),
                  pl.BlockSpec((GROUP_COLS, tn), lambda i, j: (0, j))],
        out_specs=pl.BlockSpec((tm, tn), lambda i, j: (i, j)),
        out_shape=jax.ShapeDtypeStruct((t, d), F32 if hi else BF16),
        compiler_params=_params(("arbitrary", "arbitrary")),
        name="merge",
    )(a_out, b_out, z, z, p_a, p_b)


def _out_proj_kernel(m_ref, w_ref, x_ref, gt_ref, o_ref, *, hi):
    o_ref[...] = x_ref[...] + gt_ref[0] * _dot(m_ref[...], w_ref[...], hi)


def _out_proj(merged, w_o, x, gt, *, rows_per_mod, tm, hi):
    t, d = x.shape
    tn = COL_TILE
    mod_rows = gt.shape[1]
    return pl.pallas_call(
        functools.partial(_out_proj_kernel, hi=hi),
        grid=(t // tm, d // tn),
        in_specs=[pl.BlockSpec((tm, d), lambda i, j: (i, 0)),
                  pl.BlockSpec((d, tn), lambda i, j: (0, j)),
                  pl.BlockSpec((tm, tn), lambda i, j: (i, j)),
                  pl.BlockSpec((1, mod_rows, tn), lambda i, j: (i * tm // rows_per_mod, 0, j))],
        out_specs=pl.BlockSpec((tm, tn), lambda i, j: (i, j)),
        out_shape=jax.ShapeDtypeStruct((t, d), F32),
        compiler_params=_params(("arbitrary", "arbitrary")),
        name="out_proj",
    )(merged, w_o, x, gt)


def _pack_pair(lo, hi, pair_ref):
    n = lo.shape[0]
    pair_ref[pl.ds(0, n, stride=2), :] = lo
    pair_ref[pl.ds(1, n, stride=2), :] = hi
    return pltpu.bitcast(pair_ref[0:2 * n, :].astype(BF16), U32)


def _unpack_pair(w, pair_ref):
    n = w.shape[0]
    pair_ref[0:2 * n, :] = pltpu.bitcast(w, BF16).astype(F32)
    return pair_ref[pl.ds(0, n, stride=2), :].astype(BF16), pair_ref[pl.ds(1, n, stride=2), :].astype(BF16)


def _ffn_norm_kernel(x_ref, sc_ref, sh_ref, g_ref, wr_ref, br_ref, hp_ref, e_ref, w_ref, pair_ref, *, hi):
    h = _mod_rms(x_ref[...], g_ref[...], sc_ref[0], sh_ref[0])
    tm, d = h.shape
    for s in range(d // (2 * LANES)):
        even = h[:, (2 * s) * LANES:(2 * s + 1) * LANES]
        odd = h[:, (2 * s + 1) * LANES:(2 * s + 2) * LANES]
        hp_ref[pl.ds(s, tm, stride=SUBLANES), :] = _pack_pair(even, odd, pair_ref)
    if hi:
        logits = _dot(h, wr_ref[...], True)
    else:
        wr = wr_ref[...]
        h_hi, w_hi = h.astype(BF16), wr.astype(BF16)
        h_lo = (h - h_hi.astype(F32)).astype(BF16)
        w_lo = (wr - w_hi.astype(F32)).astype(BF16)
        logits = (jnp.dot(h_hi, w_hi, preferred_element_type=F32) + jnp.dot(h_hi, w_lo, preferred_element_type=F32)
                  + jnp.dot(h_lo, w_hi, preferred_element_type=F32))
    logits = logits + br_ref[...]
    lane = lax.broadcasted_iota(I32, logits.shape, 1)
    lane_f = lane.astype(F32)
    vals, idxs = [], []
    for _ in range(TOP_K):
        m = jnp.max(logits, axis=1, keepdims=True)
        idx = jnp.min(jnp.where(logits == m, lane_f, float(logits.shape[1])), axis=1, keepdims=True)
        vals.append(m)
        idxs.append(idx)
        logits = jnp.where(lane_f == idx, 2.0 * NEG, logits)
    es = [jnp.exp(v - vals[0]) for v in vals]
    den = es[0] + es[1] + es[2] + es[3]
    e_out = jnp.full(lane.shape, -1.0, F32)
    w_out = jnp.zeros(lane.shape, F32)
    for k in range(TOP_K):
        e_out = jnp.where(lane == k, idxs[k], e_out)
        w_out = jnp.where(lane == k, es[k] / den, w_out)
    e_ref[...] = e_out.astype(I32)
    w_ref[...] = w_out


def _ffn_norm(x1, sc, sh, g, w_router, b_router, *, rows_per_mod, tm, hi):
    t, d = x1.shape
    assert d == 2 * LANES * SUBLANES, "one packed row must be exactly one (8, 128) tile"
    ne = w_router.shape[1]
    wr = jnp.zeros((d, LANES), F32).at[:, :ne].set(w_router)
    br = jnp.full((1, LANES), NEG, F32).at[0, :ne].set(b_router)
    mod_rows = sc.shape[1]
    mod_spec = pl.BlockSpec((1, mod_rows, d), lambda i: (i * tm // rows_per_mod, 0, 0))
    return pl.pallas_call(
        functools.partial(_ffn_norm_kernel, hi=hi),
        grid=(t // tm,),
        in_specs=[pl.BlockSpec((tm, d), lambda i: (i, 0)), mod_spec, mod_spec,
                  pl.BlockSpec((1, d), lambda i: (0, 0)),
                  pl.BlockSpec((d, LANES), lambda i: (0, 0)),
                  pl.BlockSpec((1, LANES), lambda i: (0, 0))],
        out_specs=[pl.BlockSpec((tm * SUBLANES, LANES), lambda i: (i, 0)),
                   pl.BlockSpec((tm, LANES), lambda i: (i, 0)),
                   pl.BlockSpec((tm, LANES), lambda i: (i, 0))],
        out_shape=[jax.ShapeDtypeStruct((t * SUBLANES, LANES), U32),
                   jax.ShapeDtypeStruct((t, LANES), I32),
                   jax.ShapeDtypeStruct((t, LANES), F32)],
        scratch_shapes=[pltpu.VMEM((2 * tm, LANES), F32)],
        compiler_params=_params(("arbitrary",)),
        name="ffn_norm",
    )(x1, sc, sh, g.reshape(1, d), wr, br)


def _route_kernel(e_ref, pos_ref, cnt_ref, tri_ref, run_ref):
    phase = pl.program_id(0)
    i = pl.program_id(1)
    e = e_ref[...]
    tm = e.shape[0]
    lane = lax.broadcasted_iota(I32, (tm, LANES), 1)
    hits = [lane == e[:, k:k + 1] for k in range(TOP_K)]
    chosen = jnp.zeros((tm, LANES), F32)
    for k in range(TOP_K):
        chosen = jnp.where(hits[k], 1.0, chosen)
    col_count = jnp.sum(chosen, axis=0, keepdims=True)

    @pl.when((phase == 0) & (i == 0))
    def _():
        cnt_ref[...] = jnp.zeros(cnt_ref.shape, F32)
        r = lax.broadcasted_iota(I32, (tm, tm), 0)
        c = lax.broadcasted_iota(I32, (tm, tm), 1)
        tri_ref[...] = jnp.where(c < r, 1.0, 0.0).astype(BF16)

    @pl.when(phase == 0)
    def _():
        cnt_ref[...] = cnt_ref[...] + col_count

    @pl.when((phase == 1) & (i == 0))
    def _():
        tiles = jnp.floor((cnt_ref[...] + (MOE_TILE - 1)) * (1.0 / MOE_TILE))
        r = lax.broadcasted_iota(I32, (LANES, LANES), 0)
        c = lax.broadcasted_iota(I32, (LANES, LANES), 1)
        below = jnp.where(r < c, 1.0, 0.0)
        tiles8 = jnp.broadcast_to(tiles, (SUBLANES, LANES))
        run_ref[...] = _dot(tiles8, below, True)[0:1] * float(MOE_TILE)

    @pl.when(phase == 1)
    def _():
        before = jnp.dot(tri_ref[...], chosen.astype(BF16), preferred_element_type=F32)
        dest = run_ref[...] + before
        out = jnp.zeros((tm, LANES), F32)
        for k in range(TOP_K):
            p = jnp.sum(jnp.where(hits[k], dest, 0.0), axis=1, keepdims=True)
            out = jnp.where(lane == k, p, out)
        pos_ref[...] = out.astype(I32)
        run_ref[...] = run_ref[...] + col_count


def _route(e_all):
    t = e_all.shape[0]
    steps = t // ROUTE_TILE
    return pl.pallas_call(
        _route_kernel,
        grid=(2, steps),
        in_specs=[pl.BlockSpec((ROUTE_TILE, LANES), lambda p, i: (i, 0))],
        out_specs=[pl.BlockSpec((ROUTE_TILE, LANES), lambda p, i: (i * p, 0)),
                   pl.BlockSpec((1, LANES), lambda p, i: (0, 0))],
        out_shape=[jax.ShapeDtypeStruct((t, LANES), I32), jax.ShapeDtypeStruct((1, LANES), F32)],
        scratch_shapes=[pltpu.VMEM((ROUTE_TILE, ROUTE_TILE), BF16), pltpu.VMEM((1, LANES), F32)],
        compiler_params=_params(("arbitrary", "arbitrary")),
        name="route",
    )(e_all)


def _row_scatter_kernel(pos_ref, pos_s_ref, pad_base_ref, pad_cnt_ref, hp_ref, hs_ref, xs_hbm, sem, *,
                        n_prompt_steps, n_sample):
    i = pl.program_id(0)

    def rows(first, n=1):
        return pl.ds(pl.multiple_of(first * SUBLANES, SUBLANES), n * SUBLANES)

    def scatter(src_ref, idx_ref, n_tok):
        def body(tt, carry):
            for k in range(TOP_K):
                pltpu.make_async_copy(src_ref.at[rows(tt)], xs_hbm.at[rows(idx_ref[0, 0, tt * TOP_K + k])],
                                      sem).start(priority=k % 2)
            return carry
        lax.fori_loop(0, n_tok, body, 0, unroll=2)
        for _ in range(TOP_K):
            pltpu.make_async_copy(src_ref.at[rows(0, n_tok)], xs_hbm.at[rows(0, n_tok)], sem).wait()

    @pl.when(i < n_prompt_steps)
    def _():
        scatter(hp_ref, pos_ref, ROUTE_TILE)

    @pl.when(i == n_prompt_steps)
    def _():
        scatter(hs_ref, pos_s_ref, n_sample)

        def fill(dst_row, size):
            return pltpu.make_async_copy(hp_ref.at[rows(0, size)], xs_hbm.at[rows(dst_row, size)], sem)

        def per_segment(wait):
            def body(e, carry):
                cnt = pad_cnt_ref[e]
                base = pad_base_ref[e]

                def whole(r, c):
                    cp = fill(base + (cnt % MOE_TILE) + r * MOE_TILE, MOE_TILE)
                    cp.wait() if wait else cp.start()
                    return c
                lax.fori_loop(0, cnt // MOE_TILE, whole, 0)
                size = MOE_TILE // 2
                while size >= 1:
                    @pl.when((cnt & size) != 0)
                    def _(size=size):
                        cp = fill(base + (cnt & (size - 1)), size)
                        cp.wait() if wait else cp.start()
                    size //= 2
                return carry
            return body
        lax.fori_loop(0, pad_cnt_ref.shape[0], per_segment(False), 0)
        lax.fori_loop(0, pad_cnt_ref.shape[0], per_segment(True), 0)


def _row_scatter(hp_p, hp_s, pos_p, pos_s, pad_base, pad_cnt, n_rows):
    t_p = hp_p.shape[0] // SUBLANES
    n_s = hp_s.shape[0] // SUBLANES
    assert ROUTE_TILE >= MOE_TILE, "padding blocks are copied from one token tile"
    steps = t_p // ROUTE_TILE
    tile_rows = ROUTE_TILE * TOP_K
    return pl.pallas_call(
        functools.partial(_row_scatter_kernel, n_prompt_steps=steps, n_sample=n_s),
        grid=(steps + 1,),
        in_specs=[pl.BlockSpec((1, 1, tile_rows), lambda i: (jnp.minimum(i, steps - 1), 0, 0),
                               memory_space=pltpu.SMEM),
                  pl.BlockSpec((1, 1, n_s * TOP_K), lambda i: (0, 0, 0), memory_space=pltpu.SMEM),
                  pl.BlockSpec(memory_space=pltpu.SMEM),
                  pl.BlockSpec(memory_space=pltpu.SMEM),
                  pl.BlockSpec((ROUTE_TILE * SUBLANES, LANES), lambda i: (jnp.minimum(i, steps - 1), 0)),
                  pl.BlockSpec((n_s * SUBLANES, LANES), lambda i: (0, 0))],
        out_specs=pl.BlockSpec(memory_space=pl.ANY),
        out_shape=jax.ShapeDtypeStruct((n_rows * SUBLANES, LANES), U32),
        scratch_shapes=[pltpu.SemaphoreType.DMA(())],
        compiler_params=_params(("arbitrary",)),
        name="row_scatter",
    )(pos_p.reshape(steps, 1, tile_rows), pos_s.reshape(1, 1, n_s * TOP_K), pad_base, pad_cnt, hp_p, hp_s)


def _expert_weights(seg_ref, sege_ref, nu_ref, fetch, on_ready):
    j = pl.program_id(0)
    t = pl.program_id(1)
    n_j = pl.num_programs(0)
    seg = seg_ref[t]
    n_seg = nu_ref[1]
    first = (t < nu_ref[0]) & ((t == 0) | (seg != seg_ref[jnp.maximum(t - 1, 0)]))
    k = j * n_seg + seg
    slot = lax.rem(k, 2)

    @pl.when(first)
    def _():
        @pl.when(k == 0)
        def _():
            for cp in fetch(sege_ref[0], 0, 0):
                cp.start()

        more = seg + 1 < n_seg

        @pl.when(more | (j + 1 < n_j))
        def _():
            for cp in fetch(sege_ref[jnp.where(more, seg + 1, 0)], jnp.where(more, j, j + 1), 1 - slot):
                cp.start()

        for cp in fetch(sege_ref[seg], j, slot):
            cp.wait()
        on_ready(slot)


def _moe_up_kernel(te_ref, nv_ref, seg_ref, sege_ref, nu_ref, x_ref, wg_hbm, wu_hbm, bg_ref, bu_ref, h_ref,
                   wg_buf, wu_buf, sem, wg_s, wu_s, xb_s, pair_ref):
    t = pl.program_id(1)
    tf = wg_s.shape[1]

    def fetch(e, j, slot):
        cols = pl.ds(pl.multiple_of(j * tf, tf), tf)
        return [pltpu.make_async_copy(wg_hbm.at[e, :, cols], wg_buf.at[slot], sem.at[slot]),
                pltpu.make_async_copy(wu_hbm.at[e, :, cols], wu_buf.at[slot], sem.at[slot])]

    def on_ready(slot):
        wg_s[...] = wg_buf[slot].astype(BF16)
        wu_s[...] = wu_buf[slot].astype(BF16)

    _expert_weights(seg_ref, sege_ref, nu_ref, fetch, on_ready)

    def compute(rows):
        for s in range(SUBLANES):
            lo, hi = _unpack_pair(x_ref[pl.ds(s, rows, stride=SUBLANES), :], pair_ref)
            xb_s[0:rows, (2 * s) * LANES:(2 * s + 1) * LANES] = lo
            xb_s[0:rows, (2 * s + 1) * LANES:(2 * s + 2) * LANES] = hi
        x = xb_s[0:rows, :]
        gt = jnp.dot(x, wg_s[...], preferred_element_type=F32) + bg_ref[0]
        up = jnp.dot(x, wu_s[...], preferred_element_type=F32) + bu_ref[0]
        gt = jnp.minimum(gt, SWIGLU_LIMIT)
        up = jnp.clip(up, -SWIGLU_LIMIT, SWIGLU_LIMIT)
        h_ref[0:rows, :] = ((up + 1.0) * gt * jax.nn.sigmoid(SWIGLU_ALPHA * gt)).astype(h_ref.dtype)

    quarters = jnp.where(t < nu_ref[0], (nv_ref[t] + MOE_QUARTER - 1) // MOE_QUARTER, 0)
    for n_q in range(MOE_TILE // MOE_QUARTER + 1):
        @pl.when(quarters == n_q)
        def _(rows=n_q * MOE_QUARTER):
            if rows > 0:
                compute(rows)
            if rows < MOE_TILE:
                h_ref[rows:, :] = jnp.zeros((MOE_TILE - rows, h_ref.shape[1]), h_ref.dtype)


def _moe_down_kernel(te_ref, nv_ref, seg_ref, sege_ref, nu_ref, h_ref, wd_hbm, bd_ref, y_ref, wd_buf, sem, wd_s):
    t = pl.program_id(1)
    n_sub = y_ref.shape[0] // MOE_TILE
    tn = wd_s.shape[1]

    def fetch(e, j, slot):
        cols = pl.ds(pl.multiple_of(j * tn, tn), tn)
        return [pltpu.make_async_copy(wd_hbm.at[e, :, cols], wd_buf.at[slot], sem.at[slot])]

    def on_ready(slot):
        wd_s[...] = wd_buf[slot].astype(BF16)

    _expert_weights(seg_ref, sege_ref, nu_ref, fetch, on_ready)

    def store(rows, y):
        for c in range(n_sub):
            y_ref[pl.ds(c, rows, stride=n_sub), :] = y[:, c * LANES:(c + 1) * LANES]

    def compute(rows):
        store(rows, jnp.dot(h_ref[0:rows, :], wd_s[...], preferred_element_type=F32) + bd_ref[0])

    quarters = jnp.where(t < nu_ref[0], (nv_ref[t] + MOE_QUARTER - 1) // MOE_QUARTER, 0)
    for n_q in range(MOE_TILE // MOE_QUARTER + 1):
        @pl.when(quarters == n_q)
        def _(rows=n_q * MOE_QUARTER):
            if rows > 0:
                compute(rows)
            if rows < MOE_TILE:
                y_ref[rows * n_sub:, :] = jnp.zeros(((MOE_TILE - rows) * n_sub, LANES), F32)


def _moe_experts(x_sorted, plan, w_gate, b_gate, w_up, b_up, w_down, b_down):
    rows = x_sorted.shape[0] // SUBLANES
    ne, d, f = w_gate.shape
    nt = rows // MOE_TILE
    tf = 1024
    tn = d
    n_sub = d // LANES
    n_plan = len(plan)

    def row_map(j, t, te, nv, seg, sege, nu):
        return (jnp.minimum(t, nu[0] - 1), 0)

    def b_map(j, t, te, nv, seg, sege, nu):
        return (te[t], 0, j)

    hbm = pl.BlockSpec(memory_space=pl.ANY)
    h = pl.pallas_call(
        _moe_up_kernel,
        grid_spec=pltpu.PrefetchScalarGridSpec(
            num_scalar_prefetch=n_plan,
            grid=(f // tf, nt),
            in_specs=[pl.BlockSpec((MOE_TILE * SUBLANES, LANES), row_map), hbm, hbm,
                      pl.BlockSpec((1, 1, tf), b_map),
                      pl.BlockSpec((1, 1, tf), b_map)],
            out_specs=pl.BlockSpec((MOE_TILE, tf), lambda j, t, *_: (t, j)),
            scratch_shapes=[pltpu.VMEM((2, d, tf), F32), pltpu.VMEM((2, d, tf), F32), pltpu.SemaphoreType.DMA((2,)),
                            pltpu.VMEM((d, tf), BF16), pltpu.VMEM((d, tf), BF16), pltpu.VMEM((MOE_TILE, d), BF16),
                            pltpu.VMEM((2 * MOE_TILE, LANES), F32)]),
        out_shape=jax.ShapeDtypeStruct((rows, f), BF16),
        compiler_params=_params(("arbitrary", "arbitrary")),
        name="moe_up",
    )(*plan, x_sorted, w_gate, w_up, b_gate.reshape(ne, 1, f), b_up.reshape(ne, 1, f))

    y = pl.pallas_call(
        _moe_down_kernel,
        grid_spec=pltpu.PrefetchScalarGridSpec(
            num_scalar_prefetch=n_plan,
            grid=(d // tn, nt),
            in_specs=[pl.BlockSpec((MOE_TILE, f), row_map), hbm,
                      pl.BlockSpec((1, 1, tn), b_map)],
            out_specs=pl.BlockSpec((MOE_TILE * n_sub, LANES), lambda j, t, *_: (t, 0)),
            scratch_shapes=[pltpu.VMEM((2, f, tn), F32), pltpu.SemaphoreType.DMA((2,)), pltpu.VMEM((f, tn), BF16)]),
        out_shape=jax.ShapeDtypeStruct((rows * n_sub, LANES), F32),
        compiler_params=_params(("arbitrary", "arbitrary")),
        name="moe_down",
    )(*plan, h, w_down, b_down.reshape(ne, 1, d))
    return y.reshape(rows, d // LANES, LANES)


def _row_copy(src_hbm, buf, sem, slot, src_row, dst_row, n_sub):
    dst = pl.ds(pl.multiple_of(dst_row * n_sub, SUBLANES), n_sub)
    return pltpu.make_async_copy(src_hbm.at[src_row], buf.at[slot, dst], sem.at[slot])


def _issue_rows(idx_ref, src_hbm, buf, sem, slot, n_rows, n_sub):
    def body(pair, carry):
        for p in range(2):
            r = 2 * pair + p
            _row_copy(src_hbm, buf, sem, slot, idx_ref[0, 0, r], r, n_sub).start(priority=p)
        return carry
    lax.fori_loop(0, n_rows // 2, body, 0, unroll=4)


def _moe_combine_kernel(idx_ref, nxt_ref, wt_ref, y_hbm, x_ref, gt_ref, g_ref, o_ref, buf, sem, acc_ref, *, tok):
    i = pl.program_id(0)
    n = pl.num_programs(0)
    n_rows = TOP_K * tok
    n_sub = y_hbm.shape[1]
    slot = lax.rem(i, 2)

    @pl.when(i == 0)
    def _():
        _issue_rows(idx_ref, y_hbm, buf, sem, 0, n_rows, n_sub)

    for s in range(2):
        @pl.when((i + 1 < n) & (slot == s))
        def _():
            _issue_rows(nxt_ref, y_hbm, buf, sem, 1 - s, n_rows, n_sub)

    pltpu.make_async_copy(buf.at[slot], buf.at[slot], sem.at[slot]).wait()

    def per_token(t, carry):
        acc = None
        for k in range(TOP_K):
            first = pl.multiple_of((k * tok + t) * n_sub, n_sub)
            part = wt_ref[0, 0, k * tok + t] * buf[slot, pl.ds(first, n_sub), :]
            acc = part if acc is None else acc + part
        acc_ref[pl.ds(pl.multiple_of(t * n_sub, n_sub), n_sub), :] = acc
        return carry
    lax.fori_loop(0, tok, per_token, 0, unroll=4)

    sumsq = jnp.zeros((tok, 1), F32)
    for c in range(n_sub):
        cs = slice(c * LANES, (c + 1) * LANES)
        x = x_ref[:, cs] + gt_ref[0, :, cs] * acc_ref[pl.ds(c, tok, stride=n_sub), :]
        o_ref[:, cs] = x
        sumsq = sumsq + jnp.sum(x * x, axis=1, keepdims=True)
    o_ref[...] = o_ref[...] * lax.rsqrt(sumsq / (n_sub * LANES) + EPS) * g_ref[...]


def _moe_combine(y_rows, pos, top_w, x1, gt, g_final, *, rows_per_mod, tok):
    t, d = x1.shape
    n_sub = y_rows.shape[1]
    steps = t // tok
    n_rows = TOP_K * tok
    def k_major(a):
        return a[:, :TOP_K].reshape(steps, tok, TOP_K).transpose(0, 2, 1).reshape(steps, 1, n_rows)

    idx3 = k_major(pos)
    mod_rows = gt.shape[1]
    idx_blk = (1, 1, n_rows)
    return pl.pallas_call(
        functools.partial(_moe_combine_kernel, tok=tok),
        grid=(steps,),
        in_specs=[pl.BlockSpec(idx_blk, lambda i: (i, 0, 0), memory_space=pltpu.SMEM),
                  pl.BlockSpec(idx_blk, lambda i: (jnp.minimum(i + 1, steps - 1), 0, 0), memory_space=pltpu.SMEM),
                  pl.BlockSpec(idx_blk, lambda i: (i, 0, 0), memory_space=pltpu.SMEM),
                  pl.BlockSpec(memory_space=pl.ANY),
                  pl.BlockSpec((tok, d), lambda i: (i, 0)),
                  pl.BlockSpec((1, mod_rows, d), lambda i: (i * tok // rows_per_mod, 0, 0)),
                  pl.BlockSpec((1, d), lambda i: (0, 0))],
        out_specs=pl.BlockSpec((tok, d), lambda i: (i, 0)),
        out_shape=jax.ShapeDtypeStruct((t, d), F32),
        scratch_shapes=[pltpu.VMEM((2, n_rows * n_sub, LANES), F32), pltpu.SemaphoreType.DMA((2,)),
                        pltpu.VMEM((tok * n_sub, LANES), F32)],
        compiler_params=_params(("arbitrary",)),
        name="moe_combine",
    )(idx3, idx3, k_major(top_w), y_rows, x1, gt, g_final.reshape(1, d))


def _tile_table(counts, n_tiles):
    tiles_e = (counts + MOE_TILE - 1) // MOE_TILE
    tile_end = jnp.cumsum(tiles_e)
    tile_start = tile_end - tiles_e
    n_used = tile_end[-1]
    tile_ids = jnp.minimum(jnp.arange(n_tiles, dtype=I32), n_used - 1)
    tile_expert = jnp.minimum(jnp.sum(tile_end[None, :] <= tile_ids[:, None], axis=1), N_EXPERTS - 1).astype(I32)
    tile_valid = jnp.clip(counts[tile_expert] - (tile_ids - tile_start[tile_expert]) * MOE_TILE, 0, MOE_TILE)
    nonempty = tiles_e > 0
    seg_of_expert = jnp.cumsum(nonempty.astype(I32)) - 1
    n_seg = jnp.sum(nonempty.astype(I32))
    experts = jnp.arange(N_EXPERTS, dtype=I32)
    hit = nonempty[None, :] & (seg_of_expert[None, :] == jnp.arange(N_EXPERTS + 1, dtype=I32)[:, None])
    seg_expert = jnp.sum(jnp.where(hit, experts[None, :], 0), axis=1).astype(I32)
    tile_seg = seg_of_expert[tile_expert].astype(I32)
    plan = (tile_expert, tile_valid.astype(I32), tile_seg, seg_expert, jnp.stack([n_used, n_seg]).astype(I32))
    pad_base = jnp.concatenate([tile_start * MOE_TILE + counts, (n_used * MOE_TILE).reshape(1)])
    pad_cnt = jnp.concatenate([tiles_e * MOE_TILE - counts, ((n_tiles - n_used) * MOE_TILE).reshape(1)])
    return plan, pad_base.astype(I32), pad_cnt.astype(I32)


def _row_tile(t, want):
    tm = min(t, want)
    assert t % tm == 0
    return tm


def kernel(x_prompt, x_sample, cache_kv_g0, cache_kv_g1, cache_kv_g2, c_prompt, c_sample, w_ada, b_ada, g_mix, w_in, ln_g, ln_b, w_s, b_s, p_a, p_b, w_o, g_ffn, w_router, b_router, w_gate, b_gate, w_up, b_up, w_down, b_down, g_final):
    depth = w_ada.shape[0]
    assert depth == 1, "single-layer trunk"
    bsz, seq, d = x_prompt.shape
    n_s, dec_seq, _ = x_sample.shape
    assert dec_seq == 1, "one new position per sample"
    caches = (cache_kv_g0, cache_kv_g1, cache_kv_g2)
    l = 0
    cols = w_in.shape[2]
    t_p = bsz * seq
    aw = A_GROUPS * CHUNK

    n_c = bsz + n_s
    n_c_pad = -(-n_c // SUBLANES) * SUBLANES
    c_all = jnp.concatenate([c_prompt, c_sample, jnp.zeros((n_c_pad - n_c, d), F32)], axis=0)
    mod = _ada(c_all, w_ada[l], b_ada[l])
    mod_p = mod[:bsz].reshape(bsz, 1, N_ADA, d)
    mod_s = mod[bsz:n_c].reshape(1, n_s, N_ADA, d)
    sh1_p, sc1_p, gt1_p, sh2_p, sc2_p, gt2_p = (mod_p[:, :, k] for k in range(N_ADA))
    sh1_s, sc1_s, gt1_s, sh2_s, sc2_s, gt2_s = (mod_s[:, :, k] for k in range(N_ADA))

    xp = x_prompt.reshape(t_p, d)
    xs = x_sample.reshape(n_s, d)

    tm_big = _row_tile(seq, 1024)
    tm_mid = _row_tile(seq, 512)
    tm_huge = _row_tile(seq, 2048)
    za_p, zqkv_p, zg_p = _in_proj_all(xp, sc1_p, sh1_p, g_mix[l], w_in[l], rows_per_mod=seq, tm_norm=tm_mid,
                                      tm=tm_huge, hi=False)
    b_p = _dil_attn(zqkv_p, bsz, seq)
    a_p = _chunk_gate(za_p, ln_g[l], ln_b[l], w_s[l], b_s[l], tm=tm_mid)
    merged_p = _merge(a_p, b_p, zg_p, p_a[l], p_b[l], tm=tm_big, hi=False)
    x1_p = _out_proj(merged_p, w_o[l], xp, gt1_p, rows_per_mod=seq, tm=tm_huge, hi=False)
    hp_p, e_p, w_p = _ffn_norm(x1_p, sc2_p, sh2_p, g_ffn[l], w_router[l], b_router[l], rows_per_mod=seq, tm=tm_mid,
                               hi=False)

    za_s, zqkv_s, zg_s = _in_proj_all(xs, sc1_s, sh1_s, g_mix[l], w_in[l], rows_per_mod=n_s, tm_norm=n_s, tm=n_s,
                                      hi=True)
    b_s_out = _sample_attn(zqkv_s, tuple(c[l] for c in caches))
    a_s, vn_s = _sample_gate(za_s, ln_g[l], ln_b[l], w_s[l], b_s[l])
    merged_s = _merge(a_s, b_s_out, zg_s, p_a[l], p_b[l], tm=n_s, hi=True)
    x1_s = _out_proj(merged_s, w_o[l], xs, gt1_s, rows_per_mod=n_s, tm=n_s, hi=True)
    hp_s, e_s, w_s_top = _ffn_norm(x1_s, sc2_s, sh2_s, g_ffn[l], w_router[l], b_router[l], rows_per_mod=n_s, tm=n_s,
                                   hi=True)

    assert t_p % ROUTE_TILE == 0
    t_all = t_p + n_s
    t_pad = -(-t_all // ROUTE_TILE) * ROUTE_TILE
    e_all = jnp.concatenate([e_p, e_s, jnp.full((t_pad - t_all, LANES), -1, I32)], axis=0)
    pos, cnt = _route(e_all)
    counts = cnt[0, :N_EXPERTS].astype(I32)
    n_tiles = (t_all * TOP_K + N_EXPERTS * (MOE_TILE - 1)) // MOE_TILE
    plan, pad_base, pad_cnt = _tile_table(counts, n_tiles)
    pos_p = pos[:t_p, :TOP_K]
    pos_s = pos[t_p:t_all, :TOP_K]
    x_sorted = _row_scatter(hp_p, hp_s, pos_p.reshape(-1), pos_s.reshape(-1), pad_base, pad_cnt,
                            n_tiles * MOE_TILE)
    y_rows = _moe_experts(x_sorted, plan, w_gate[l], b_gate[l], w_up[l], b_up[l], w_down[l], b_down[l])
    y_p = _moe_combine(y_rows, pos_p, w_p, x1_p, gt2_p, g_final, rows_per_mod=seq, tok=_row_tile(seq, 128))
    y_s = _moe_combine(y_rows, pos_s, w_s_top, x1_s, gt2_s, g_final, rows_per_mod=n_s, tok=n_s)

    k0 = N_GROUPS * GROUP_COLS
    v0 = k0 + N_GROUPS * GROUP_COLS
    z_p3 = zqkv_p.reshape(bsz, seq, 3 * N_GROUPS * GROUP_COLS)
    z_s = zqkv_s
    kv_prompt, kv_sample = [], []
    for g, (win, dil) in enumerate(DIL_GROUPS):
        keep = min(win, seq)
        kc = slice(k0 + g * GROUP_COLS, k0 + (g + 1) * GROUP_COLS)
        vc = slice(v0 + g * GROUP_COLS, v0 + (g + 1) * GROUP_COLS)
        kv = jnp.stack([z_p3[:, seq - keep:, kc], z_p3[:, seq - keep:, vc]], axis=2)
        kv_prompt.append(kv.reshape(1, bsz, keep, 2, HEADS_PER_GROUP, HEAD_DIM))
        kvs = jnp.stack([z_s[:, kc], z_s[:, vc]], axis=1)
        kv_sample.append(kvs.reshape(1, n_s, 1, 2, HEADS_PER_GROUP, HEAD_DIM))
    return (y_p.reshape(bsz, seq, d), y_s.reshape(n_s, 1, d),
            kv_prompt[0], kv_prompt[1], kv_prompt[2],
            kv_sample[0], kv_sample[1], kv_sample[2],
            vn_s.reshape(1, n_s, 1, aw))
```

```python
import functools

import jax
import jax.numpy as jnp
from jax import lax
from jax.experimental import pallas as pl
from jax.experimental.pallas import tpu as pltpu

F32 = jnp.float32
BF16 = jnp.bfloat16
U32 = jnp.uint32
I32 = jnp.int32
HIGHEST = lax.Precision.HIGHEST

EPS = 1e-6
A_GROUPS = 8
CHUNK = 128
HEAD_DIM = 128
HEADS_PER_GROUP = 4
DIL_GROUPS = ((128, 1), (512, 4), (2048, 16))
N_GROUPS = len(DIL_GROUPS)
N_HEADS = HEADS_PER_GROUP * N_GROUPS
ALIBI_SLOPES = tuple(2.0 ** (-8.0 * (h + 1) / N_HEADS) for h in range(N_HEADS))
GROUP_COLS = HEADS_PER_GROUP * HEAD_DIM
N_EXPERTS = 32
TOP_K = 4
SWIGLU_LIMIT = 7.0
SWIGLU_ALPHA = 1.702
N_ADA = 6
NEG = -1e30

LANES = 128
SUBLANES = 8
COL_TILE = 512
ATTN_BLOCK = 2048
ATTN_BATCH = 8
MOE_TILE = 512
MOE_QUARTER = MOE_TILE // 4
ROUTE_TILE = 512
VMEM_LIMIT = 56 * 1024 * 1024


def _params(sem, vmem=VMEM_LIMIT):
    return pltpu.CompilerParams(dimension_semantics=sem, vmem_limit_bytes=vmem)


def _dot_exact(a, b):
    return jnp.dot(a, b, precision=HIGHEST, preferred_element_type=F32)


def _dot_split(a, b):
    a_hi, b_hi = a.astype(BF16), b.astype(BF16)
    a_lo = (a - a_hi.astype(F32)).astype(BF16)
    b_lo = (b - b_hi.astype(F32)).astype(BF16)
    return (jnp.dot(a_hi, b_hi, preferred_element_type=F32) + jnp.dot(a_hi, b_lo, preferred_element_type=F32)
            + jnp.dot(a_lo, b_hi, preferred_element_type=F32))


def _dot(a, b, hi):
    if hi:
        return _dot_split(a, b)
    return jnp.dot(a.astype(BF16), b.astype(BF16), preferred_element_type=F32)


def _mod_rms(x, g, sc, sh):
    y = x * lax.rsqrt(jnp.mean(x * x, axis=-1, keepdims=True) + EPS)
    return y * g * (1.0 + sc) + sh


def _ada_kernel(c_ref, w_ref, b_ref, o_ref):
    c = c_ref[...]
    o_ref[...] = _dot(c * jax.nn.sigmoid(c), w_ref[...], True) + b_ref[...]


def _ada(c_all, w_ada, b_ada):
    n, d = c_all.shape
    cols = w_ada.shape[1]
    tn = 1024
    return pl.pallas_call(
        _ada_kernel,
        grid=(cols // tn,),
        in_specs=[pl.BlockSpec((n, d), lambda j: (0, 0)),
                  pl.BlockSpec((d, tn), lambda j: (0, j)),
                  pl.BlockSpec((1, tn), lambda j: (0, j))],
        out_specs=pl.BlockSpec((n, tn), lambda j: (0, j)),
        out_shape=jax.ShapeDtypeStruct((n, cols), F32),
        compiler_params=_params(("arbitrary",)),
        name="ada",
    )(c_all, w_ada, b_ada.reshape(1, cols))


def _mix_norm_kernel(x_ref, sc_ref, sh_ref, g_ref, h_ref):
    h_ref[...] = _mod_rms(x_ref[...], g_ref[...], sc_ref[0], sh_ref[0]).astype(h_ref.dtype)


def _mix_norm(x, sc, sh, g, *, rows_per_mod, tm, hi):
    t, d = x.shape
    mod_rows = sc.shape[1]
    mod_spec = pl.BlockSpec((1, mod_rows, d), lambda i: (i * tm // rows_per_mod, 0, 0))
    return pl.pallas_call(
        _mix_norm_kernel,
        grid=(t // tm,),
        in_specs=[pl.BlockSpec((tm, d), lambda i: (i, 0)), mod_spec, mod_spec, pl.BlockSpec((1, d), lambda i: (0, 0))],
        out_specs=pl.BlockSpec((tm, d), lambda i: (i, 0)),
        out_shape=jax.ShapeDtypeStruct((t, d), F32 if hi else BF16),
        compiler_params=_params(("arbitrary",)),
        name="mix_norm",
    )(x, sc, sh, g.reshape(1, d))


def _in_proj_kernel(h_ref, w_ref, o_ref, *, hi, act):
    z = _dot(h_ref[...], w_ref[...], hi)
    o_ref[...] = z if act is None else act(z)


def _in_proj(h, w_in, col0, n_cols, act, *, tm, hi):
    t, d = h.shape
    assert col0 % COL_TILE == 0 and n_cols % COL_TILE == 0
    blk0 = col0 // COL_TILE
    return pl.pallas_call(
        functools.partial(_in_proj_kernel, hi=hi, act=act),
        grid=(t // tm, n_cols // COL_TILE),
        in_specs=[pl.BlockSpec((tm, d), lambda i, j: (i, 0)),
                  pl.BlockSpec((d, COL_TILE), lambda i, j: (0, blk0 + j))],
        out_specs=pl.BlockSpec((tm, COL_TILE), lambda i, j: (i, j)),
        out_shape=jax.ShapeDtypeStruct((t, n_cols), F32),
        compiler_params=_params(("arbitrary", "arbitrary")),
        name="in_proj",
    )(h, w_in)


def _in_proj_all(x, sc, sh, g, w_in, *, rows_per_mod, tm_norm, tm, hi):
    d = x.shape[1]
    aw2 = 2 * A_GROUPS * CHUNK
    qkv = 3 * N_HEADS * HEAD_DIM
    h = _mix_norm(x, sc, sh, g, rows_per_mod=rows_per_mod, tm=tm_norm, hi=hi)
    z_a = _in_proj(h, w_in, 0, aw2, jax.nn.gelu, tm=tm, hi=hi)
    z_qkv = _in_proj(h, w_in, aw2, qkv, None, tm=tm, hi=hi)
    z_gate = _in_proj(h, w_in, aw2 + qkv, 2 * d, jax.nn.sigmoid, tm=tm, hi=hi)
    return z_a, z_qkv, z_gate


def _attend(q, kc, kp, vc, vp, slope, dil, prev_bias):
    row = lax.broadcasted_iota(I32, (CHUNK, CHUNK), 0)
    col = lax.broadcasted_iota(I32, (CHUNK, CHUNK), 1)
    pen_c = jnp.where(col <= row, -slope * ((row - col) * dil).astype(F32), NEG)
    pen_p = jnp.where(col >= row, -slope * ((row + CHUNK - col) * dil).astype(F32), NEG)
    scale = HEAD_DIM ** -0.5
    qk = (((2,), (2,)), ((0,), (0,)))
    pv = (((2,), (1,)), ((0,), (0,)))
    s_c = lax.dot_general(q, kc, qk, preferred_element_type=F32) * scale + pen_c[None]
    s_p = lax.dot_general(q, kp, qk, preferred_element_type=F32) * scale + pen_p[None] + prev_bias
    m = jnp.max(jnp.maximum(s_c, s_p), axis=2, keepdims=True)
    p_c = jnp.exp(s_c - m)
    p_p = jnp.exp(s_p - m)
    l = jnp.sum(p_c + p_p, axis=2, keepdims=True)
    o = (lax.dot_general((p_c / l).astype(BF16), vc, pv, preferred_element_type=F32)
         + lax.dot_general((p_p / l).astype(BF16), vp, pv, preferred_element_type=F32))
    return o, m + jnp.log(l)


def _dil_attn_kernel(slope_ref, *refs):
    ins = refs[:5 * N_GROUPS]
    o_ref = refs[5 * N_GROUPS]
    acc_refs = refs[5 * N_GROUPS + 1:5 * N_GROUPS + 1 + N_GROUPS]
    lse_refs = refs[5 * N_GROUPS + 1 + N_GROUPS:]
    head = pl.program_id(1)
    i = pl.program_id(2)
    no_prev = jnp.where(i > 0, 0.0, NEG)

    for g, (win, dil) in enumerate(DIL_GROUPS):
        q_ref, k_ref, v_ref, kp_ref, vp_ref = ins[5 * g:5 * g + 5]
        acc_ref, lse_ref = acc_refs[g], lse_refs[g]
        slope = slope_ref[g * HEADS_PER_GROUP + head]
        span = CHUNK * dil
        n_blocks = ATTN_BLOCK // CHUNK

        def rows(start, dil=dil):
            return pl.ds(start, CHUNK) if dil == 1 else pl.ds(start, CHUNK, stride=dil)

        cur = [rows((b // dil) * span + b % dil) for b in range(n_blocks)]
        prev = [rows((b // dil - 1) * span + b % dil) for b in range(n_blocks)]

        def gather(blocks, ref, first_ref=None):
            tiles = []
            for b in blocks:
                if first_ref is None:
                    tiles.append(ref[cur[b], :])
                elif b < dil:
                    tiles.append(first_ref[rows(b), :])
                else:
                    tiles.append(ref[prev[b], :])
            return jnp.stack(tiles).astype(BF16)

        for b0 in range(0, n_blocks, ATTN_BATCH):
            blocks = range(b0, b0 + ATTN_BATCH)
            first_blocks = b0 + lax.broadcasted_iota(I32, (ATTN_BATCH, 1, 1), 0) < dil
            o, lse = _attend(gather(blocks, q_ref), gather(blocks, k_ref), gather(blocks, k_ref, kp_ref),
                             gather(blocks, v_ref), gather(blocks, v_ref, vp_ref),
                             slope, dil, jnp.where(first_blocks, no_prev, 0.0))
            for i_b, b in enumerate(blocks):
                acc_ref[cur[b], :] = o[i_b]
                lse_ref[cur[b], :] = jnp.broadcast_to(lse[i_b], (CHUNK, HEAD_DIM))

    ls = [r[...] for r in lse_refs]
    m = jnp.maximum(jnp.maximum(ls[0], ls[1]), ls[2])
    e = [jnp.exp(l - m) for l in ls]
    den = e[0] + e[1] + e[2]
    o_ref[...] = (acc_refs[0][...] * (e[0] / den) + acc_refs[1][...] * (e[1] / den)
                  + acc_refs[2][...] * (e[2] / den))


def _dil_attn(z, bsz, seq):
    assert seq % ATTN_BLOCK == 0
    nblk = seq // ATTN_BLOCK
    a_blocks = 0
    in_specs = [pl.BlockSpec(memory_space=pltpu.SMEM)]
    for g, (win, dil) in enumerate(DIL_GROUPS):
        span = CHUNK * dil
        per_blk = ATTN_BLOCK // span
        for which in range(3):
            cb = a_blocks + (which * N_GROUPS + g) * HEADS_PER_GROUP
            in_specs.append(pl.BlockSpec((ATTN_BLOCK, HEAD_DIM), lambda b, h, i, cb=cb: (b * nblk + i, cb + h)))
        for which in (1, 2):
            cb = a_blocks + (which * N_GROUPS + g) * HEADS_PER_GROUP
            in_specs.append(pl.BlockSpec(
                (span, HEAD_DIM),
                lambda b, h, i, cb=cb, per_blk=per_blk: (jnp.maximum((b * nblk + i) * per_blk - 1, 0), cb + h)))
    slopes = jnp.asarray(ALIBI_SLOPES, F32)
    blk = pltpu.VMEM((ATTN_BLOCK, HEAD_DIM), F32)
    return pl.pallas_call(
        _dil_attn_kernel,
        grid=(bsz, HEADS_PER_GROUP, nblk),
        in_specs=in_specs,
        out_specs=pl.BlockSpec((ATTN_BLOCK, HEAD_DIM), lambda b, h, i: (b * nblk + i, h)),
        out_shape=jax.ShapeDtypeStruct((bsz * seq, GROUP_COLS), F32),
        scratch_shapes=[blk] * (2 * N_GROUPS),
        compiler_params=_params(("arbitrary", "arbitrary", "arbitrary")),
        name="dil_attn",
    )(slopes, *([z] * (5 * N_GROUPS)))


def _sample_attn_kernel(pen_ref, q_ref, k_ref, v_ref, c0_ref, c1_ref, c2_ref, o_ref):
    caches = (c0_ref, c1_ref, c2_ref)
    scale = HEAD_DIM ** -0.5
    heads = [(g, h) for g in range(N_GROUPS) for h in range(HEADS_PER_GROUP)]

    def head_rows(ref):
        return jnp.stack([ref[0, :, i * HEAD_DIM:(i + 1) * HEAD_DIM] for i in range(N_HEADS)])

    q = head_rows(q_ref)
    k_new = head_rows(k_ref)
    v_new = head_rows(v_ref)
    k_buf = jnp.stack([caches[g][:, 0, h, :] for g, h in heads])
    v_buf = jnp.stack([caches[g][:, 1, h, :] for g, h in heads])
    q8 = jnp.broadcast_to(q, (N_HEADS, SUBLANES, HEAD_DIM))
    qk = (((2,), (2,)), ((0,), (0,)))
    pv = (((2,), (1,)), ((0,), (0,)))
    s_buf = lax.dot_general(q8, k_buf, qk, precision=HIGHEST, preferred_element_type=F32)[:, 0:1, :]
    s_buf = s_buf * scale - pen_ref[...]
    s_new = jnp.sum(q * k_new, axis=2, keepdims=True) * scale
    m = jnp.maximum(jnp.max(s_buf, axis=2, keepdims=True), s_new)
    p_buf = jnp.exp(s_buf - m)
    p_new = jnp.exp(s_new - m)
    l = jnp.sum(p_buf, axis=2, keepdims=True) + p_new
    pb8 = jnp.broadcast_to(p_buf / l, (N_HEADS, SUBLANES, CHUNK))
    o = lax.dot_general(pb8, v_buf, pv, precision=HIGHEST, preferred_element_type=F32)[:, 0:1, :]
    o = o + (p_new / l) * v_new
    lse = m + jnp.log(l)
    n = HEADS_PER_GROUP
    ls = [lse[g * n:(g + 1) * n] for g in range(N_GROUPS)]
    mm = jnp.maximum(jnp.maximum(ls[0], ls[1]), ls[2])
    e = [jnp.exp(x - mm) for x in ls]
    den = e[0] + e[1] + e[2]
    out = o[0:n] * (e[0] / den) + o[n:2 * n] * (e[1] / den) + o[2 * n:3 * n] * (e[2] / den)
    for h in range(HEADS_PER_GROUP):
        o_ref[0, :, h * HEAD_DIM:(h + 1) * HEAD_DIM] = out[h]


def _sample_attn(zs, caches):
    n, cols = zs.shape
    z3 = zs.reshape(n, 1, cols)
    a_blocks = 0
    qkv_cols = N_GROUPS * GROUP_COLS
    c_views = []
    c_specs = []
    for g, (win, dil) in enumerate(DIL_GROUPS):
        c = caches[g]
        assert c.shape[1] == win, "cache must hold exactly one window"
        c_views.append(c.reshape(n, win // dil, dil, 2, HEADS_PER_GROUP, HEAD_DIM))
        c_specs.append(pl.BlockSpec((None, win // dil, None, 2, HEADS_PER_GROUP, HEAD_DIM),
                                    lambda b: (b, 0, 0, 0, 0, 0)))
    q0 = a_blocks * GROUP_COLS
    steps = jnp.arange(CHUNK, 0, -1, dtype=F32)
    pen = jnp.stack([ALIBI_SLOPES[g * HEADS_PER_GROUP + h] * dil * steps
                     for g, (win, dil) in enumerate(DIL_GROUPS) for h in range(HEADS_PER_GROUP)])
    assert all(win // dil == CHUNK for win, dil in DIL_GROUPS)
    row_spec = pl.BlockSpec((1, 1, qkv_cols), lambda b: (b, 0, 0))
    out = pl.pallas_call(
        _sample_attn_kernel,
        grid=(n,),
        in_specs=[pl.BlockSpec((N_HEADS, 1, CHUNK), lambda b: (0, 0, 0)), row_spec, row_spec, row_spec] + c_specs,
        out_specs=pl.BlockSpec((1, 1, GROUP_COLS), lambda b: (b, 0, 0)),
        out_shape=jax.ShapeDtypeStruct((n, 1, GROUP_COLS), F32),
        compiler_params=_params(("arbitrary",)),
        name="sample_attn",
    )(pen.reshape(N_HEADS, 1, CHUNK), z3[:, :, q0:q0 + qkv_cols], z3[:, :, q0 + qkv_cols:q0 + 2 * qkv_cols],
      z3[:, :, q0 + 2 * qkv_cols:q0 + 3 * qkv_cols], *c_views)
    return out.reshape(n, GROUP_COLS)


def _layer_norm(v, g, b):
    mu = jnp.mean(v, axis=-1, keepdims=True)
    var = jnp.mean(jnp.square(v - mu), axis=-1, keepdims=True)
    return (v - mu) * lax.rsqrt(var + EPS) * g + b


def _chunk_gate_kernel(u_ref, v_ref, lg_ref, lb_ref, ws_ref, bs_ref, o_ref, *, n_chunks):
    va = _layer_norm(v_ref[...], lg_ref[...], lb_ref[...]).astype(BF16)
    row = lax.broadcasted_iota(I32, (CHUNK, CHUNK), 0)
    col = lax.broadcasted_iota(I32, (CHUNK, CHUNK), 1)
    for g in range(A_GROUPS):
        w = jnp.where(col <= row, ws_ref[g], 0.0).astype(BF16)
        gs = slice(g * CHUNK, (g + 1) * CHUNK)
        for c in range(n_chunks):
            rs = slice(c * CHUNK, (c + 1) * CHUNK)
            mixed = jnp.dot(w, va[rs, gs], preferred_element_type=F32) + bs_ref[g]
            o_ref[rs, gs] = (u_ref[rs, gs] * mixed).astype(o_ref.dtype)


def _chunk_gate(z, ln_g, ln_b, w_s, b_s, *, tm):
    t = z.shape[0]
    aw = A_GROUPS * CHUNK
    bs_full = jnp.broadcast_to(b_s[:, :, None], (A_GROUPS, CHUNK, CHUNK))
    kern = functools.partial(_chunk_gate_kernel, n_chunks=tm // CHUNK)
    return pl.pallas_call(
        kern,
        grid=(t // tm,),
        in_specs=[pl.BlockSpec((tm, aw), lambda i: (i, 0)),
                  pl.BlockSpec((tm, aw), lambda i: (i, 1)),
                  pl.BlockSpec((1, aw), lambda i: (0, 0)),
                  pl.BlockSpec((1, aw), lambda i: (0, 0)),
                  pl.BlockSpec((A_GROUPS, CHUNK, CHUNK), lambda i: (0, 0, 0)),
                  pl.BlockSpec((A_GROUPS, CHUNK, CHUNK), lambda i: (0, 0, 0))],
        out_specs=pl.BlockSpec((tm, aw), lambda i: (i, 0)),
        out_shape=jax.ShapeDtypeStruct((t, aw), BF16),
        compiler_params=_params(("arbitrary",)),
        name="chunk_gate",
    )(z, z, ln_g.reshape(1, aw), ln_b.reshape(1, aw), w_s, bs_full)


def _sample_gate_kernel(u_ref, v_ref, lg_ref, lb_ref, w0_ref, b0_ref, a_ref, vn_ref):
    va = _layer_norm(v_ref[...], lg_ref[...], lb_ref[...])
    vn_ref[...] = va
    a_ref[...] = u_ref[...] * (w0_ref[...] * va + b0_ref[...])


def _sample_gate(zs, ln_g, ln_b, w_s, b_s):
    n = zs.shape[0]
    aw = A_GROUPS * CHUNK
    w0 = jnp.repeat(w_s[:, 0, 0], CHUNK).reshape(1, aw)
    b0 = jnp.repeat(b_s[:, 0], CHUNK).reshape(1, aw)
    vec = pl.BlockSpec((1, aw), lambda i: (0, 0))
    return pl.pallas_call(
        _sample_gate_kernel,
        grid=(1,),
        in_specs=[pl.BlockSpec((n, aw), lambda i: (0, 0)), pl.BlockSpec((n, aw), lambda i: (0, 1)),
                  vec, vec, vec, vec],
        out_specs=[pl.BlockSpec((n, aw), lambda i: (0, 0)), pl.BlockSpec((n, aw), lambda i: (0, 0))],
        out_shape=[jax.ShapeDtypeStruct((n, aw), F32), jax.ShapeDtypeStruct((n, aw), F32)],
        compiler_params=_params(("arbitrary",)),
        name="sample_gate",
    )(zs, zs, ln_g.reshape(1, aw), ln_b.reshape(1, aw), w0, b0)


def _merge_kernel(a_ref, b_ref, ga_ref, gb_ref, pa_ref, pb_ref, out_ref, *, hi):
    ya = _dot(a_ref[...], pa_ref[...], hi)
    yb = _dot(b_ref[...], pb_ref[...], hi)
    out_ref[...] = (ga_ref[...] * ya + gb_ref[...] * yb).astype(out_ref.dtype)


def _merge(a_out, b_out, z, p_a, p_b, *, tm, hi):
    t, aw = a_out.shape
    d = p_a.shape[1]
    tn = COL_TILE
    ga_blk0 = 0
    gb_blk0 = d // tn
    return pl.pallas_call(
        functools.partial(_merge_kernel, hi=hi),
        grid=(t // tm, d // tn),
        in_specs=[pl.BlockSpec((tm, aw), lambda i, j: (i, 0)),
                  pl.BlockSpec((tm, GROUP_COLS), lambda i, j: (i, 0)),
                  pl.BlockSpec((tm, tn), lambda i, j: (i, ga_blk0 + j)),
                  pl.BlockSpec((tm, tn), lambda i, j: (i, gb_blk0 + j)),
                  pl.BlockSpec((aw, tn), lambda i, j: (0, j)),
                  pl.BlockSpec((GROUP_COLS, tn), lambda i, j: (0, j))],
        out_specs=pl.BlockSpec((tm, tn), lambda i, j: (i, j)),
        out_shape=jax.ShapeDtypeStruct((t, d), F32 if hi else BF16),
        compiler_params=_params(("arbitrary", "arbitrary")),
        name="merge",
    )(a_out, b_out, z, z, p_a, p_b)


def _out_proj_kernel(m_ref, w_ref, x_ref, gt_ref, o_ref, *, hi):
    o_ref[...] = x_ref[...] + gt_ref[0] * _dot(m_ref[...], w_ref[...], hi)


def _out_proj(merged, w_o, x, gt, *, rows_per_mod, tm, hi):
    t, d = x.shape
    tn = COL_TILE
    mod_rows = gt.shape[1]
    return pl.pallas_call(
        functools.partial(_out_proj_kernel, hi=hi),
        grid=(t // tm, d // tn),
        in_specs=[pl.BlockSpec((tm, d), lambda i, j: (i, 0)),
                  pl.BlockSpec((d, tn), lambda i, j: (0, j)),
                  pl.BlockSpec((tm, tn), lambda i, j: (i, j)),
                  pl.BlockSpec((1, mod_rows, tn), lambda i, j: (i * tm // rows_per_mod, 0, j))],
        out_specs=pl.BlockSpec((tm, tn), lambda i, j: (i, j)),
        out_shape=jax.ShapeDtypeStruct((t, d), F32),
        compiler_params=_params(("arbitrary", "arbitrary")),
        name="out_proj",
    )(merged, w_o, x, gt)


def _pack_pair(lo, hi, pair_ref):
    n = lo.shape[0]
    pair_ref[pl.ds(0, n, stride=2), :] = lo
    pair_ref[pl.ds(1, n, stride=2), :] = hi
    return pltpu.bitcast(pair_ref[0:2 * n, :].astype(BF16), U32)


def _unpack_pair(w, pair_ref):
    n = w.shape[0]
    pair_ref[0:2 * n, :] = pltpu.bitcast(w, BF16).astype(F32)
    return pair_ref[pl.ds(0, n, stride=2), :].astype(BF16), pair_ref[pl.ds(1, n, stride=2), :].astype(BF16)


def _ffn_norm_kernel(x_ref, sc_ref, sh_ref, g_ref, wr_ref, br_ref, hp_ref, e_ref, w_ref, pair_ref, *, hi):
    h = _mod_rms(x_ref[...], g_ref[...], sc_ref[0], sh_ref[0])
    tm, d = h.shape
    for s in range(d // (2 * LANES)):
        even = h[:, (2 * s) * LANES:(2 * s + 1) * LANES]
        odd = h[:, (2 * s + 1) * LANES:(2 * s + 2) * LANES]
        hp_ref[pl.ds(s, tm, stride=SUBLANES), :] = _pack_pair(even, odd, pair_ref)
    logits = (_dot_exact if hi else _dot_split)(h, wr_ref[...]) + br_ref[...]
    lane = lax.broadcasted_iota(I32, logits.shape, 1)
    lane_f = lane.astype(F32)
    vals, idxs = [], []
    for _ in range(TOP_K):
        m = jnp.max(logits, axis=1, keepdims=True)
        idx = jnp.min(jnp.where(logits == m, lane_f, float(logits.shape[1])), axis=1, keepdims=True)
        vals.append(m)
        idxs.append(idx)
        logits = jnp.where(lane_f == idx, 2.0 * NEG, logits)
    es = [jnp.exp(v - vals[0]) for v in vals]
    den = es[0] + es[1] + es[2] + es[3]
    e_out = jnp.full(lane.shape, -1.0, F32)
    w_out = jnp.zeros(lane.shape, F32)
    for k in range(TOP_K):
        e_out = jnp.where(lane == k, idxs[k], e_out)
        w_out = jnp.where(lane == k, es[k] / den, w_out)
    e_ref[...] = e_out.astype(I32)
    w_ref[...] = w_out


def _ffn_norm(x1, sc, sh, g, w_router, b_router, *, rows_per_mod, tm, hi):
    t, d = x1.shape
    assert d == 2 * LANES * SUBLANES, "one packed row must be exactly one (8, 128) tile"
    ne = w_router.shape[1]
    wr = jnp.zeros((d, LANES), F32).at[:, :ne].set(w_router)
    br = jnp.full((1, LANES), NEG, F32).at[0, :ne].set(b_router)
    mod_rows = sc.shape[1]
    mod_spec = pl.BlockSpec((1, mod_rows, d), lambda i: (i * tm // rows_per_mod, 0, 0))
    return pl.pallas_call(
        functools.partial(_ffn_norm_kernel, hi=hi),
        grid=(t // tm,),
        in_specs=[pl.BlockSpec((tm, d), lambda i: (i, 0)), mod_spec, mod_spec,
                  pl.BlockSpec((1, d), lambda i: (0, 0)),
                  pl.BlockSpec((d, LANES), lambda i: (0, 0)),
                  pl.BlockSpec((1, LANES), lambda i: (0, 0))],
        out_specs=[pl.BlockSpec((tm * SUBLANES, LANES), lambda i: (i, 0)),
                   pl.BlockSpec((tm, LANES), lambda i: (i, 0)),
                   pl.BlockSpec((tm, LANES), lambda i: (i, 0))],
        out_shape=[jax.ShapeDtypeStruct((t * SUBLANES, LANES), U32),
                   jax.ShapeDtypeStruct((t, LANES), I32),
                   jax.ShapeDtypeStruct((t, LANES), F32)],
        scratch_shapes=[pltpu.VMEM((2 * tm, LANES), F32)],
        compiler_params=_params(("arbitrary",)),
        name="ffn_norm",
    )(x1, sc, sh, g.reshape(1, d), wr, br)


def _route_kernel(e_ref, pos_ref, cnt_ref, tri_ref, run_ref):
    phase = pl.program_id(0)
    i = pl.program_id(1)
    e = e_ref[...]
    tm = e.shape[0]
    lane = lax.broadcasted_iota(I32, (tm, LANES), 1)
    hits = [lane == e[:, k:k + 1] for k in range(TOP_K)]
    chosen = jnp.zeros((tm, LANES), F32)
    for k in range(TOP_K):
        chosen = jnp.where(hits[k], 1.0, chosen)
    col_count = jnp.sum(chosen, axis=0, keepdims=True)

    @pl.when((phase == 0) & (i == 0))
    def _():
        cnt_ref[...] = jnp.zeros(cnt_ref.shape, F32)
        r = lax.broadcasted_iota(I32, (tm, tm), 0)
        c = lax.broadcasted_iota(I32, (tm, tm), 1)
        tri_ref[...] = jnp.where(c < r, 1.0, 0.0).astype(BF16)

    @pl.when(phase == 0)
    def _():
        cnt_ref[...] = cnt_ref[...] + col_count

    @pl.when((phase == 1) & (i == 0))
    def _():
        tiles = jnp.floor((cnt_ref[...] + (MOE_TILE - 1)) * (1.0 / MOE_TILE))
        r = lax.broadcasted_iota(I32, (LANES, LANES), 0)
        c = lax.broadcasted_iota(I32, (LANES, LANES), 1)
        below = jnp.where(r < c, 1.0, 0.0)
        tiles8 = jnp.broadcast_to(tiles, (SUBLANES, LANES))
        run_ref[...] = _dot_exact(tiles8, below)[0:1] * float(MOE_TILE)

    @pl.when(phase == 1)
    def _():
        before = jnp.dot(tri_ref[...], chosen.astype(BF16), preferred_element_type=F32)
        dest = run_ref[...] + before
        out = jnp.zeros((tm, LANES), F32)
        for k in range(TOP_K):
            p = jnp.sum(jnp.where(hits[k], dest, 0.0), axis=1, keepdims=True)
            out = jnp.where(lane == k, p, out)
        pos_ref[...] = out.astype(I32)
        run_ref[...] = run_ref[...] + col_count


def _route(e_all):
    t = e_all.shape[0]
    steps = t // ROUTE_TILE
    return pl.pallas_call(
        _route_kernel,
        grid=(2, steps),
        in_specs=[pl.BlockSpec((ROUTE_TILE, LANES), lambda p, i: (i, 0))],
        out_specs=[pl.BlockSpec((ROUTE_TILE, LANES), lambda p, i: (i * p, 0)),
                   pl.BlockSpec((1, LANES), lambda p, i: (0, 0))],
        out_shape=[jax.ShapeDtypeStruct((t, LANES), I32), jax.ShapeDtypeStruct((1, LANES), F32)],
        scratch_shapes=[pltpu.VMEM((ROUTE_TILE, ROUTE_TILE), BF16), pltpu.VMEM((1, LANES), F32)],
        compiler_params=_params(("arbitrary", "arbitrary")),
        name="route",
    )(e_all)


def _row_scatter_kernel(pos_ref, pos_s_ref, pad_base_ref, pad_cnt_ref, hp_ref, hs_ref, xs_hbm, sem, *,
                        n_prompt_steps, n_sample):
    i = pl.program_id(0)

    def rows(first, n=1):
        return pl.ds(pl.multiple_of(first * SUBLANES, SUBLANES), n * SUBLANES)

    def scatter(src_ref, idx_ref, n_tok):
        def body(tt, carry):
            for k in range(TOP_K):
                pltpu.make_async_copy(src_ref.at[rows(tt)], xs_hbm.at[rows(idx_ref[0, 0, tt * TOP_K + k])],
                                      sem).start(priority=k % 2)
            return carry
        lax.fori_loop(0, n_tok, body, 0, unroll=2)
        for _ in range(TOP_K):
            pltpu.make_async_copy(src_ref.at[rows(0, n_tok)], xs_hbm.at[rows(0, n_tok)], sem).wait()

    @pl.when(i < n_prompt_steps)
    def _():
        scatter(hp_ref, pos_ref, ROUTE_TILE)

    @pl.when(i == n_prompt_steps)
    def _():
        scatter(hs_ref, pos_s_ref, n_sample)

        def fill(dst_row, size):
            return pltpu.make_async_copy(hp_ref.at[rows(0, size)], xs_hbm.at[rows(dst_row, size)], sem)

        def per_segment(wait):
            def body(e, carry):
                cnt = pad_cnt_ref[e]
                base = pad_base_ref[e]

                def whole(r, c):
                    cp = fill(base + (cnt % MOE_TILE) + r * MOE_TILE, MOE_TILE)
                    cp.wait() if wait else cp.start()
                    return c
                lax.fori_loop(0, cnt // MOE_TILE, whole, 0)
                size = MOE_TILE // 2
                while size >= 1:
                    @pl.when((cnt & size) != 0)
                    def _(size=size):
                        cp = fill(base + (cnt & (size - 1)), size)
                        cp.wait() if wait else cp.start()
                    size //= 2
                return carry
            return body
        lax.fori_loop(0, pad_cnt_ref.shape[0], per_segment(False), 0)
        lax.fori_loop(0, pad_cnt_ref.shape[0], per_segment(True), 0)


def _row_scatter(hp_p, hp_s, pos_p, pos_s, pad_base, pad_cnt, n_rows):
    t_p = hp_p.shape[0] // SUBLANES
    n_s = hp_s.shape[0] // SUBLANES
    assert ROUTE_TILE >= MOE_TILE, "padding blocks are copied from one token tile"
    steps = t_p // ROUTE_TILE
    tile_rows = ROUTE_TILE * TOP_K
    return pl.pallas_call(
        functools.partial(_row_scatter_kernel, n_prompt_steps=steps, n_sample=n_s),
        grid=(steps + 1,),
        in_specs=[pl.BlockSpec((1, 1, tile_rows), lambda i: (jnp.minimum(i, steps - 1), 0, 0),
                               memory_space=pltpu.SMEM),
                  pl.BlockSpec((1, 1, n_s * TOP_K), lambda i: (0, 0, 0), memory_space=pltpu.SMEM),
                  pl.BlockSpec(memory_space=pltpu.SMEM),
                  pl.BlockSpec(memory_space=pltpu.SMEM),
                  pl.BlockSpec((ROUTE_TILE * SUBLANES, LANES), lambda i: (jnp.minimum(i, steps - 1), 0)),
                  pl.BlockSpec((n_s * SUBLANES, LANES), lambda i: (0, 0))],
        out_specs=pl.BlockSpec(memory_space=pl.ANY),
        out_shape=jax.ShapeDtypeStruct((n_rows * SUBLANES, LANES), U32),
        scratch_shapes=[pltpu.SemaphoreType.DMA(())],
        compiler_params=_params(("arbitrary",)),
        name="row_scatter",
    )(pos_p.reshape(steps, 1, tile_rows), pos_s.reshape(1, 1, n_s * TOP_K), pad_base, pad_cnt, hp_p, hp_s)


def _expert_weights(seg_ref, sege_ref, nu_ref, fetch, on_ready):
    j = pl.program_id(0)
    t = pl.program_id(1)
    n_j = pl.num_programs(0)
    seg = seg_ref[t]
    n_seg = nu_ref[1]
    first = (t < nu_ref[0]) & ((t == 0) | (seg != seg_ref[jnp.maximum(t - 1, 0)]))
    k = j * n_seg + seg
    slot = lax.rem(k, 2)

    @pl.when(first)
    def _():
        @pl.when(k == 0)
        def _():
            for cp in fetch(sege_ref[0], 0, 0):
                cp.start()

        more = seg + 1 < n_seg

        @pl.when(more | (j + 1 < n_j))
        def _():
            for cp in fetch(sege_ref[jnp.where(more, seg + 1, 0)], jnp.where(more, j, j + 1), 1 - slot):
                cp.start()

        for cp in fetch(sege_ref[seg], j, slot):
            cp.wait()
        on_ready(slot)


def _moe_up_kernel(te_ref, nv_ref, seg_ref, sege_ref, nu_ref, x_ref, wg_hbm, wu_hbm, bg_ref, bu_ref, h_ref,
                   wg_buf, wu_buf, sem, wg_s, wu_s, xb_s, pair_ref):
    t = pl.program_id(1)
    tf = wg_s.shape[1]

    def fetch(e, j, slot):
        cols = pl.ds(pl.multiple_of(j * tf, tf), tf)
        return [pltpu.make_async_copy(wg_hbm.at[e, :, cols], wg_buf.at[slot], sem.at[slot]),
                pltpu.make_async_copy(wu_hbm.at[e, :, cols], wu_buf.at[slot], sem.at[slot])]

    def on_ready(slot):
        wg_s[...] = wg_buf[slot].astype(BF16)
        wu_s[...] = wu_buf[slot].astype(BF16)

    _expert_weights(seg_ref, sege_ref, nu_ref, fetch, on_ready)

    def compute(rows):
        for s in range(SUBLANES):
            lo, hi = _unpack_pair(x_ref[pl.ds(s, rows, stride=SUBLANES), :], pair_ref)
            xb_s[0:rows, (2 * s) * LANES:(2 * s + 1) * LANES] = lo
            xb_s[0:rows, (2 * s + 1) * LANES:(2 * s + 2) * LANES] = hi
        x = xb_s[0:rows, :]
        gt = jnp.dot(x, wg_s[...], preferred_element_type=F32) + bg_ref[0]
        up = jnp.dot(x, wu_s[...], preferred_element_type=F32) + bu_ref[0]
        gt = jnp.minimum(gt, SWIGLU_LIMIT)
        up = jnp.clip(up, -SWIGLU_LIMIT, SWIGLU_LIMIT)
        h_ref[0:rows, :] = ((up + 1.0) * gt * jax.nn.sigmoid(SWIGLU_ALPHA * gt)).astype(h_ref.dtype)

    quarters = jnp.where(t < nu_ref[0], (nv_ref[t] + MOE_QUARTER - 1) // MOE_QUARTER, 0)
    for n_q in range(MOE_TILE // MOE_QUARTER + 1):
        @pl.when(quarters == n_q)
        def _(rows=n_q * MOE_QUARTER):
            if rows > 0:
                compute(rows)
            if rows < MOE_TILE:
                h_ref[rows:, :] = jnp.zeros((MOE_TILE - rows, h_ref.shape[1]), h_ref.dtype)


def _moe_down_kernel(te_ref, nv_ref, seg_ref, sege_ref, nu_ref, h_ref, wd_hbm, bd_ref, y_ref, wd_buf, sem, wd_s):
    t = pl.program_id(1)
    n_sub = y_ref.shape[0] // MOE_TILE
    tn = wd_s.shape[1]

    def fetch(e, j, slot):
        cols = pl.ds(pl.multiple_of(j * tn, tn), tn)
        return [pltpu.make_async_copy(wd_hbm.at[e, :, cols], wd_buf.at[slot], sem.at[slot])]

    def on_ready(slot):
        wd_s[...] = wd_buf[slot].astype(BF16)

    _expert_weights(seg_ref, sege_ref, nu_ref, fetch, on_ready)

    def store(rows, y):
        for c in range(n_sub):
            y_ref[pl.ds(c, rows, stride=n_sub), :] = y[:, c * LANES:(c + 1) * LANES]

    def compute(rows):
        store(rows, jnp.dot(h_ref[0:rows, :], wd_s[...], preferred_element_type=F32) + bd_ref[0])

    quarters = jnp.where(t < nu_ref[0], (nv_ref[t] + MOE_QUARTER - 1) // MOE_QUARTER, 0)
    for n_q in range(MOE_TILE // MOE_QUARTER + 1):
        @pl.when(quarters == n_q)
        def _(rows=n_q * MOE_QUARTER):
            if rows > 0:
                compute(rows)
            if rows < MOE_TILE:
                y_ref[rows * n_sub:, :] = jnp.zeros(((MOE_TILE - rows) * n_sub, LANES), F32)


def _moe_experts(x_sorted, plan, w_gate, b_gate, w_up, b_up, w_down, b_down):
    rows = x_sorted.shape[0] // SUBLANES
    ne, d, f = w_gate.shape
    nt = rows // MOE_TILE
    tf = 1024
    tn = d
    n_sub = d // LANES
    n_plan = len(plan)

    def row_map(j, t, te, nv, seg, sege, nu):
        return (jnp.minimum(t, nu[0] - 1), 0)

    def b_map(j, t, te, nv, seg, sege, nu):
        return (te[t], 0, j)

    hbm = pl.BlockSpec(memory_space=pl.ANY)
    h = pl.pallas_call(
        _moe_up_kernel,
        grid_spec=pltpu.PrefetchScalarGridSpec(
            num_scalar_prefetch=n_plan,
            grid=(f // tf, nt),
            in_specs=[pl.BlockSpec((MOE_TILE * SUBLANES, LANES), row_map), hbm, hbm,
                      pl.BlockSpec((1, 1, tf), b_map),
                      pl.BlockSpec((1, 1, tf), b_map)],
            out_specs=pl.BlockSpec((MOE_TILE, tf), lambda j, t, *_: (t, j)),
            scratch_shapes=[pltpu.VMEM((2, d, tf), F32), pltpu.VMEM((2, d, tf), F32), pltpu.SemaphoreType.DMA((2,)),
                            pltpu.VMEM((d, tf), BF16), pltpu.VMEM((d, tf), BF16), pltpu.VMEM((MOE_TILE, d), BF16),
                            pltpu.VMEM((2 * MOE_TILE, LANES), F32)]),
        out_shape=jax.ShapeDtypeStruct((rows, f), BF16),
        compiler_params=_params(("arbitrary", "arbitrary")),
        name="moe_up",
    )(*plan, x_sorted, w_gate, w_up, b_gate.reshape(ne, 1, f), b_up.reshape(ne, 1, f))

    y = pl.pallas_call(
        _moe_down_kernel,
        grid_spec=pltpu.PrefetchScalarGridSpec(
            num_scalar_prefetch=n_plan,
            grid=(d // tn, nt),
            in_specs=[pl.BlockSpec((MOE_TILE, f), row_map), hbm,
                      pl.BlockSpec((1, 1, tn), b_map)],
            out_specs=pl.BlockSpec((MOE_TILE * n_sub, LANES), lambda j, t, *_: (t, 0)),
            scratch_shapes=[pltpu.VMEM((2, f, tn), F32), pltpu.SemaphoreType.DMA((2,)), pltpu.VMEM((f, tn), BF16)]),
        out_shape=jax.ShapeDtypeStruct((rows * n_sub, LANES), F32),
        compiler_params=_params(("arbitrary", "arbitrary")),
        name="moe_down",
    )(*plan, h, w_down, b_down.reshape(ne, 1, d))
    return y.reshape(rows, d // LANES, LANES)


def _row_copy(src_hbm, buf, sem, slot, src_row, dst_row, n_sub):
    dst = pl.ds(pl.multiple_of(dst_row * n_sub, SUBLANES), n_sub)
    return pltpu.make_async_copy(src_hbm.at[src_row], buf.at[slot, dst], sem.at[slot])


def _issue_rows(idx_ref, src_hbm, buf, sem, slot, n_rows, n_sub):
    def body(pair, carry):
        for p in range(2):
            r = 2 * pair + p
            _row_copy(src_hbm, buf, sem, slot, idx_ref[0, 0, r], r, n_sub).start(priority=p)
        return carry
    lax.fori_loop(0, n_rows // 2, body, 0, unroll=4)


def _moe_combine_kernel(idx_ref, nxt_ref, wt_ref, y_hbm, x_ref, gt_ref, g_ref, o_ref, buf, sem, acc_ref, *, tok):
    i = pl.program_id(0)
    n = pl.num_programs(0)
    n_rows = TOP_K * tok
    n_sub = y_hbm.shape[1]
    slot = lax.rem(i, 2)

    @pl.when(i == 0)
    def _():
        _issue_rows(idx_ref, y_hbm, buf, sem, 0, n_rows, n_sub)

    for s in range(2):
        @pl.when((i + 1 < n) & (slot == s))
        def _():
            _issue_rows(nxt_ref, y_hbm, buf, sem, 1 - s, n_rows, n_sub)

    pltpu.make_async_copy(buf.at[slot], buf.at[slot], sem.at[slot]).wait()

    def per_token(t, carry):
        acc = None
        for k in range(TOP_K):
            first = pl.multiple_of((k * tok + t) * n_sub, n_sub)
            part = wt_ref[0, 0, k * tok + t] * buf[slot, pl.ds(first, n_sub), :]
            acc = part if acc is None else acc + part
        acc_ref[pl.ds(pl.multiple_of(t * n_sub, n_sub), n_sub), :] = acc
        return carry
    lax.fori_loop(0, tok, per_token, 0, unroll=4)

    sumsq = jnp.zeros((tok, 1), F32)
    for c in range(n_sub):
        cs = slice(c * LANES, (c + 1) * LANES)
        x = x_ref[:, cs] + gt_ref[0, :, cs] * acc_ref[pl.ds(c, tok, stride=n_sub), :]
        o_ref[:, cs] = x
        sumsq = sumsq + jnp.sum(x * x, axis=1, keepdims=True)
    o_ref[...] = o_ref[...] * lax.rsqrt(sumsq / (n_sub * LANES) + EPS) * g_ref[...]


def _moe_combine(y_rows, pos, top_w, x1, gt, g_final, *, rows_per_mod, tok):
    t, d = x1.shape
    n_sub = y_rows.shape[1]
    steps = t // tok
    n_rows = TOP_K * tok
    def k_major(a):
        return a[:, :TOP_K].reshape(steps, tok, TOP_K).transpose(0, 2, 1).reshape(steps, 1, n_rows)

    idx3 = k_major(pos)
    mod_rows = gt.shape[1]
    idx_blk = (1, 1, n_rows)
    return pl.pallas_call(
        functools.partial(_moe_combine_kernel, tok=tok),
        grid=(steps,),
        in_specs=[pl.BlockSpec(idx_blk, lambda i: (i, 0, 0), memory_space=pltpu.SMEM),
                  pl.BlockSpec(idx_blk, lambda i: (jnp.minimum(i + 1, steps - 1), 0, 0), memory_space=pltpu.SMEM),
                  pl.BlockSpec(idx_blk, lambda i: (i, 0, 0), memory_space=pltpu.SMEM),
                  pl.BlockSpec(memory_space=pl.ANY),
                  pl.BlockSpec((tok, d), lambda i: (i, 0)),
                  pl.BlockSpec((1, mod_rows, d), lambda i: (i * tok // rows_per_mod, 0, 0)),
                  pl.BlockSpec((1, d), lambda i: (0, 0))],
        out_specs=pl.BlockSpec((tok, d), lambda i: (i, 0)),
        out_shape=jax.ShapeDtypeStruct((t, d), F32),
        scratch_shapes=[pltpu.VMEM((2, n_rows * n_sub, LANES), F32), pltpu.SemaphoreType.DMA((2,)),
                        pltpu.VMEM((tok * n_sub, LANES), F32)],
        compiler_params=_params(("arbitrary",)),
        name="moe_combine",
    )(idx3, idx3, k_major(top_w), y_rows, x1, gt, g_final.reshape(1, d))


def _tile_table(counts, n_tiles):
    tiles_e = (counts + MOE_TILE - 1) // MOE_TILE
    tile_end = jnp.cumsum(tiles_e)
    tile_start = tile_end - tiles_e
    n_used = tile_end[-1]
    tile_ids = jnp.minimum(jnp.arange(n_tiles, dtype=I32), n_used - 1)
    tile_expert = jnp.minimum(jnp.sum(tile_end[None, :] <= tile_ids[:, None], axis=1), N_EXPERTS - 1).astype(I32)
    tile_valid = jnp.clip(counts[tile_expert] - (tile_ids - tile_start[tile_expert]) * MOE_TILE, 0, MOE_TILE)
    nonempty = tiles_e > 0
    seg_of_expert = jnp.cumsum(nonempty.astype(I32)) - 1
    n_seg = jnp.sum(nonempty.astype(I32))
    experts = jnp.arange(N_EXPERTS, dtype=I32)
    hit = nonempty[None, :] & (seg_of_expert[None, :] == jnp.arange(N_EXPERTS + 1, dtype=I32)[:, None])
    seg_expert = jnp.sum(jnp.where(hit, experts[None, :], 0), axis=1).astype(I32)
    tile_seg = seg_of_expert[tile_expert].astype(I32)
    plan = (tile_expert, tile_valid.astype(I32), tile_seg, seg_expert, jnp.stack([n_used, n_seg]).astype(I32))
    pad_base = jnp.concatenate([tile_start * MOE_TILE + counts, (n_used * MOE_TILE).reshape(1)])
    pad_cnt = jnp.concatenate([tiles_e * MOE_TILE - counts, ((n_tiles - n_used) * MOE_TILE).reshape(1)])
    return plan, pad_base.astype(I32), pad_cnt.astype(I32)


def _row_tile(t, want):
    tm = min(t, want)
    assert t % tm == 0
    return tm


def kernel(x_prompt, x_sample, cache_kv_g0, cache_kv_g1, cache_kv_g2, c_prompt, c_sample, w_ada, b_ada, g_mix, w_in, ln_g, ln_b, w_s, b_s, p_a, p_b, w_o, g_ffn, w_router, b_router, w_gate, b_gate, w_up, b_up, w_down, b_down, g_final):
    depth = w_ada.shape[0]
    assert depth == 1, "single-layer trunk"
    bsz, seq, d = x_prompt.shape
    n_s, dec_seq, _ = x_sample.shape
    assert dec_seq == 1, "one new position per sample"
    caches = (cache_kv_g0, cache_kv_g1, cache_kv_g2)
    l = 0
    cols = w_in.shape[2]
    t_p = bsz * seq
    aw = A_GROUPS * CHUNK

    n_c = bsz + n_s
    n_c_pad = -(-n_c // SUBLANES) * SUBLANES
    c_all = jnp.concatenate([c_prompt, c_sample, jnp.zeros((n_c_pad - n_c, d), F32)], axis=0)
    mod = _ada(c_all, w_ada[l], b_ada[l])
    mod_p = mod[:bsz].reshape(bsz, 1, N_ADA, d)
    mod_s = mod[bsz:n_c].reshape(1, n_s, N_ADA, d)
    sh1_p, sc1_p, gt1_p, sh2_p, sc2_p, gt2_p = (mod_p[:, :, k] for k in range(N_ADA))
    sh1_s, sc1_s, gt1_s, sh2_s, sc2_s, gt2_s = (mod_s[:, :, k] for k in range(N_ADA))

    xp = x_prompt.reshape(t_p, d)
    xs = x_sample.reshape(n_s, d)

    tm_big = _row_tile(seq, 1024)
    tm_mid = _row_tile(seq, 512)
    tm_huge = _row_tile(seq, 2048)
    za_p, zqkv_p, zg_p = _in_proj_all(xp, sc1_p, sh1_p, g_mix[l], w_in[l], rows_per_mod=seq, tm_norm=tm_mid,
                                      tm=tm_huge, hi=False)
    b_p = _dil_attn(zqkv_p, bsz, seq)
    a_p = _chunk_gate(za_p, ln_g[l], ln_b[l], w_s[l], b_s[l], tm=tm_mid)
    merged_p = _merge(a_p, b_p, zg_p, p_a[l], p_b[l], tm=tm_big, hi=False)
    x1_p = _out_proj(merged_p, w_o[l], xp, gt1_p, rows_per_mod=seq, tm=tm_huge, hi=False)
    hp_p, e_p, w_p = _ffn_norm(x1_p, sc2_p, sh2_p, g_ffn[l], w_router[l], b_router[l], rows_per_mod=seq, tm=tm_mid,
                               hi=False)

    za_s, zqkv_s, zg_s = _in_proj_all(xs, sc1_s, sh1_s, g_mix[l], w_in[l], rows_per_mod=n_s, tm_norm=n_s, tm=n_s,
                                      hi=True)
    b_s_out = _sample_attn(zqkv_s, tuple(c[l] for c in caches))
    a_s, vn_s = _sample_gate(za_s, ln_g[l], ln_b[l], w_s[l], b_s[l])
    merged_s = _merge(a_s, b_s_out, zg_s, p_a[l], p_b[l], tm=n_s, hi=True)
    x1_s = _out_proj(merged_s, w_o[l], xs, gt1_s, rows_per_mod=n_s, tm=n_s, hi=True)
    hp_s, e_s, w_s_top = _ffn_norm(x1_s, sc2_s, sh2_s, g_ffn[l], w_router[l], b_router[l], rows_per_mod=n_s, tm=n_s,
                                   hi=True)

    assert t_p % ROUTE_TILE == 0
    t_all = t_p + n_s
    t_pad = -(-t_all // ROUTE_TILE) * ROUTE_TILE
    e_all = jnp.concatenate([e_p, e_s, jnp.full((t_pad - t_all, LANES), -1, I32)], axis=0)
    pos, cnt = _route(e_all)
    counts = cnt[0, :N_EXPERTS].astype(I32)
    n_tiles = (t_all * TOP_K + N_EXPERTS * (MOE_TILE - 1)) // MOE_TILE
    plan, pad_base, pad_cnt = _tile_table(counts, n_tiles)
    pos_p = pos[:t_p, :TOP_K]
    pos_s = pos[t_p:t_all, :TOP_K]
    x_sorted = _row_scatter(hp_p, hp_s, pos_p.reshape(-1), pos_s.reshape(-1), pad_base, pad_cnt,
                            n_tiles * MOE_TILE)
    y_rows = _moe_experts(x_sorted, plan, w_gate[l], b_gate[l], w_up[l], b_up[l], w_down[l], b_down[l])
    y_p = _moe_combine(y_rows, pos_p, w_p, x1_p, gt2_p, g_final, rows_per_mod=seq, tok=_row_tile(seq, 128))
    y_s = _moe_combine(y_rows, pos_s, w_s_top, x1_s, gt2_s, g_final, rows_per_mod=n_s, tok=n_s)

    k0 = N_GROUPS * GROUP_COLS
    v0 = k0 + N_GROUPS * GROUP_COLS
    z_p3 = zqkv_p.reshape(bsz, seq, 3 * N_GROUPS * GROUP_COLS)
    z_s = zqkv_s
    kv_prompt, kv_sample = [], []
    for g, (win, dil) in enumerate(DIL_GROUPS):
        keep = min(win, seq)
        kc = slice(k0 + g * GROUP_COLS, k0 + (g + 1) * GROUP_COLS)
        vc = slice(v0 + g * GROUP_COLS, v0 + (g + 1) * GROUP_COLS)
        kv = jnp.stack([z_p3[:, seq - keep:, kc], z_p3[:, seq - keep:, vc]], axis=2)
        kv_prompt.append(kv.reshape(1, bsz, keep, 2, HEADS_PER_GROUP, HEAD_DIM))
        kvs = jnp.stack([z_s[:, kc], z_s[:, vc]], axis=1)
        kv_sample.append(kvs.reshape(1, n_s, 1, 2, HEADS_PER_GROUP, HEAD_DIM))
    return (y_p.reshape(bsz, seq, d), y_s.reshape(n_s, 1, d),
            kv_prompt[0], kv_prompt[1], kv_prompt[2],
            kv_sample[0], kv_sample[1], kv_sample[2],
            vn_s.reshape(1, n_s, 1, aw))
```

```python
import functools

import jax
import jax.numpy as jnp
from jax import lax
from jax.experimental import pallas as pl
from jax.experimental.pallas import tpu as pltpu

F32 = jnp.float32
BF16 = jnp.bfloat16
U32 = jnp.uint32
I32 = jnp.int32
HIGHEST = lax.Precision.HIGHEST

EPS = 1e-6
A_GROUPS = 8
CHUNK = 128
HEAD_DIM = 128
HEADS_PER_GROUP = 4
DIL_GROUPS = ((128, 1), (512, 4), (2048, 16))
N_GROUPS = len(DIL_GROUPS)
N_HEADS = HEADS_PER_GROUP * N_GROUPS
ALIBI_SLOPES = tuple(2.0 ** (-8.0 * (h + 1) / N_HEADS) for h in range(N_HEADS))
GROUP_COLS = HEADS_PER_GROUP * HEAD_DIM
N_EXPERTS = 32
TOP_K = 4
SWIGLU_LIMIT = 7.0
SWIGLU_ALPHA = 1.702
N_ADA = 6
NEG = -1e30

LANES = 128
SUBLANES = 8
COL_TILE = 512
ATTN_BLOCK = 2048
ATTN_BATCH = 8
MOE_TILE = 512
MOE_QUARTER = MOE_TILE // 4
ROUTE_TILE = 512
VMEM_LIMIT = 56 * 1024 * 1024


def _params(sem, vmem=VMEM_LIMIT):
    return pltpu.CompilerParams(dimension_semantics=sem, vmem_limit_bytes=vmem)


def _dot_exact(a, b):
    return jnp.dot(a, b, precision=HIGHEST, preferred_element_type=F32)


def _dot_split(a, b):
    a_hi, b_hi = a.astype(BF16), b.astype(BF16)
    a_lo = (a - a_hi.astype(F32)).astype(BF16)
    b_lo = (b - b_hi.astype(F32)).astype(BF16)
    return (jnp.dot(a_hi, b_hi, preferred_element_type=F32) + jnp.dot(a_hi, b_lo, preferred_element_type=F32)
            + jnp.dot(a_lo, b_hi, preferred_element_type=F32))


def _dot(a, b, hi):
    if hi:
        return _dot_split(a, b)
    return jnp.dot(a.astype(BF16), b.astype(BF16), preferred_element_type=F32)


def _mod_rms(x, g, sc, sh):
    y = x * lax.rsqrt(jnp.mean(x * x, axis=-1, keepdims=True) + EPS)
    return y * g * (1.0 + sc) + sh


def _ada_kernel(c_ref, w_ref, b_ref, o_ref):
    c = c_ref[...]
    o_ref[...] = _dot(c * jax.nn.sigmoid(c), w_ref[...], True) + b_ref[...]


def _ada(c_all, w_ada, b_ada):
    n, d = c_all.shape
    cols = w_ada.shape[1]
    tn = 1024
    return pl.pallas_call(
        _ada_kernel,
        grid=(cols // tn,),
        in_specs=[pl.BlockSpec((n, d), lambda j: (0, 0)),
                  pl.BlockSpec((d, tn), lambda j: (0, j)),
                  pl.BlockSpec((1, tn), lambda j: (0, j))],
        out_specs=pl.BlockSpec((n, tn), lambda j: (0, j)),
        out_shape=jax.ShapeDtypeStruct((n, cols), F32),
        compiler_params=_params(("arbitrary",)),
        name="ada",
    )(c_all, w_ada, b_ada.reshape(1, cols))


def _mix_norm_kernel(x_ref, sc_ref, sh_ref, g_ref, h_ref):
    h_ref[...] = _mod_rms(x_ref[...], g_ref[...], sc_ref[0], sh_ref[0]).astype(h_ref.dtype)


def _mix_norm(x, sc, sh, g, *, rows_per_mod, tm, hi):
    t, d = x.shape
    mod_rows = sc.shape[1]
    mod_spec = pl.BlockSpec((1, mod_rows, d), lambda i: (i * tm // rows_per_mod, 0, 0))
    return pl.pallas_call(
        _mix_norm_kernel,
        grid=(t // tm,),
        in_specs=[pl.BlockSpec((tm, d), lambda i: (i, 0)), mod_spec, mod_spec, pl.BlockSpec((1, d), lambda i: (0, 0))],
        out_specs=pl.BlockSpec((tm, d), lambda i: (i, 0)),
        out_shape=jax.ShapeDtypeStruct((t, d), F32 if hi else BF16),
        compiler_params=_params(("arbitrary",)),
        name="mix_norm",
    )(x, sc, sh, g.reshape(1, d))


def _in_proj_kernel(h_ref, w_ref, o_ref, *, hi, act):
    z = _dot(h_ref[...], w_ref[...], hi)
    o_ref[...] = z if act is None else act(z)


def _in_proj(h, w_in, col0, n_cols, act, *, tm, hi):
    t, d = h.shape
    assert col0 % COL_TILE == 0 and n_cols % COL_TILE == 0
    blk0 = col0 // COL_TILE
    return pl.pallas_call(
        functools.partial(_in_proj_kernel, hi=hi, act=act),
        grid=(t // tm, n_cols // COL_TILE),
        in_specs=[pl.BlockSpec((tm, d), lambda i, j: (i, 0)),
                  pl.BlockSpec((d, COL_TILE), lambda i, j: (0, blk0 + j))],
        out_specs=pl.BlockSpec((tm, COL_TILE), lambda i, j: (i, j)),
        out_shape=jax.ShapeDtypeStruct((t, n_cols), F32),
        compiler_params=_params(("arbitrary", "arbitrary")),
        name="in_proj",
    )(h, w_in)


def _in_proj_all(x, sc, sh, g, w_in, *, rows_per_mod, tm_norm, tm, hi):
    d = x.shape[1]
    aw2 = 2 * A_GROUPS * CHUNK
    qkv = 3 * N_HEADS * HEAD_DIM
    h = _mix_norm(x, sc, sh, g, rows_per_mod=rows_per_mod, tm=tm_norm, hi=hi)
    z_a = _in_proj(h, w_in, 0, aw2, jax.nn.gelu, tm=tm, hi=hi)
    z_qkv = _in_proj(h, w_in, aw2, qkv, None, tm=tm, hi=hi)
    z_gate = _in_proj(h, w_in, aw2 + qkv, 2 * d, jax.nn.sigmoid, tm=tm, hi=hi)
    return z_a, z_qkv, z_gate


def _attend(q, kc, kp, vc, vp, slope, dil, prev_bias):
    row = lax.broadcasted_iota(I32, (CHUNK, CHUNK), 0)
    col = lax.broadcasted_iota(I32, (CHUNK, CHUNK), 1)
    pen_c = jnp.where(col <= row, -slope * ((row - col) * dil).astype(F32), NEG)
    pen_p = jnp.where(col >= row, -slope * ((row + CHUNK - col) * dil).astype(F32), NEG)
    scale = HEAD_DIM ** -0.5
    qk = (((2,), (2,)), ((0,), (0,)))
    pv = (((2,), (1,)), ((0,), (0,)))
    s_c = lax.dot_general(q, kc, qk, preferred_element_type=F32) * scale + pen_c[None]
    s_p = lax.dot_general(q, kp, qk, preferred_element_type=F32) * scale + pen_p[None] + prev_bias
    m = jnp.max(jnp.maximum(s_c, s_p), axis=2, keepdims=True)
    p_c = jnp.exp(s_c - m)
    p_p = jnp.exp(s_p - m)
    l = jnp.sum(p_c + p_p, axis=2, keepdims=True)
    o = (lax.dot_general((p_c / l).astype(BF16), vc, pv, preferred_element_type=F32)
         + lax.dot_general((p_p / l).astype(BF16), vp, pv, preferred_element_type=F32))
    return o, m + jnp.log(l)


def _dil_attn_kernel(slope_ref, *refs):
    ins = refs[:5 * N_GROUPS]
    o_ref = refs[5 * N_GROUPS]
    acc_refs = refs[5 * N_GROUPS + 1:5 * N_GROUPS + 1 + N_GROUPS]
    lse_refs = refs[5 * N_GROUPS + 1 + N_GROUPS:]
    head = pl.program_id(1)
    i = pl.program_id(2)
    no_prev = jnp.where(i > 0, 0.0, NEG)

    for g, (win, dil) in enumerate(DIL_GROUPS):
        q_ref, k_ref, v_ref, kp_ref, vp_ref = ins[5 * g:5 * g + 5]
        acc_ref, lse_ref = acc_refs[g], lse_refs[g]
        slope = slope_ref[g * HEADS_PER_GROUP + head]
        span = CHUNK * dil
        n_blocks = ATTN_BLOCK // CHUNK

        def rows(start, dil=dil):
            return pl.ds(start, CHUNK) if dil == 1 else pl.ds(start, CHUNK, stride=dil)

        cur = [rows((b // dil) * span + b % dil) for b in range(n_blocks)]
        prev = [rows((b // dil - 1) * span + b % dil) for b in range(n_blocks)]

        def gather(blocks, ref, first_ref=None):
            tiles = []
            for b in blocks:
                if first_ref is None:
                    tiles.append(ref[cur[b], :])
                elif b < dil:
                    tiles.append(first_ref[rows(b), :])
                else:
                    tiles.append(ref[prev[b], :])
            return jnp.stack(tiles).astype(BF16)

        for b0 in range(0, n_blocks, ATTN_BATCH):
            blocks = range(b0, b0 + ATTN_BATCH)
            first_blocks = b0 + lax.broadcasted_iota(I32, (ATTN_BATCH, 1, 1), 0) < dil
            o, lse = _attend(gather(blocks, q_ref), gather(blocks, k_ref), gather(blocks, k_ref, kp_ref),
                             gather(blocks, v_ref), gather(blocks, v_ref, vp_ref),
                             slope, dil, jnp.where(first_blocks, no_prev, 0.0))
            for i_b, b in enumerate(blocks):
                acc_ref[cur[b], :] = o[i_b]
                lse_ref[cur[b], :] = jnp.broadcast_to(lse[i_b], (CHUNK, HEAD_DIM))

    ls = [r[...] for r in lse_refs]
    m = jnp.maximum(jnp.maximum(ls[0], ls[1]), ls[2])
    e = [jnp.exp(l - m) for l in ls]
    den = e[0] + e[1] + e[2]
    o_ref[...] = (acc_refs[0][...] * (e[0] / den) + acc_refs[1][...] * (e[1] / den)
                  + acc_refs[2][...] * (e[2] / den))


def _dil_attn(z, bsz, seq):
    assert seq % ATTN_BLOCK == 0
    nblk = seq // ATTN_BLOCK
    a_blocks = 0
    in_specs = [pl.BlockSpec(memory_space=pltpu.SMEM)]
    for g, (win, dil) in enumerate(DIL_GROUPS):
        span = CHUNK * dil
        per_blk = ATTN_BLOCK // span
        for which in range(3):
            cb = a_blocks + (which * N_GROUPS + g) * HEADS_PER_GROUP
            in_specs.append(pl.BlockSpec((ATTN_BLOCK, HEAD_DIM), lambda b, h, i, cb=cb: (b * nblk + i, cb + h)))
        for which in (1, 2):
            cb = a_blocks + (which * N_GROUPS + g) * HEADS_PER_GROUP
            in_specs.append(pl.BlockSpec(
                (span, HEAD_DIM),
                lambda b, h, i, cb=cb, per_blk=per_blk: (jnp.maximum((b * nblk + i) * per_blk - 1, 0), cb + h)))
    slopes = jnp.asarray(ALIBI_SLOPES, F32)
    blk = pltpu.VMEM((ATTN_BLOCK, HEAD_DIM), F32)
    return pl.pallas_call(
        _dil_attn_kernel,
        grid=(bsz, HEADS_PER_GROUP, nblk),
        in_specs=in_specs,
        out_specs=pl.BlockSpec((ATTN_BLOCK, HEAD_DIM), lambda b, h, i: (b * nblk + i, h)),
        out_shape=jax.ShapeDtypeStruct((bsz * seq, GROUP_COLS), F32),
        scratch_shapes=[blk] * (2 * N_GROUPS),
        compiler_params=_params(("arbitrary", "arbitrary", "arbitrary")),
        name="dil_attn",
    )(slopes, *([z] * (5 * N_GROUPS)))


def _sample_attn_kernel(pen_ref, q_ref, k_ref, v_ref, c0_ref, c1_ref, c2_ref, o_ref):
    caches = (c0_ref, c1_ref, c2_ref)
    scale = HEAD_DIM ** -0.5
    heads = [(g, h) for g in range(N_GROUPS) for h in range(HEADS_PER_GROUP)]

    def head_rows(ref):
        return jnp.stack([ref[0, :, i * HEAD_DIM:(i + 1) * HEAD_DIM] for i in range(N_HEADS)])

    q = head_rows(q_ref)
    k_new = head_rows(k_ref)
    v_new = head_rows(v_ref)
    k_buf = jnp.stack([caches[g][:, 0, h, :] for g, h in heads])
    v_buf = jnp.stack([caches[g][:, 1, h, :] for g, h in heads])
    q8 = jnp.broadcast_to(q, (N_HEADS, SUBLANES, HEAD_DIM))
    qk = (((2,), (2,)), ((0,), (0,)))
    pv = (((2,), (1,)), ((0,), (0,)))
    s_buf = lax.dot_general(q8, k_buf, qk, precision=HIGHEST, preferred_element_type=F32)[:, 0:1, :]
    s_buf = s_buf * scale - pen_ref[...]
    s_new = jnp.sum(q * k_new, axis=2, keepdims=True) * scale
    m = jnp.maximum(jnp.max(s_buf, axis=2, keepdims=True), s_new)
    p_buf = jnp.exp(s_buf - m)
    p_new = jnp.exp(s_new - m)
    l = jnp.sum(p_buf, axis=2, keepdims=True) + p_new
    pb8 = jnp.broadcast_to(p_buf / l, (N_HEADS, SUBLANES, CHUNK))
    o = lax.dot_general(pb8, v_buf, pv, precision=HIGHEST, preferred_element_type=F32)[:, 0:1, :]
    o = o + (p_new / l) * v_new
    lse = m + jnp.log(l)
    n = HEADS_PER_GROUP
    ls = [lse[g * n:(g + 1) * n] for g in range(N_GROUPS)]
    mm = jnp.maximum(jnp.maximum(ls[0], ls[1]), ls[2])
    e = [jnp.exp(x - mm) for x in ls]
    den = e[0] + e[1] + e[2]
    out = o[0:n] * (e[0] / den) + o[n:2 * n] * (e[1] / den) + o[2 * n:3 * n] * (e[2] / den)
    for h in range(HEADS_PER_GROUP):
        o_ref[0, :, h * HEAD_DIM:(h + 1) * HEAD_DIM] = out[h]


def _sample_attn(zs, caches):
    n, cols = zs.shape
    z3 = zs.reshape(n, 1, cols)
    a_blocks = 0
    qkv_cols = N_GROUPS * GROUP_COLS
    c_views = []
    c_specs = []
    for g, (win, dil) in enumerate(DIL_GROUPS):
        c = caches[g]
        assert c.shape[1] == win, "cache must hold exactly one window"
        c_views.append(c.reshape(n, win // dil, dil, 2, HEADS_PER_GROUP, HEAD_DIM))
        c_specs.append(pl.BlockSpec((None, win // dil, None, 2, HEADS_PER_GROUP, HEAD_DIM),
                                    lambda b: (b, 0, 0, 0, 0, 0)))
    q0 = a_blocks * GROUP_COLS
    steps = jnp.arange(CHUNK, 0, -1, dtype=F32)
    pen = jnp.stack([ALIBI_SLOPES[g * HEADS_PER_GROUP + h] * dil * steps
                     for g, (win, dil) in enumerate(DIL_GROUPS) for h in range(HEADS_PER_GROUP)])
    assert all(win // dil == CHUNK for win, dil in DIL_GROUPS)
    row_spec = pl.BlockSpec((1, 1, qkv_cols), lambda b: (b, 0, 0))
    out = pl.pallas_call(
        _sample_attn_kernel,
        grid=(n,),
        in_specs=[pl.BlockSpec((N_HEADS, 1, CHUNK), lambda b: (0, 0, 0)), row_spec, row_spec, row_spec] + c_specs,
        out_specs=pl.BlockSpec((1, 1, GROUP_COLS), lambda b: (b, 0, 0)),
        out_shape=jax.ShapeDtypeStruct((n, 1, GROUP_COLS), F32),
        compiler_params=_params(("arbitrary",)),
        name="sample_attn",
    )(pen.reshape(N_HEADS, 1, CHUNK), z3[:, :, q0:q0 + qkv_cols], z3[:, :, q0 + qkv_cols:q0 + 2 * qkv_cols],
      z3[:, :, q0 + 2 * qkv_cols:q0 + 3 * qkv_cols], *c_views)
    return out.reshape(n, GROUP_COLS)


def _layer_norm(v, g, b):
    mu = jnp.mean(v, axis=-1, keepdims=True)
    var = jnp.mean(jnp.square(v - mu), axis=-1, keepdims=True)
    return (v - mu) * lax.rsqrt(var + EPS) * g + b


def _chunk_gate_kernel(u_ref, v_ref, lg_ref, lb_ref, ws_ref, bs_ref, o_ref, *, n_chunks):
    va = _layer_norm(v_ref[...], lg_ref[...], lb_ref[...]).astype(BF16)
    row = lax.broadcasted_iota(I32, (CHUNK, CHUNK), 0)
    col = lax.broadcasted_iota(I32, (CHUNK, CHUNK), 1)
    for g in range(A_GROUPS):
        w = jnp.where(col <= row, ws_ref[g], 0.0).astype(BF16)
        gs = slice(g * CHUNK, (g + 1) * CHUNK)
        for c in range(n_chunks):
            rs = slice(c * CHUNK, (c + 1) * CHUNK)
            mixed = jnp.dot(w, va[rs, gs], preferred_element_type=F32) + bs_ref[g]
            o_ref[rs, gs] = (u_ref[rs, gs] * mixed).astype(o_ref.dtype)


def _chunk_gate(z, ln_g, ln_b, w_s, b_s, *, tm):
    t = z.shape[0]
    aw = A_GROUPS * CHUNK
    bs_full = jnp.broadcast_to(b_s[:, :, None], (A_GROUPS, CHUNK, CHUNK))
    kern = functools.partial(_chunk_gate_kernel, n_chunks=tm // CHUNK)
    return pl.pallas_call(
        kern,
        grid=(t // tm,),
        in_specs=[pl.BlockSpec((tm, aw), lambda i: (i, 0)),
                  pl.BlockSpec((tm, aw), lambda i: (i, 1)),
                  pl.BlockSpec((1, aw), lambda i: (0, 0)),
                  pl.BlockSpec((1, aw), lambda i: (0, 0)),
                  pl.BlockSpec((A_GROUPS, CHUNK, CHUNK), lambda i: (0, 0, 0)),
                  pl.BlockSpec((A_GROUPS, CHUNK, CHUNK), lambda i: (0, 0, 0))],
        out_specs=pl.BlockSpec((tm, aw), lambda i: (i, 0)),
        out_shape=jax.ShapeDtypeStruct((t, aw), BF16),
        compiler_params=_params(("arbitrary",)),
        name="chunk_gate",
    )(z, z, ln_g.reshape(1, aw), ln_b.reshape(1, aw), w_s, bs_full)


def _sample_gate_kernel(u_ref, v_ref, lg_ref, lb_ref, w0_ref, b0_ref, a_ref, vn_ref):
    va = _layer_norm(v_ref[...], lg_ref[...], lb_ref[...])
    vn_ref[...] = va
    a_ref[...] = u_ref[...] * (w0_ref[...] * va + b0_ref[...])


def _sample_gate(zs, ln_g, ln_b, w_s, b_s):
    n = zs.shape[0]
    aw = A_GROUPS * CHUNK
    w0 = jnp.repeat(w_s[:, 0, 0], CHUNK).reshape(1, aw)
    b0 = jnp.repeat(b_s[:, 0], CHUNK).reshape(1, aw)
    vec = pl.BlockSpec((1, aw), lambda i: (0, 0))
    return pl.pallas_call(
        _sample_gate_kernel,
        grid=(1,),
        in_specs=[pl.BlockSpec((n, aw), lambda i: (0, 0)), pl.BlockSpec((n, aw), lambda i: (0, 1)),
                  vec, vec, vec, vec],
        out_specs=[pl.BlockSpec((n, aw), lambda i: (0, 0)), pl.BlockSpec((n, aw), lambda i: (0, 0))],
        out_shape=[jax.ShapeDtypeStruct((n, aw), F32), jax.ShapeDtypeStruct((n, aw), F32)],
        compiler_params=_params(("arbitrary",)),
        name="sample_gate",
    )(zs, zs, ln_g.reshape(1, aw), ln_b.reshape(1, aw), w0, b0)


def _merge_kernel(a_ref, b_ref, ga_ref, gb_ref, pa_ref, pb_ref, out_ref, *, hi):
    ya = _dot(a_ref[...], pa_ref[...], hi)
    yb = _dot(b_ref[...], pb_ref[...], hi)
    out_ref[...] = (ga_ref[...] * ya + gb_ref[...] * yb).astype(out_ref.dtype)


def _merge(a_out, b_out, z, p_a, p_b, *, tm, hi):
    t, aw = a_out.shape
    d = p_a.shape[1]
    tn = COL_TILE
    ga_blk0 = 0
    gb_blk0 = d // tn
    return pl.pallas_call(
        functools.partial(_merge_kernel, hi=hi),
        grid=(t // tm, d // tn),
        in_specs=[pl.BlockSpec((tm, aw), lambda i, j: (i, 0)),
                  pl.BlockSpec((tm, GROUP_COLS), lambda i, j: (i, 0)),
                  pl.BlockSpec((tm, tn), lambda i, j: (i, ga_blk0 + j)),
                  pl.BlockSpec((tm, tn), lambda i, j: (i, gb_blk0 + j)),
                  pl.BlockSpec((aw, tn), lambda i, j: (0, j)),
                  pl.BlockSpec((GROUP_COLS, tn), lambda i, j: (0, j))],
        out_specs=pl.BlockSpec((tm, tn), lambda i, j: (i, j)),
        out_shape=jax.ShapeDtypeStruct((t, d), F32 if hi else BF16),
        compiler_params=_params(("arbitrary", "arbitrary")),
        name="merge",
    )(a_out, b_out, z, z, p_a, p_b)


def _out_proj_kernel(m_ref, w_ref, x_ref, gt_ref, o_ref, *, hi):
    o_ref[...] = x_ref[...] + gt_ref[0] * _dot(m_ref[...], w_ref[...], hi)


def _out_proj(merged, w_o, x, gt, *, rows_per_mod, tm, hi):
    t, d = x.shape
    tn = COL_TILE
    mod_rows = gt.shape[1]
    return pl.pallas_call(
        functools.partial(_out_proj_kernel, hi=hi),
        grid=(t // tm, d // tn),
        in_specs=[pl.BlockSpec((tm, d), lambda i, j: (i, 0)),
                  pl.BlockSpec((d, tn), lambda i, j: (0, j)),
                  pl.BlockSpec((tm, tn), lambda i, j: (i, j)),
                  pl.BlockSpec((1, mod_rows, tn), lambda i, j: (i * tm // rows_per_mod, 0, j))],
        out_specs=pl.BlockSpec((tm, tn), lambda i, j: (i, j)),
        out_shape=jax.ShapeDtypeStruct((t, d), F32),
        compiler_params=_params(("arbitrary", "arbitrary")),
        name="out_proj",
    )(merged, w_o, x, gt)


def _pack_pair(lo, hi, pair_ref):
    n = lo.shape[0]
    pair_ref[pl.ds(0, n, stride=2), :] = lo
    pair_ref[pl.ds(1, n, stride=2), :] = hi
    return pltpu.bitcast(pair_ref[0:2 * n, :].astype(BF16), U32)


def _unpack_pair(w, pair_ref):
    n = w.shape[0]
    pair_ref[0:2 * n, :] = pltpu.bitcast(w, BF16).astype(F32)
    return pair_ref[pl.ds(0, n, stride=2), :].astype(BF16), pair_ref[pl.ds(1, n, stride=2), :].astype(BF16)


def _ffn_norm_kernel(x_ref, sc_ref, sh_ref, g_ref, wr_ref, br_ref, hp_ref, e_ref, w_ref, pair_ref, *, hi):
    h = _mod_rms(x_ref[...], g_ref[...], sc_ref[0], sh_ref[0])
    tm, d = h.shape
    for s in range(d // (2 * LANES)):
        even = h[:, (2 * s) * LANES:(2 * s + 1) * LANES]
        odd = h[:, (2 * s + 1) * LANES:(2 * s + 2) * LANES]
        hp_ref[pl.ds(s, tm, stride=SUBLANES), :] = _pack_pair(even, odd, pair_ref)
    logits = (_dot_exact if hi else _dot_split)(h, wr_ref[...]) + br_ref[...]
    lane = lax.broadcasted_iota(I32, logits.shape, 1)
    lane_f = lane.astype(F32)
    vals, idxs = [], []
    for _ in range(TOP_K):
        m = jnp.max(logits, axis=1, keepdims=True)
        idx = jnp.min(jnp.where(logits == m, lane_f, float(logits.shape[1])), axis=1, keepdims=True)
        vals.append(m)
        idxs.append(idx)
        logits = jnp.where(lane_f == idx, 2.0 * NEG, logits)
    es = [jnp.exp(v - vals[0]) for v in vals]
    den = es[0] + es[1] + es[2] + es[3]
    e_out = jnp.full(lane.shape, -1.0, F32)
    w_out = jnp.zeros(lane.shape, F32)
    for k in range(TOP_K):
        e_out = jnp.where(lane == k, idxs[k], e_out)
        w_out = jnp.where(lane == k, es[k] / den, w_out)
    e_ref[...] = e_out.astype(I32)
    w_ref[...] = w_out


def _ffn_norm(x1, sc, sh, g, w_router, b_router, *, rows_per_mod, tm, hi):
    t, d = x1.shape
    assert d == 2 * LANES * SUBLANES, "one packed row must be exactly one (8, 128) tile"
    ne = w_router.shape[1]
    wr = jnp.zeros((d, LANES), F32).at[:, :ne].set(w_router)
    br = jnp.full((1, LANES), NEG, F32).at[0, :ne].set(b_router)
    mod_rows = sc.shape[1]
    mod_spec = pl.BlockSpec((1, mod_rows, d), lambda i: (i * tm // rows_per_mod, 0, 0))
    return pl.pallas_call(
        functools.partial(_ffn_norm_kernel, hi=hi),
        grid=(t // tm,),
        in_specs=[pl.BlockSpec((tm, d), lambda i: (i, 0)), mod_spec, mod_spec,
                  pl.BlockSpec((1, d), lambda i: (0, 0)),
                  pl.BlockSpec((d, LANES), lambda i: (0, 0)),
                  pl.BlockSpec((1, LANES), lambda i: (0, 0))],
        out_specs=[pl.BlockSpec((tm * SUBLANES, LANES), lambda i: (i, 0)),
                   pl.BlockSpec((tm, LANES), lambda i: (i, 0)),
                   pl.BlockSpec((tm, LANES), lambda i: (i, 0))],
        out_shape=[jax.ShapeDtypeStruct((t * SUBLANES, LANES), U32),
                   jax.ShapeDtypeStruct((t, LANES), I32),
                   jax.ShapeDtypeStruct((t, LANES), F32)],
        scratch_shapes=[pltpu.VMEM((2 * tm, LANES), F32)],
        compiler_params=_params(("arbitrary",)),
        name="ffn_norm",
    )(x1, sc, sh, g.reshape(1, d), wr, br)


def _route_kernel(e_ref, pos_ref, cnt_ref, tri_ref, run_ref):
    phase = pl.program_id(0)
    i = pl.program_id(1)
    e = e_ref[...]
    tm = e.shape[0]
    lane = lax.broadcasted_iota(I32, (tm, LANES), 1)
    hits = [lane == e[:, k:k + 1] for k in range(TOP_K)]
    chosen = jnp.zeros((tm, LANES), F32)
    for k in range(TOP_K):
        chosen = jnp.where(hits[k], 1.0, chosen)
    col_count = jnp.sum(chosen, axis=0, keepdims=True)

    @pl.when((phase == 0) & (i == 0))
    def _():
        cnt_ref[...] = jnp.zeros(cnt_ref.shape, F32)
        r = lax.broadcasted_iota(I32, (tm, tm), 0)
        c = lax.broadcasted_iota(I32, (tm, tm), 1)
        tri_ref[...] = jnp.where(c < r, 1.0, 0.0).astype(BF16)

    @pl.when(phase == 0)
    def _():
        cnt_ref[...] = cnt_ref[...] + col_count

    @pl.when((phase == 1) & (i == 0))
    def _():
        tiles = jnp.floor((cnt_ref[...] + (MOE_TILE - 1)) * (1.0 / MOE_TILE))
        r = lax.broadcasted_iota(I32, (LANES, LANES), 0)
        c = lax.broadcasted_iota(I32, (LANES, LANES), 1)
        below = jnp.where(r < c, 1.0, 0.0)
        tiles8 = jnp.broadcast_to(tiles, (SUBLANES, LANES))
        run_ref[...] = _dot_exact(tiles8, below)[0:1] * float(MOE_TILE)

    @pl.when(phase == 1)
    def _():
        before = jnp.dot(tri_ref[...], chosen.astype(BF16), preferred_element_type=F32)
        dest = run_ref[...] + before
        out = jnp.zeros((tm, LANES), F32)
        for k in range(TOP_K):
            p = jnp.sum(jnp.where(hits[k], dest, 0.0), axis=1, keepdims=True)
            out = jnp.where(lane == k, p, out)
        pos_ref[...] = out.astype(I32)
        run_ref[...] = run_ref[...] + col_count


def _route(e_all):
    t = e_all.shape[0]
    steps = t // ROUTE_TILE
    return pl.pallas_call(
        _route_kernel,
        grid=(2, steps),
        in_specs=[pl.BlockSpec((ROUTE_TILE, LANES), lambda p, i: (i, 0))],
        out_specs=[pl.BlockSpec((ROUTE_TILE, LANES), lambda p, i: (i * p, 0)),
                   pl.BlockSpec((1, LANES), lambda p, i: (0, 0))],
        out_shape=[jax.ShapeDtypeStruct((t, LANES), I32), jax.ShapeDtypeStruct((1, LANES), F32)],
        scratch_shapes=[pltpu.VMEM((ROUTE_TILE, ROUTE_TILE), BF16), pltpu.VMEM((1, LANES), F32)],
        compiler_params=_params(("arbitrary", "arbitrary")),
        name="route",
    )(e_all)


def _row_scatter_kernel(pos_ref, pos_s_ref, pad_base_ref, pad_cnt_ref, hp_ref, hs_ref, xs_hbm, sem, *,
                        n_prompt_steps, n_sample):
    i = pl.program_id(0)

    def rows(first, n=1):
        return pl.ds(pl.multiple_of(first * SUBLANES, SUBLANES), n * SUBLANES)

    def scatter(src_ref, idx_ref, n_tok):
        def body(tt, carry):
            for k in range(TOP_K):
                pltpu.make_async_copy(src_ref.at[rows(tt)], xs_hbm.at[rows(idx_ref[0, 0, tt * TOP_K + k])],
                                      sem).start(priority=k % 2)
            return carry
        lax.fori_loop(0, n_tok, body, 0, unroll=2)
        for _ in range(TOP_K):
            pltpu.make_async_copy(src_ref.at[rows(0, n_tok)], xs_hbm.at[rows(0, n_tok)], sem).wait()

    @pl.when(i < n_prompt_steps)
    def _():
        scatter(hp_ref, pos_ref, ROUTE_TILE)

    @pl.when(i == n_prompt_steps)
    def _():
        scatter(hs_ref, pos_s_ref, n_sample)

        def fill(dst_row, size):
            return pltpu.make_async_copy(hp_ref.at[rows(0, size)], xs_hbm.at[rows(dst_row, size)], sem)

        def per_segment(wait):
            def body(e, carry):
                cnt = pad_cnt_ref[e]
                base = pad_base_ref[e]

                def whole(r, c):
                    cp = fill(base + (cnt % MOE_TILE) + r * MOE_TILE, MOE_TILE)
                    cp.wait() if wait else cp.start()
                    return c
                lax.fori_loop(0, cnt // MOE_TILE, whole, 0)
                size = MOE_TILE // 2
                while size >= 1:
                    @pl.when((cnt & size) != 0)
                    def _(size=size):
                        cp = fill(base + (cnt & (size - 1)), size)
                        cp.wait() if wait else cp.start()
                    size //= 2
                return carry
            return body
        lax.fori_loop(0, pad_cnt_ref.shape[0], per_segment(False), 0)
        lax.fori_loop(0, pad_cnt_ref.shape[0], per_segment(True), 0)


def _row_scatter(hp_p, hp_s, pos_p, pos_s, pad_base, pad_cnt, n_rows):
    t_p = hp_p.shape[0] // SUBLANES
    n_s = hp_s.shape[0] // SUBLANES
    assert ROUTE_TILE >= MOE_TILE, "padding blocks are copied from one token tile"
    steps = t_p // ROUTE_TILE
    tile_rows = ROUTE_TILE * TOP_K
    return pl.pallas_call(
        functools.partial(_row_scatter_kernel, n_prompt_steps=steps, n_sample=n_s),
        grid=(steps + 1,),
        in_specs=[pl.BlockSpec((1, 1, tile_rows), lambda i: (jnp.minimum(i, steps - 1), 0, 0),
                               memory_space=pltpu.SMEM),
                  pl.BlockSpec((1, 1, n_s * TOP_K), lambda i: (0, 0, 0), memory_space=pltpu.SMEM),
                  pl.BlockSpec(memory_space=pltpu.SMEM),
                  pl.BlockSpec(memory_space=pltpu.SMEM),
                  pl.BlockSpec((ROUTE_TILE * SUBLANES, LANES), lambda i: (jnp.minimum(i, steps - 1), 0)),
                  pl.BlockSpec((n_s * SUBLANES, LANES), lambda i: (0, 0))],
        out_specs=pl.BlockSpec(memory_space=pl.ANY),
        out_shape=jax.ShapeDtypeStruct((n_rows * SUBLANES, LANES), U32),
        scratch_shapes=[pltpu.SemaphoreType.DMA(())],
        compiler_params=_params(("arbitrary",)),
        name="row_scatter",
    )(pos_p.reshape(steps, 1, tile_rows), pos_s.reshape(1, 1, n_s * TOP_K), pad_base, pad_cnt, hp_p, hp_s)


def _expert_weights(seg_ref, sege_ref, nu_ref, fetch, on_ready):
    j = pl.program_id(0)
    t = pl.program_id(1)
    n_j = pl.num_programs(0)
    seg = seg_ref[t]
    n_seg = nu_ref[1]
    first = (t < nu_ref[0]) & ((t == 0) | (seg != seg_ref[jnp.maximum(t - 1, 0)]))
    k = j * n_seg + seg
    slot = lax.rem(k, 2)

    @pl.when(first)
    def _():
        @pl.when(k == 0)
        def _():
            for cp in fetch(sege_ref[0], 0, 0):
                cp.start()

        more = seg + 1 < n_seg

        @pl.when(more | (j + 1 < n_j))
        def _():
            for cp in fetch(sege_ref[jnp.where(more, seg + 1, 0)], jnp.where(more, j, j + 1), 1 - slot):
                cp.start()

        for cp in fetch(sege_ref[seg], j, slot):
            cp.wait()
        on_ready(slot)


def _moe_up_kernel(te_ref, nv_ref, seg_ref, sege_ref, nu_ref, x_ref, wg_hbm, wu_hbm, bg_ref, bu_ref, h_ref,
                   wg_buf, wu_buf, sem, wg_s, wu_s, xb_s, pair_ref):
    t = pl.program_id(1)
    tf = wg_s.shape[1]

    def fetch(e, j, slot):
        cols = pl.ds(pl.multiple_of(j * tf, tf), tf)
        return [pltpu.make_async_copy(wg_hbm.at[e, :, cols], wg_buf.at[slot], sem.at[slot]),
                pltpu.make_async_copy(wu_hbm.at[e, :, cols], wu_buf.at[slot], sem.at[slot])]

    def on_ready(slot):
        wg_s[...] = wg_buf[slot].astype(BF16)
        wu_s[...] = wu_buf[slot].astype(BF16)

    _expert_weights(seg_ref, sege_ref, nu_ref, fetch, on_ready)

    def compute(rows):
        for s in range(SUBLANES):
            lo, hi = _unpack_pair(x_ref[pl.ds(s, rows, stride=SUBLANES), :], pair_ref)
            xb_s[0:rows, (2 * s) * LANES:(2 * s + 1) * LANES] = lo
            xb_s[0:rows, (2 * s + 1) * LANES:(2 * s + 2) * LANES] = hi
        x = xb_s[0:rows, :]
        gt = jnp.dot(x, wg_s[...], preferred_element_type=F32) + bg_ref[0]
        up = jnp.dot(x, wu_s[...], preferred_element_type=F32) + bu_ref[0]
        gt = jnp.minimum(gt, SWIGLU_LIMIT)
        up = jnp.clip(up, -SWIGLU_LIMIT, SWIGLU_LIMIT)
        h_ref[0:rows, :] = ((up + 1.0) * gt * jax.nn.sigmoid(SWIGLU_ALPHA * gt)).astype(h_ref.dtype)

    quarters = jnp.where(t < nu_ref[0], (nv_ref[t] + MOE_QUARTER - 1) // MOE_QUARTER, 0)
    for n_q in range(MOE_TILE // MOE_QUARTER + 1):
        @pl.when(quarters == n_q)
        def _(rows=n_q * MOE_QUARTER):
            if rows > 0:
                compute(rows)
            if rows < MOE_TILE:
                h_ref[rows:, :] = jnp.zeros((MOE_TILE - rows, h_ref.shape[1]), h_ref.dtype)


def _moe_down_kernel(te_ref, nv_ref, seg_ref, sege_ref, nu_ref, h_ref, wd_hbm, bd_ref, y_ref, wd_buf, sem, wd_s):
    t = pl.program_id(1)
    n_sub = y_ref.shape[0] // MOE_TILE
    tn = wd_s.shape[1]

    def fetch(e, j, slot):
        cols = pl.ds(pl.multiple_of(j * tn, tn), tn)
        return [pltpu.make_async_copy(wd_hbm.at[e, :, cols], wd_buf.at[slot], sem.at[slot])]

    def on_ready(slot):
        wd_s[...] = wd_buf[slot].astype(BF16)

    _expert_weights(seg_ref, sege_ref, nu_ref, fetch, on_ready)

    def store(rows, y):
        for c in range(n_sub):
            y_ref[pl.ds(c, rows, stride=n_sub), :] = y[:, c * LANES:(c + 1) * LANES]

    def compute(rows):
        store(rows, jnp.dot(h_ref[0:rows, :], wd_s[...], preferred_element_type=F32) + bd_ref[0])

    quarters = jnp.where(t < nu_ref[0], (nv_ref[t] + MOE_QUARTER - 1) // MOE_QUARTER, 0)
    for n_q in range(MOE_TILE // MOE_QUARTER + 1):
        @pl.when(quarters == n_q)
        def _(rows=n_q * MOE_QUARTER):
            if rows > 0:
                compute(rows)
            if rows < MOE_TILE:
                y_ref[rows * n_sub:, :] = jnp.zeros(((MOE_TILE - rows) * n_sub, LANES), F32)


def _moe_experts(x_sorted, plan, w_gate, b_gate, w_up, b_up, w_down, b_down):
    rows = x_sorted.shape[0] // SUBLANES
    ne, d, f = w_gate.shape
    nt = rows // MOE_TILE
    tf = 1024
    tn = d
    n_sub = d // LANES
    n_plan = len(plan)

    def row_map(j, t, te, nv, seg, sege, nu):
        return (jnp.minimum(t, nu[0] - 1), 0)

    def b_map(j, t, te, nv, seg, sege, nu):
        return (te[t], 0, j)

    hbm = pl.BlockSpec(memory_space=pl.ANY)
    h = pl.pallas_call(
        _moe_up_kernel,
        grid_spec=pltpu.PrefetchScalarGridSpec(
            num_scalar_prefetch=n_plan,
            grid=(f // tf, nt),
            in_specs=[pl.BlockSpec((MOE_TILE * SUBLANES, LANES), row_map), hbm, hbm,
                      pl.BlockSpec((1, 1, tf), b_map),
                      pl.BlockSpec((1, 1, tf), b_map)],
            out_specs=pl.BlockSpec((MOE_TILE, tf), lambda j, t, *_: (t, j)),
            scratch_shapes=[pltpu.VMEM((2, d, tf), F32), pltpu.VMEM((2, d, tf), F32), pltpu.SemaphoreType.DMA((2,)),
                            pltpu.VMEM((d, tf), BF16), pltpu.VMEM((d, tf), BF16), pltpu.VMEM((MOE_TILE, d), BF16),
                            pltpu.VMEM((2 * MOE_TILE, LANES), F32)]),
        out_shape=jax.ShapeDtypeStruct((rows, f), BF16),
        compiler_params=_params(("arbitrary", "arbitrary")),
        name="moe_up",
    )(*plan, x_sorted, w_gate, w_up, b_gate.reshape(ne, 1, f), b_up.reshape(ne, 1, f))

    y = pl.pallas_call(
        _moe_down_kernel,
        grid_spec=pltpu.PrefetchScalarGridSpec(
            num_scalar_prefetch=n_plan,
            grid=(d // tn, nt),
            in_specs=[pl.BlockSpec((MOE_TILE, f), row_map), hbm,
                      pl.BlockSpec((1, 1, tn), b_map)],
            out_specs=pl.BlockSpec((MOE_TILE * n_sub, LANES), lambda j, t, *_: (t, 0)),
            scratch_shapes=[pltpu.VMEM((2, f, tn), F32), pltpu.SemaphoreType.DMA((2,)), pltpu.VMEM((f, tn), BF16)]),
        out_shape=jax.ShapeDtypeStruct((rows * n_sub, LANES), F32),
        compiler_params=_params(("arbitrary", "arbitrary")),
        name="moe_down",
    )(*plan, h, w_down, b_down.reshape(ne, 1, d))
    return y.reshape(rows, d // LANES, LANES)


def _row_copy(src_hbm, buf, sem, slot, src_row, dst_row, n_sub):
    dst = pl.ds(pl.multiple_of(dst_row * n_sub, SUBLANES), n_sub)
    return pltpu.make_async_copy(src_hbm.at[src_row], buf.at[slot, dst], sem.at[slot])


def _issue_rows(idx_ref, src_hbm, buf, sem, slot, n_rows, n_sub):
    def body(pair, carry):
        for p in range(2):
            r = 2 * pair + p
            _row_copy(src_hbm, buf, sem, slot, idx_ref[0, 0, r], r, n_sub).start(priority=p)
        return carry
    lax.fori_loop(0, n_rows // 2, body, 0, unroll=4)


def _moe_combine_kernel(idx_ref, nxt_ref, wt_ref, y_hbm, x_ref, gt_ref, g_ref, o_ref, buf, sem, acc_ref, *, tok):
    i = pl.program_id(0)
    n = pl.num_programs(0)
    n_rows = TOP_K * tok
    n_sub = y_hbm.shape[1]
    slot = lax.rem(i, 2)

    @pl.when(i == 0)
    def _():
        _issue_rows(idx_ref, y_hbm, buf, sem, 0, n_rows, n_sub)

    for s in range(2):
        @pl.when((i + 1 < n) & (slot == s))
        def _():
            _issue_rows(nxt_ref, y_hbm, buf, sem, 1 - s, n_rows, n_sub)

    pltpu.make_async_copy(buf.at[slot], buf.at[slot], sem.at[slot]).wait()

    def per_token(t, carry):
        acc = None
        for k in range(TOP_K):
            first = pl.multiple_of((k * tok + t) * n_sub, n_sub)
            part = wt_ref[0, 0, k * tok + t] * buf[slot, pl.ds(first, n_sub), :]
            acc = part if acc is None else acc + part
        acc_ref[pl.ds(pl.multiple_of(t * n_sub, n_sub), n_sub), :] = acc
        return carry
    lax.fori_loop(0, tok, per_token, 0, unroll=4)

    sumsq = jnp.zeros((tok, 1), F32)
    for c in range(n_sub):
        cs = slice(c * LANES, (c + 1) * LANES)
        x = x_ref[:, cs] + gt_ref[0, :, cs] * acc_ref[pl.ds(c, tok, stride=n_sub), :]
        o_ref[:, cs] = x
        sumsq = sumsq + jnp.sum(x * x, axis=1, keepdims=True)
    o_ref[...] = o_ref[...] * lax.rsqrt(sumsq / (n_sub * LANES) + EPS) * g_ref[...]


def _moe_combine(y_rows, pos, top_w, x1, gt, g_final, *, rows_per_mod, tok):
    t, d = x1.shape
    n_sub = y_rows.shape[1]
    steps = t // tok
    n_rows = TOP_K * tok
    def k_major(a):
        return a[:, :TOP_K].reshape(steps, tok, TOP_K).transpose(0, 2, 1).reshape(steps, 1, n_rows)

    idx3 = k_major(pos)
    mod_rows = gt.shape[1]
    idx_blk = (1, 1, n_rows)
    return pl.pallas_call(
        functools.partial(_moe_combine_kernel, tok=tok),
        grid=(steps,),
        in_specs=[pl.BlockSpec(idx_blk, lambda i: (i, 0, 0), memory_space=pltpu.SMEM),
                  pl.BlockSpec(idx_blk, lambda i: (jnp.minimum(i + 1, steps - 1), 0, 0), memory_space=pltpu.SMEM),
                  pl.BlockSpec(idx_blk, lambda i: (i, 0, 0), memory_space=pltpu.SMEM),
                  pl.BlockSpec(memory_space=pl.ANY),
                  pl.BlockSpec((tok, d), lambda i: (i, 0)),
                  pl.BlockSpec((1, mod_rows, d), lambda i: (i * tok // rows_per_mod, 0, 0)),
                  pl.BlockSpec((1, d), lambda i: (0, 0))],
        out_specs=pl.BlockSpec((tok, d), lambda i: (i, 0)),
        out_shape=jax.ShapeDtypeStruct((t, d), F32),
        scratch_shapes=[pltpu.VMEM((2, n_rows * n_sub, LANES), F32), pltpu.SemaphoreType.DMA((2,)),
                        pltpu.VMEM((tok * n_sub, LANES), F32)],
        compiler_params=_params(("arbitrary",)),
        name="moe_combine",
    )(idx3, idx3, k_major(top_w), y_rows, x1, gt, g_final.reshape(1, d))


def _tile_table(counts, n_tiles):
    tiles_e = (counts + MOE_TILE - 1) // MOE_TILE
    tile_end = jnp.cumsum(tiles_e)
    tile_start = tile_end - tiles_e
    n_used = tile_end[-1]
    tile_ids = jnp.minimum(jnp.arange(n_tiles, dtype=I32), n_used - 1)
    tile_expert = jnp.minimum(jnp.sum(tile_end[None, :] <= tile_ids[:, None], axis=1), N_EXPERTS - 1).astype(I32)
    tile_valid = jnp.clip(counts[tile_expert] - (tile_ids - tile_start[tile_expert]) * MOE_TILE, 0, MOE_TILE)
    nonempty = tiles_e > 0
    seg_of_expert = jnp.cumsum(nonempty.astype(I32)) - 1
    n_seg = jnp.sum(nonempty.astype(I32))
    experts = jnp.arange(N_EXPERTS, dtype=I32)
    hit = nonempty[None, :] & (seg_of_expert[None, :] == jnp.arange(N_EXPERTS + 1, dtype=I32)[:, None])
    seg_expert = jnp.sum(jnp.where(hit, experts[None, :], 0), axis=1).astype(I32)
    tile_seg = seg_of_expert[tile_expert].astype(I32)
    plan = (tile_expert, tile_valid.astype(I32), tile_seg, seg_expert, jnp.stack([n_used, n_seg]).astype(I32))
    pad_base = jnp.concatenate([tile_start * MOE_TILE + counts, (n_used * MOE_TILE).reshape(1)])
    pad_cnt = jnp.concatenate([tiles_e * MOE_TILE - counts, ((n_tiles - n_used) * MOE_TILE).reshape(1)])
    return plan, pad_base.astype(I32), pad_cnt.astype(I32)


def _row_tile(t, want):
    tm = min(t, want)
    assert t % tm == 0
    return tm


def kernel(x_prompt, x_sample, cache_kv_g0, cache_kv_g1, cache_kv_g2, c_prompt, c_sample, w_ada, b_ada, g_mix, w_in, ln_g, ln_b, w_s, b_s, p_a, p_b, w_o, g_ffn, w_router, b_router, w_gate, b_gate, w_up, b_up, w_down, b_down, g_final):
    depth = w_ada.shape[0]
    assert depth == 1, "single-layer trunk"
    bsz, seq, d = x_prompt.shape
    n_s, dec_seq, _ = x_sample.shape
    assert dec_seq == 1, "one new position per sample"
    caches = (cache_kv_g0, cache_kv_g1, cache_kv_g2)
    l = 0
    cols = w_in.shape[2]
    t_p = bsz * seq
    aw = A_GROUPS * CHUNK

    n_c = bsz + n_s
    n_c_pad = -(-n_c // SUBLANES) * SUBLANES
    c_all = jnp.concatenate([c_prompt, c_sample, jnp.zeros((n_c_pad - n_c, d), F32)], axis=0)
    mod = _ada(c_all, w_ada[l], b_ada[l])
    mod_p = mod[:bsz].reshape(bsz, 1, N_ADA, d)
    mod_s = mod[bsz:n_c].reshape(1, n_s, N_ADA, d)
    sh1_p, sc1_p, gt1_p, sh2_p, sc2_p, gt2_p = (mod_p[:, :, k] for k in range(N_ADA))
    sh1_s, sc1_s, gt1_s, sh2_s, sc2_s, gt2_s = (mod_s[:, :, k] for k in range(N_ADA))

    xp = x_prompt.reshape(t_p, d)
    xs = x_sample.reshape(n_s, d)

    tm_big = _row_tile(seq, 1024)
    tm_mid = _row_tile(seq, 512)
    tm_huge = _row_tile(seq, 2048)
    za_p, zqkv_p, zg_p = _in_proj_all(xp, sc1_p, sh1_p, g_mix[l], w_in[l], rows_per_mod=seq, tm_norm=tm_big,
                                      tm=tm_huge, hi=False)
    b_p = _dil_attn(zqkv_p, bsz, seq)
    a_p = _chunk_gate(za_p, ln_g[l], ln_b[l], w_s[l], b_s[l], tm=tm_big)
    merged_p = _merge(a_p, b_p, zg_p, p_a[l], p_b[l], tm=tm_huge, hi=False)
    x1_p = _out_proj(merged_p, w_o[l], xp, gt1_p, rows_per_mod=seq, tm=tm_huge, hi=False)
    hp_p, e_p, w_p = _ffn_norm(x1_p, sc2_p, sh2_p, g_ffn[l], w_router[l], b_router[l], rows_per_mod=seq, tm=tm_mid,
                               hi=False)

    za_s, zqkv_s, zg_s = _in_proj_all(xs, sc1_s, sh1_s, g_mix[l], w_in[l], rows_per_mod=n_s, tm_norm=n_s, tm=n_s,
                                      hi=True)
    b_s_out = _sample_attn(zqkv_s, tuple(c[l] for c in caches))
    a_s, vn_s = _sample_gate(za_s, ln_g[l], ln_b[l], w_s[l], b_s[l])
    merged_s = _merge(a_s, b_s_out, zg_s, p_a[l], p_b[l], tm=n_s, hi=True)
    x1_s = _out_proj(merged_s, w_o[l], xs, gt1_s, rows_per_mod=n_s, tm=n_s, hi=True)
    hp_s, e_s, w_s_top = _ffn_norm(x1_s, sc2_s, sh2_s, g_ffn[l], w_router[l], b_router[l], rows_per_mod=n_s, tm=n_s,
                                   hi=True)

    assert t_p % ROUTE_TILE == 0
    t_all = t_p + n_s
    t_pad = -(-t_all // ROUTE_TILE) * ROUTE_TILE
    e_all = jnp.concatenate([e_p, e_s, jnp.full((t_pad - t_all, LANES), -1, I32)], axis=0)
    pos, cnt = _route(e_all)
    counts = cnt[0, :N_EXPERTS].astype(I32)
    n_tiles = (t_all * TOP_K + N_EXPERTS * (MOE_TILE - 1)) // MOE_TILE
    plan, pad_base, pad_cnt = _tile_table(counts, n_tiles)
    pos_p = pos[:t_p, :TOP_K]
    pos_s = pos[t_p:t_all, :TOP_K]
    x_sorted = _row_scatter(hp_p, hp_s, pos_p.reshape(-1), pos_s.reshape(-1), pad_base, pad_cnt,
                            n_tiles * MOE_TILE)
    y_rows = _moe_experts(x_sorted, plan, w_gate[l], b_gate[l], w_up[l], b_up[l], w_down[l], b_down[l])
    y_p = _moe_combine(y_rows, pos_p, w_p, x1_p, gt2_p, g_final, rows_per_mod=seq, tok=_row_tile(seq, 128))
    y_s = _moe_combine(y_rows, pos_s, w_s_top, x1_s, gt2_s, g_final, rows_per_mod=n_s, tok=n_s)

    k0 = N_GROUPS * GROUP_COLS
    v0 = k0 + N_GROUPS * GROUP_COLS
    z_p3 = zqkv_p.reshape(bsz, seq, 3 * N_GROUPS * GROUP_COLS)
    z_s = zqkv_s
    kv_prompt, kv_sample = [], []
    for g, (win, dil) in enumerate(DIL_GROUPS):
        keep = min(win, seq)
        kc = slice(k0 + g * GROUP_COLS, k0 + (g + 1) * GROUP_COLS)
        vc = slice(v0 + g * GROUP_COLS, v0 + (g + 1) * GROUP_COLS)
        kv = jnp.stack([z_p3[:, seq - keep:, kc], z_p3[:, seq - keep:, vc]], axis=2)
        kv_prompt.append(kv.reshape(1, bsz, keep, 2, HEADS_PER_GROUP, HEAD_DIM))
        kvs = jnp.stack([z_s[:, kc], z_s[:, vc]], axis=1)
        kv_sample.append(kvs.reshape(1, n_s, 1, 2, HEADS_PER_GROUP, HEAD_DIM))
    return (y_p.reshape(bsz, seq, d), y_s.reshape(n_s, 1, d),
            kv_prompt[0], kv_prompt[1], kv_prompt[2],
            kv_sample[0], kv_sample[1], kv_sample[2],
            vn_s.reshape(1, n_s, 1, aw))
```
